```python
import math
import jax, jax.numpy as jnp
from jax import lax
import numpy as np

D_MODEL = 1024
BATCH = 32
SEQ = 2048
DEPTH = 1

D_MIX = D_MODEL
D_HYENA = D_MIX // 2
HYENA_GROUPS = 8
D_ATTN = D_MIX - D_HYENA
N_HEADS = 8
HEAD_DIM = D_ATTN // N_HEADS
N_KV_HEADS = 2
Q_PER_KV = N_HEADS // N_KV_HEADS
KV_DIM = N_KV_HEADS * HEAD_DIM
WINDOW = 128
BLOCK = 128
SHORT_CONV = 3
FILTER_ORDER = 64
N_BANDS = 16
POS_EMB_DIM = 1 + 2 * N_BANDS
DECAY_TARGET = 1e-2
FAST_DECAY_PCT = 0.3
SLOW_DECAY_PCT = 1.5
RMS_EPS = 1e-6
NEG_INF = -1e30
D_IN = 3 * D_HYENA + D_HYENA + D_ATTN + 2 * KV_DIM + D_ATTN

kernel_name = "hybrid_hyena_swa_alibi_sandwich"


def rmsnorm(x, g):
    xf = x.astype(jnp.float32)
    xf = xf * lax.rsqrt(jnp.mean(xf * xf, axis=-1, keepdims=True) + RMS_EPS)
    return (xf * g.astype(jnp.float32)).astype(x.dtype)


def short_conv(u, w, b):
    L = u.shape[1]
    half = SHORT_CONV // 2
    up = jnp.pad(u, ((0, 0), (half, SHORT_CONV - 1 - half), (0, 0)))
    y = up[:, 0:L] * w[0]
    for j in range(1, SHORT_CONV):
        y = y + up[:, j:j + L] * w[j]
    return y + b


def hyena_filter(L, w_f1, b_f1, w_f2, b_f2, w_f3, b_f3, w_f4, sin_freq):
    f32 = jnp.float32
    t_np = np.arange(L, dtype=np.float32)
    t_norm_np = t_np / np.float32(max(L - 1, 1))
    w_np = np.float32(2.0 * math.pi) * t_np / np.float32(L)
    bands_np = np.linspace(1e-4, N_BANDS - 1, N_BANDS).astype(np.float32)
    ang_np = w_np[:, None] * bands_np[None, :]
    z = jnp.asarray(np.concatenate([t_norm_np[:, None], np.cos(ang_np), -np.sin(ang_np)], axis=-1))
    fr = sin_freq.astype(f32)
    h = jnp.sin(fr[0] * (z @ w_f1.astype(f32) + b_f1.astype(f32)))
    h = jnp.sin(fr[1] * (h @ w_f2.astype(f32) + b_f2.astype(f32)))
    h = jnp.sin(fr[2] * (h @ w_f3.astype(f32) + b_f3.astype(f32)))
    h = (h @ w_f4.astype(f32)).reshape(L, 2, D_HYENA)
    min_decay = math.log(DECAY_TARGET) / SLOW_DECAY_PCT
    max_decay = math.log(DECAY_TARGET) / FAST_DECAY_PCT
    deltas_np = np.abs(np.linspace(min_decay, max_decay, D_HYENA)).astype(np.float32)
    decay = jnp.asarray(np.exp(-t_norm_np[:, None] * deltas_np[None, :]).astype(np.float32))
    h = h * decay[:, None, :]
    h_fwd = h[:, 0, :]
    h_bwd = h[1:, 1, :]
    k = jnp.concatenate([h_fwd, jnp.zeros((1, D_HYENA), f32), h_bwd[::-1]], axis=0)
    return k * lax.rsqrt(jnp.sum(k * k, axis=0, keepdims=True) + 1e-12)


def hyena_mixer(u3, w_short, b_short, filt, hyena_d):
    L = u3.shape[1]
    uc = short_conv(u3, w_short, b_short)
    x0 = uc[..., :D_HYENA]
    x1 = uc[..., D_HYENA:2 * D_HYENA]
    v = uc[..., 2 * D_HYENA:]
    v = (v * x1).astype(jnp.float32)
    vf = jnp.fft.rfft(v, n=2 * L, axis=1)
    kf = jnp.fft.rfft(filt, axis=0)
    y = jnp.fft.irfft(vf * kf[None], n=2 * L, axis=1)[:, :L]
    y = (y + v * hyena_d.astype(jnp.float32)).astype(u3.dtype)
    return y * x0


def alibi_slopes_np():
    return np.exp2(-8.0 * np.arange(1, N_HEADS + 1, dtype=np.float32) / N_HEADS).astype(np.float32)


def windowed_attention(q, k, v, sink):
    B, L, _ = q.shape
    nb = L // BLOCK
    span = BLOCK + 2 * WINDOW
    scale = HEAD_DIM ** -0.5
    q5 = (q * scale).reshape(B, L, N_KV_HEADS, Q_PER_KV, HEAD_DIM)
    pad = ((0, 0), (WINDOW, WINDOW), (0, 0), (0, 0))
    k_pad = jnp.pad(k.reshape(B, L, N_KV_HEADS, HEAD_DIM), pad)
    v_pad = jnp.pad(v.reshape(B, L, N_KV_HEADS, HEAD_DIM), pad)
    slope = jnp.asarray(alibi_slopes_np().reshape(N_KV_HEADS, Q_PER_KV))
    sink_f = sink.astype(jnp.float32).reshape(N_KV_HEADS, Q_PER_KV)
    sk = sink_f[None, :, :, None]
    outs = []
    for i in range(nb):
        start = i * BLOCK
        qb = q5[:, start:start + BLOCK]
        kb = k_pad[:, start:start + span]
        vb = v_pad[:, start:start + span]
        q_pos = start + np.arange(BLOCK)
        k_pos = start - WINDOW + np.arange(span)
        rel_np = np.abs(k_pos[None, :] - q_pos[:, None])
        valid = jnp.asarray((rel_np <= WINDOW) & (k_pos >= 0)[None, :] & (k_pos < L)[None, :])
        rel = jnp.asarray(rel_np.astype(np.float32))
        s = jnp.einsum('bqkgd,bskd->bkgqs', qb, kb).astype(jnp.float32)
        s = s - slope[None, :, :, None, None] * rel
        s = jnp.where(valid, s, NEG_INF)
        m = jnp.maximum(jnp.max(s, axis=-1), sk)
        p = jnp.exp(s - m[..., None])
        den = jnp.sum(p, axis=-1) + jnp.exp(sk - m)
        o = jnp.einsum('bkgqs,bskd->bqkgd', p.astype(vb.dtype), vb)
        outs.append(o / jnp.transpose(den, (0, 3, 1, 2))[..., None].astype(o.dtype))
    out = jnp.concatenate(outs, axis=1)
    return out.reshape(B, L, D_ATTN)


def setup_inputs(seed: int = 0) -> dict:
    key = jax.random.key(seed)
    ks = jax.random.split(key, 20)
    nrm = jax.random.normal
    f32 = jnp.float32
    return {
        "x": nrm(ks[0], (BATCH, SEQ, D_MODEL), f32),
        "pre_g": 1.0 + 0.05 * nrm(ks[1], (DEPTH, D_MODEL), f32),
        "w_in": nrm(ks[2], (DEPTH, D_MODEL, D_IN), f32) * D_MODEL ** -0.5,
        "w_short": nrm(ks[3], (DEPTH, SHORT_CONV, 3 * D_HYENA), f32) * SHORT_CONV ** -0.5,
        "b_short": 0.02 * nrm(ks[4], (DEPTH, 3 * D_HYENA), f32),
        "w_f1": nrm(ks[5], (DEPTH, POS_EMB_DIM, FILTER_ORDER), f32) * POS_EMB_DIM ** -0.5,
        "b_f1": 0.1 * nrm(ks[6], (DEPTH, FILTER_ORDER), f32),
        "w_f2": nrm(ks[7], (DEPTH, FILTER_ORDER, FILTER_ORDER), f32) * FILTER_ORDER ** -0.5,
        "b_f2": 0.1 * nrm(ks[8], (DEPTH, FILTER_ORDER), f32),
        "w_f3": nrm(ks[9], (DEPTH, FILTER_ORDER, FILTER_ORDER), f32) * FILTER_ORDER ** -0.5,
        "b_f3": 0.1 * nrm(ks[10], (DEPTH, FILTER_ORDER), f32),
        "w_f4": nrm(ks[11], (DEPTH, FILTER_ORDER, 2 * D_HYENA), f32) * FILTER_ORDER ** -0.5,
        "sin_freq": 1.0 + 0.1 * nrm(ks[12], (DEPTH, 3, FILTER_ORDER), f32),
        "hyena_d": nrm(ks[13], (DEPTH, D_HYENA), f32),
        "attn_sink": 0.5 * nrm(ks[14], (DEPTH, N_HEADS), f32),
        "w_out": nrm(ks[15], (DEPTH, D_MIX, D_MODEL), f32) * D_MIX ** -0.5,
        "post_g": 1.0 + 0.05 * nrm(ks[16], (DEPTH, D_MODEL), f32),
    }


def reference(x, pre_g, w_in, w_short, b_short, w_f1, b_f1, w_f2, b_f2, w_f3, b_f3,
              w_f4, sin_freq, hyena_d, attn_sink, w_out, post_g):
    L = x.shape[1]
    o_hg = 3 * D_HYENA
    o_q = o_hg + D_HYENA
    o_k = o_q + D_ATTN
    o_v = o_k + KV_DIM
    o_ag = o_v + KV_DIM
    for l in range(DEPTH):
        h = rmsnorm(x, pre_g[l])
        z = h @ w_in[l]
        u_h, g_h = z[..., :o_hg], z[..., o_hg:o_q]
        q, k, v = z[..., o_q:o_k], z[..., o_k:o_v], z[..., o_v:o_ag]
        g_a = z[..., o_ag:]
        filt = hyena_filter(L, w_f1[l], b_f1[l], w_f2[l], b_f2[l], w_f3[l], b_f3[l],
                            w_f4[l], sin_freq[l])
        y_h = hyena_mixer(u_h, w_short[l], b_short[l], filt, hyena_d[l]) * jax.nn.silu(g_h)
        y_a = windowed_attention(q, k, v, attn_sink[l]) * jax.nn.silu(g_a)
        y = jnp.concatenate([y_h, y_a], axis=-1) @ w_out[l]
        x = x + rmsnorm(y, post_g[l])
    return x
```

```python
import functools
import math

import jax
import jax.numpy as jnp
import numpy as np
from jax import lax
from jax.experimental import pallas as pl
from jax.experimental.pallas import tpu as pltpu

D_MODEL = 1024
D_HYENA = 512
D_ATTN = 512
N_HEADS = 8
HEAD_DIM = 64
N_KV_HEADS = 2
Q_PER_KV = N_HEADS // N_KV_HEADS
KV_DIM = N_KV_HEADS * HEAD_DIM
WINDOW = 128
BLOCK = 128
FILTER_ORDER = 64
N_BANDS = 16
POS_EMB_DIM = 1 + 2 * N_BANDS
POS_EMB_PAD = 40
DECAY_TARGET = 1e-2
FAST_DECAY_PCT = 0.3
SLOW_DECAY_PCT = 1.5
RMS_EPS = 1e-6
NEG_INF = -1e30

N_HY = 3 * D_HYENA
O_GH = N_HY
O_ATT = O_GH + D_HYENA
N_ATT = 2 * D_ATTN + 2 * KV_DIM
D_IN = O_ATT + N_ATT

TBLK = 256
SHIFT_ROWS = 128

VMEM_LIMIT = 56 * 1024 * 1024

_HI = lax.Precision.HIGHEST


def _filter_consts(L):
    t = np.arange(L, dtype=np.float32)
    t_norm = t / np.float32(max(L - 1, 1))
    w = np.float32(2.0 * math.pi) * t / np.float32(L)
    bands = np.linspace(1e-4, N_BANDS - 1, N_BANDS).astype(np.float32)
    ang = w[:, None] * bands[None, :]
    z = np.concatenate([t_norm[:, None], np.cos(ang), -np.sin(ang)], axis=-1)
    min_decay = math.log(DECAY_TARGET) / SLOW_DECAY_PCT
    max_decay = math.log(DECAY_TARGET) / FAST_DECAY_PCT
    deltas = np.abs(np.linspace(min_decay, max_decay, D_HYENA)).astype(np.float32)
    decay = np.exp(-t_norm[:, None] * deltas[None, :]).astype(np.float32)
    idx = (L - np.arange(L)) % L
    z_rev = z[idx]
    decay_rev = decay[idx]
    pad = np.zeros((POS_EMB_PAD - POS_EMB_DIM, L), np.float32)
    zt = np.concatenate([z.T, pad], axis=0)
    zt_rev = np.concatenate([z_rev.T, pad], axis=0)
    return zt, zt_rev, np.ascontiguousarray(decay.T), np.ascontiguousarray(decay_rev.T)


def _filter_kernel(zt_ref, ztr_ref, dec_ref, decr_ref, w1_ref, b1_ref, w2_ref, b2_ref,
                   w3_ref, b3_ref, w4_ref, fr_ref, k_ref):
    L = zt_ref.shape[1]

    def mlp(z):
        h = jnp.dot(w1_ref[...], z, precision=_HI, preferred_element_type=jnp.float32)
        h = jnp.sin(fr_ref[:, 0:1] * (h + b1_ref[...]))
        h = jnp.dot(w2_ref[...], h, precision=_HI, preferred_element_type=jnp.float32)
        h = jnp.sin(fr_ref[:, 1:2] * (h + b2_ref[...]))
        h = jnp.dot(w3_ref[...], h, precision=_HI, preferred_element_type=jnp.float32)
        return jnp.sin(fr_ref[:, 2:3] * (h + b3_ref[...]))

    hf = jnp.dot(w4_ref[0:D_HYENA, :], mlp(zt_ref[...]), precision=_HI,
                 preferred_element_type=jnp.float32) * dec_ref[...]
    hb = jnp.dot(w4_ref[D_HYENA:, :], mlp(ztr_ref[...]), precision=_HI,
                 preferred_element_type=jnp.float32) * decr_ref[...]
    col = lax.broadcasted_iota(jnp.int32, hb.shape, 1)
    hb = jnp.where(col == 0, 0.0, hb)
    ss = jnp.sum(hf * hf, axis=1, keepdims=True) + jnp.sum(hb * hb, axis=1, keepdims=True)
    inv = lax.rsqrt(ss + 1e-12)
    k_ref[:, 0:L] = hf * inv
    k_ref[:, L:2 * L] = hb * inv


def _hyena_filter(L, w_f1, b_f1, w_f2, b_f2, w_f3, b_f3, w_f4, sin_freq):
    zt, ztr, dec, decr = _filter_consts(L)
    w1t = jnp.pad(w_f1.T, ((0, 0), (0, POS_EMB_PAD - POS_EMB_DIM)))
    args = (jnp.asarray(zt), jnp.asarray(ztr), jnp.asarray(dec), jnp.asarray(decr),
            w1t, b_f1[:, None], w_f2.T, b_f2[:, None], w_f3.T, b_f3[:, None],
            w_f4.T, sin_freq.T)
    return pl.pallas_call(
        _filter_kernel,
        out_shape=jax.ShapeDtypeStruct((D_HYENA, 2 * L), jnp.float32),
        compiler_params=pltpu.CompilerParams(vmem_limit_bytes=VMEM_LIMIT),
        name="hyena_filter",
    )(*args)


def _inproj_kernel(xp_ref, x_ref, xn_ref, g_ref, w_ref, wsh_ref, bsh_ref,
                   v_ref, x0g_ref, za_ref):
    t = pl.program_id(1)
    nt = pl.num_programs(1)
    tm = x_ref.shape[1]

    def norm(xv):
        ms = jnp.mean(xv * xv, axis=-1, keepdims=True)
        return (xv * lax.rsqrt(ms + RMS_EPS) * g_ref[...]).astype(jnp.bfloat16)

    z = jnp.dot(norm(x_ref[0]), w_ref[...], preferred_element_type=jnp.float32)
    zp = jnp.dot(norm(xp_ref[0]), w_ref[:, 0:N_HY], preferred_element_type=jnp.float32)
    zn = jnp.dot(norm(xn_ref[0]), w_ref[:, 0:N_HY], preferred_element_type=jnp.float32)
    zp_last = jnp.where(t > 0, zp[7:8, :], 0.0)
    zn_first = jnp.where(t < nt - 1, zn[0:1, :], 0.0)

    u = z[:, 0:N_HY]
    row = lax.broadcasted_iota(jnp.int32, u.shape, 0)
    u_prev = jnp.where(row == 0, zp_last, pltpu.roll(u, 1, axis=0))
    u_next = jnp.where(row == tm - 1, zn_first, pltpu.roll(u, tm - 1, axis=0))
    uc = (u_prev * wsh_ref[0:1, :] + u * wsh_ref[1:2, :] + u_next * wsh_ref[2:3, :]
          + bsh_ref[...])
    x0 = uc[:, 0:D_HYENA]
    x1 = uc[:, D_HYENA:2 * D_HYENA]
    vv = uc[:, 2 * D_HYENA:]
    gh = z[:, O_GH:O_ATT]
    v_ref[0] = (vv * x1).astype(v_ref.dtype)
    x0g_ref[0] = (x0 * (gh * jax.nn.sigmoid(gh))).astype(x0g_ref.dtype)
    za_ref[0] = z[:, O_ATT:].astype(za_ref.dtype)


def _inproj(x, pre_g, w_cat, w_short, b_short, tm=512):
    B, L, D = x.shape
    rb = tm // 8
    nrb = L // 8
    grid = (B, L // tm)
    return pl.pallas_call(
        _inproj_kernel,
        grid=grid,
        in_specs=[
            pl.BlockSpec((1, 8, D), lambda b, t: (b, jnp.maximum(t * rb - 1, 0), 0)),
            pl.BlockSpec((1, tm, D), lambda b, t: (b, t, 0)),
            pl.BlockSpec((1, 8, D), lambda b, t: (b, jnp.minimum((t + 1) * rb, nrb - 1), 0)),
            pl.BlockSpec((1, D), lambda b, t: (0, 0)),
            pl.BlockSpec((D, D_IN), lambda b, t: (0, 0)),
            pl.BlockSpec((3, N_HY), lambda b, t: (0, 0)),
            pl.BlockSpec((1, N_HY), lambda b, t: (0, 0)),
        ],
        out_specs=[
            pl.BlockSpec((1, tm, D_HYENA), lambda b, t: (b, t, 0)),
            pl.BlockSpec((1, tm, D_HYENA), lambda b, t: (b, t, 0)),
            pl.BlockSpec((1, tm, N_ATT), lambda b, t: (b, t, 0)),
        ],
        out_shape=[
            jax.ShapeDtypeStruct((B, L, D_HYENA), jnp.bfloat16),
            jax.ShapeDtypeStruct((B, L, D_HYENA), jnp.bfloat16),
            jax.ShapeDtypeStruct((B, L, N_ATT), jnp.bfloat16),
        ],
        compiler_params=pltpu.CompilerParams(
            dimension_semantics=("parallel", "arbitrary"), vmem_limit_bytes=VMEM_LIMIT),
        name="inproj",
    )(x, x, x, pre_g[None, :], w_cat, w_short, b_short[None, :])


def _alibi_slopes():
    return [float(v) for v in
            np.exp2(-8.0 * np.arange(1, N_HEADS + 1, dtype=np.float32) / N_HEADS).astype(np.float32)]


def _attn_kernel(sink_ref, za_ref, ya_ref, kpad_ref, vpad_ref):
    L = za_ref.shape[1]
    nb = L // BLOCK
    span = BLOCK + 2 * WINDOW
    o_k = 2 * D_ATTN
    o_v = o_k + KV_DIM
    zeros = jnp.zeros((WINDOW, KV_DIM), kpad_ref.dtype)
    kpad_ref[0:WINDOW, :] = zeros
    kpad_ref[WINDOW + L:, :] = zeros
    vpad_ref[0:WINDOW, :] = zeros
    vpad_ref[WINDOW + L:, :] = zeros
    kpad_ref[WINDOW:WINDOW + L, :] = za_ref[0, :, o_k:o_k + KV_DIM]
    vpad_ref[WINDOW:WINDOW + L, :] = za_ref[0, :, o_v:o_v + KV_DIM]

    slopes = _alibi_slopes()
    scale = HEAD_DIM ** -0.5
    r = lax.broadcasted_iota(jnp.int32, (BLOCK, span), 0)
    c = lax.broadcasted_iota(jnp.int32, (BLOCK, span), 1)
    rel_i = jnp.abs(c - WINDOW - r)
    rel = rel_i.astype(jnp.float32)
    in_window = rel_i <= WINDOW

    def body(i, carry):
        start = pl.multiple_of(i * BLOCK, BLOCK)
        qblk = za_ref[0, pl.ds(start, BLOCK), 0:D_ATTN]
        gblk = za_ref[0, pl.ds(start, BLOCK), D_ATTN:2 * D_ATTN].astype(jnp.float32)
        kw = kpad_ref[pl.ds(start, span), :]
        vw = vpad_ref[pl.ds(start, span), :]
        kpos = c + (start - WINDOW)
        valid = in_window & (kpos >= 0) & (kpos < L)
        outs = []
        for h in range(N_HEADS):
            kv = h // Q_PER_KV
            q_h = qblk[:, h * HEAD_DIM:(h + 1) * HEAD_DIM]
            k_h = kw[:, kv * HEAD_DIM:(kv + 1) * HEAD_DIM]
            v_h = vw[:, kv * HEAD_DIM:(kv + 1) * HEAD_DIM]
            s = lax.dot_general(q_h, k_h, (((1,), (1,)), ((), ())),
                                preferred_element_type=jnp.float32)
            s = s * scale - slopes[h] * rel
            s = jnp.where(valid, s, NEG_INF)
            sink = sink_ref[h]
            m = jnp.maximum(jnp.max(s, axis=-1, keepdims=True), sink)
            p = jnp.exp(s - m)
            den = jnp.sum(p, axis=-1, keepdims=True) + jnp.exp(sink - m)
            o = jnp.dot(p.astype(v_h.dtype), v_h, preferred_element_type=jnp.float32)
            outs.append(o / den)
        y = jnp.concatenate(outs, axis=-1)
        ya_ref[0, pl.ds(start, BLOCK), :] = (y * (gblk * jax.nn.sigmoid(gblk))).astype(ya_ref.dtype)
        return carry

    lax.fori_loop(0, nb, body, 0)


def _attention(za, sink):
    B, L, _ = za.shape
    return pl.pallas_call(
        _attn_kernel,
        grid=(B,),
        in_specs=[
            pl.BlockSpec(memory_space=pltpu.SMEM),
            pl.BlockSpec((1, L, N_ATT), lambda b: (b, 0, 0)),
        ],
        out_specs=pl.BlockSpec((1, L, D_ATTN), lambda b: (b, 0, 0)),
        out_shape=jax.ShapeDtypeStruct((B, L, D_ATTN), jnp.bfloat16),
        scratch_shapes=[
            pltpu.VMEM((L + 2 * WINDOW, KV_DIM), jnp.bfloat16),
            pltpu.VMEM((L + 2 * WINDOW, KV_DIM), jnp.bfloat16),
        ],
        compiler_params=pltpu.CompilerParams(
            dimension_semantics=("parallel",), vmem_limit_bytes=VMEM_LIMIT),
        name="swa_attention",
    )(sink, za)


def _hyena_kernel(k_ref, d_ref, v_ref, y_ref, s_ref, acc_ref):
    cg = v_ref.shape[0]
    rows = v_ref.shape[1]
    n2 = k_ref.shape[1]
    nblk = n2 // (2 * TBLK)
    bsz = rows // nblk
    off = TBLK * (nblk - 1) + SHIFT_ROWS

    for ci in range(cg):
        kb = jnp.broadcast_to(k_ref[ci:ci + 1, :], (SHIFT_ROWS, n2))
        s_ref[...] = pltpu.roll(kb, off, axis=1, stride=1, stride_axis=0).astype(s_ref.dtype)
        acc_ref[...] = d_ref[ci:ci + 1, :] * v_ref[ci].astype(jnp.float32)
        for d in range(-(nblk - 1), nblk):
            x0 = TBLK * (d + nblk - 1)
            w = jnp.concatenate(
                [s_ref[:, x0 + SHIFT_ROWS:x0 + SHIFT_ROWS + TBLK], s_ref[:, x0:x0 + TBLK]], axis=0)
            n = (nblk - abs(d)) * bsz
            src = 0 if d >= 0 else -d * bsz
            dst = d * bsz if d >= 0 else 0
            acc_ref[dst:dst + n, :] += jnp.dot(v_ref[ci, src:src + n, :], w,
                                               preferred_element_type=jnp.float32)
        y_ref[ci] = acc_ref[...].astype(y_ref.dtype)


def _hyena_conv(kt, hyena_d, v_rows, cg=8):
    C, rows, _ = v_rows.shape
    n2 = kt.shape[1]
    return pl.pallas_call(
        _hyena_kernel,
        grid=(C // cg,),
        in_specs=[
            pl.BlockSpec((cg, n2), lambda c: (c, 0)),
            pl.BlockSpec((cg, 1), lambda c: (c, 0)),
            pl.BlockSpec((cg, rows, TBLK), lambda c: (c, 0, 0)),
        ],
        out_specs=pl.BlockSpec((cg, rows, TBLK), lambda c: (c, 0, 0)),
        out_shape=jax.ShapeDtypeStruct((C, rows, TBLK), jnp.float32),
        scratch_shapes=[
            pltpu.VMEM((SHIFT_ROWS, n2), jnp.bfloat16),
            pltpu.VMEM((rows, TBLK), jnp.float32),
        ],
        compiler_params=pltpu.CompilerParams(
            dimension_semantics=("parallel",), vmem_limit_bytes=VMEM_LIMIT),
        name="hyena_conv",
    )(kt, hyena_d[:, None], v_rows)


def _outproj_kernel(x_ref, yh_ref, x0g_ref, ya_ref, w_ref, g_ref, o_ref):
    yh = (yh_ref[0] * x0g_ref[0].astype(jnp.float32)).astype(jnp.bfloat16)
    y = jnp.dot(yh, w_ref[0:D_HYENA, :], preferred_element_type=jnp.float32)
    y = y + jnp.dot(ya_ref[0], w_ref[D_HYENA:, :], preferred_element_type=jnp.float32)
    ms = jnp.mean(y * y, axis=-1, keepdims=True)
    o_ref[0] = x_ref[0] + y * lax.rsqrt(ms + RMS_EPS) * g_ref[...]


def _outproj(x, yh, x0g, ya, w_out, post_g, tm=512):
    B, L, D = x.shape
    return pl.pallas_call(
        _outproj_kernel,
        grid=(B, L // tm),
        in_specs=[
            pl.BlockSpec((1, tm, D), lambda b, t: (b, t, 0)),
            pl.BlockSpec((1, tm, D_HYENA), lambda b, t: (b, t, 0)),
            pl.BlockSpec((1, tm, D_HYENA), lambda b, t: (b, t, 0)),
            pl.BlockSpec((1, tm, D_ATTN), lambda b, t: (b, t, 0)),
            pl.BlockSpec((D_HYENA + D_ATTN, D), lambda b, t: (0, 0)),
            pl.BlockSpec((1, D), lambda b, t: (0, 0)),
        ],
        out_specs=pl.BlockSpec((1, tm, D), lambda b, t: (b, t, 0)),
        out_shape=jax.ShapeDtypeStruct((B, L, D), jnp.float32),
        compiler_params=pltpu.CompilerParams(
            dimension_semantics=("parallel", "parallel"), vmem_limit_bytes=VMEM_LIMIT),
        name="outproj",
    )(x, yh, x0g, ya, w_out, post_g[None, :])


def _layer(x, pre_g, w_in, w_short, b_short, w_f1, b_f1, w_f2, b_f2, w_f3, b_f3, w_f4,
           sin_freq, hyena_d, attn_sink, w_out, post_g):
    B, L, _ = x.shape
    nblk = L // TBLK
    o_q = 4 * D_HYENA
    o_k = o_q + D_ATTN
    o_v = o_k + KV_DIM
    o_ag = o_v + KV_DIM
    w_cat = jnp.concatenate(
        [w_in[:, :o_q], w_in[:, o_q:o_k], w_in[:, o_ag:], w_in[:, o_k:o_v], w_in[:, o_v:o_ag]],
        axis=1).astype(jnp.bfloat16)

    kt = _hyena_filter(L, w_f1, b_f1, w_f2, b_f2, w_f3, b_f3, w_f4, sin_freq)
    v, x0g, za = _inproj(x, pre_g, w_cat, w_short, b_short)
    ya = _attention(za, attn_sink)
    v_rows = v.reshape(B, nblk, TBLK, D_HYENA).transpose(3, 1, 0, 2).reshape(D_HYENA, nblk * B, TBLK)
    y_rows = _hyena_conv(kt, hyena_d, v_rows)
    yh = y_rows.reshape(D_HYENA, nblk, B, TBLK).transpose(2, 1, 3, 0).reshape(B, L, D_HYENA)
    return _outproj(x, yh, x0g, ya, w_out.astype(jnp.bfloat16), post_g)


def kernel(x, pre_g, w_in, w_short, b_short, w_f1, b_f1, w_f2, b_f2, w_f3, b_f3, w_f4, sin_freq, hyena_d, attn_sink, w_out, post_g):
    depth = pre_g.shape[0]
    for l in range(depth):
        x = _layer(x, pre_g[l], w_in[l], w_short[l], b_short[l], w_f1[l], b_f1[l], w_f2[l],
                   b_f2[l], w_f3[l], b_f3[l], w_f4[l], sin_freq[l], hyena_d[l], attn_sink[l],
                   w_out[l], post_g[l])
    return x
```

```python
import functools
import math

import jax
import jax.numpy as jnp
import numpy as np
from jax import lax
from jax.experimental import pallas as pl
from jax.experimental.pallas import tpu as pltpu

D_MODEL = 1024
D_HYENA = 512
D_ATTN = 512
N_HEADS = 8
HEAD_DIM = 64
N_KV_HEADS = 2
Q_PER_KV = N_HEADS // N_KV_HEADS
KV_DIM = N_KV_HEADS * HEAD_DIM
WINDOW = 128
BLOCK = 128
FILTER_ORDER = 64
N_BANDS = 16
POS_EMB_DIM = 1 + 2 * N_BANDS
POS_EMB_PAD = 40
DECAY_TARGET = 1e-2
FAST_DECAY_PCT = 0.3
SLOW_DECAY_PCT = 1.5
RMS_EPS = 1e-6
NEG_INF = -1e30
LOG2E = math.log2(math.e)
Q_SCALE_LOG2 = HEAD_DIM ** -0.5 * LOG2E

N_HY = 3 * D_HYENA
O_GH = N_HY
O_ATT = O_GH + D_HYENA
N_ATT = 2 * D_ATTN + 2 * KV_DIM
D_IN = O_ATT + N_ATT

TBLK = 256
SHIFT_ROWS = 128

VMEM_LIMIT = 56 * 1024 * 1024

_HI = lax.Precision.HIGHEST


def _filter_consts(L):
    t = np.arange(L, dtype=np.float32)
    t_norm = t / np.float32(max(L - 1, 1))
    w = np.float32(2.0 * math.pi) * t / np.float32(L)
    bands = np.linspace(1e-4, N_BANDS - 1, N_BANDS).astype(np.float32)
    ang = w[:, None] * bands[None, :]
    z = np.concatenate([t_norm[:, None], np.cos(ang), -np.sin(ang)], axis=-1)
    min_decay = math.log(DECAY_TARGET) / SLOW_DECAY_PCT
    max_decay = math.log(DECAY_TARGET) / FAST_DECAY_PCT
    deltas = np.abs(np.linspace(min_decay, max_decay, D_HYENA)).astype(np.float32)
    decay = np.exp(-t_norm[:, None] * deltas[None, :]).astype(np.float32)
    idx = (L - np.arange(L)) % L
    z_rev = z[idx]
    decay_rev = decay[idx]
    pad = np.zeros((POS_EMB_PAD - POS_EMB_DIM, L), np.float32)
    zt = np.concatenate([z.T, pad], axis=0)
    zt_rev = np.concatenate([z_rev.T, pad], axis=0)
    return zt, zt_rev, np.ascontiguousarray(decay.T), np.ascontiguousarray(decay_rev.T)


def _filter_kernel(zt_ref, ztr_ref, dec_ref, decr_ref, w1_ref, b1_ref, w2_ref, b2_ref,
                   w3_ref, b3_ref, w4_ref, fr_ref, k_ref):
    L = zt_ref.shape[1]

    def mlp(z):
        h = jnp.dot(w1_ref[...], z, precision=_HI, preferred_element_type=jnp.float32)
        h = jnp.sin(fr_ref[:, 0:1] * (h + b1_ref[...]))
        h = jnp.dot(w2_ref[...], h, precision=_HI, preferred_element_type=jnp.float32)
        h = jnp.sin(fr_ref[:, 1:2] * (h + b2_ref[...]))
        h = jnp.dot(w3_ref[...], h, precision=_HI, preferred_element_type=jnp.float32)
        return jnp.sin(fr_ref[:, 2:3] * (h + b3_ref[...]))

    hf = jnp.dot(w4_ref[0:D_HYENA, :], mlp(zt_ref[...]), precision=_HI,
                 preferred_element_type=jnp.float32) * dec_ref[...]
    hb = jnp.dot(w4_ref[D_HYENA:, :], mlp(ztr_ref[...]), precision=_HI,
                 preferred_element_type=jnp.float32) * decr_ref[...]
    col = lax.broadcasted_iota(jnp.int32, hb.shape, 1)
    hb = jnp.where(col == 0, 0.0, hb)
    ss = jnp.sum(hf * hf, axis=1, keepdims=True) + jnp.sum(hb * hb, axis=1, keepdims=True)
    inv = lax.rsqrt(ss + 1e-12)
    k_ref[:, 0:L] = hf * inv
    k_ref[:, L:2 * L] = hb * inv


def _hyena_filter(L, w_f1, b_f1, w_f2, b_f2, w_f3, b_f3, w_f4, sin_freq):
    zt, ztr, dec, decr = _filter_consts(L)
    w1t = jnp.pad(w_f1.T, ((0, 0), (0, POS_EMB_PAD - POS_EMB_DIM)))
    args = (jnp.asarray(zt), jnp.asarray(ztr), jnp.asarray(dec), jnp.asarray(decr),
            w1t, b_f1[:, None], w_f2.T, b_f2[:, None], w_f3.T, b_f3[:, None],
            w_f4.T, sin_freq.T)
    return pl.pallas_call(
        _filter_kernel,
        out_shape=jax.ShapeDtypeStruct((D_HYENA, 2 * L), jnp.float32),
        compiler_params=pltpu.CompilerParams(vmem_limit_bytes=VMEM_LIMIT),
        name="hyena_filter",
    )(*args)


def _inproj_kernel(xp_ref, x_ref, xn_ref, g_ref, w_ref, wsh_ref, bsh_ref,
                   v_ref, x0g_ref, za_ref):
    t = pl.program_id(1)
    nt = pl.num_programs(1)
    tm = x_ref.shape[1]

    def norm(xv):
        ms = jnp.mean(xv * xv, axis=-1, keepdims=True)
        return (xv * lax.rsqrt(ms + RMS_EPS) * g_ref[...]).astype(jnp.bfloat16)

    z = jnp.dot(norm(x_ref[0]), w_ref[...], preferred_element_type=jnp.float32)
    zp = jnp.dot(norm(xp_ref[0]), w_ref[:, 0:N_HY], preferred_element_type=jnp.float32)
    zn = jnp.dot(norm(xn_ref[0]), w_ref[:, 0:N_HY], preferred_element_type=jnp.float32)
    zp_last = jnp.where(t > 0, zp[7:8, :], 0.0)
    zn_first = jnp.where(t < nt - 1, zn[0:1, :], 0.0)

    u = z[:, 0:N_HY]
    row = lax.broadcasted_iota(jnp.int32, u.shape, 0)
    u_prev = jnp.where(row == 0, zp_last, pltpu.roll(u, 1, axis=0))
    u_next = jnp.where(row == tm - 1, zn_first, pltpu.roll(u, tm - 1, axis=0))
    uc = (u_prev * wsh_ref[0:1, :] + u * wsh_ref[1:2, :] + u_next * wsh_ref[2:3, :]
          + bsh_ref[...])
    x0 = uc[:, 0:D_HYENA]
    x1 = uc[:, D_HYENA:2 * D_HYENA]
    vv = uc[:, 2 * D_HYENA:]
    gh = z[:, O_GH:O_ATT]
    v_ref[0] = (vv * x1).astype(v_ref.dtype)
    x0g_ref[0] = (x0 * (gh * jax.nn.sigmoid(gh))).astype(x0g_ref.dtype)
    za_ref[0, :, 0:D_ATTN] = (z[:, O_ATT:O_ATT + D_ATTN] * Q_SCALE_LOG2).astype(za_ref.dtype)
    za_ref[0, :, D_ATTN:] = z[:, O_ATT + D_ATTN:].astype(za_ref.dtype)


def _inproj(x, pre_g, w_cat, w_short, b_short, tm=512):
    B, L, D = x.shape
    rb = tm // 8
    nrb = L // 8
    grid = (B, L // tm)
    return pl.pallas_call(
        _inproj_kernel,
        grid=grid,
        in_specs=[
            pl.BlockSpec((1, 8, D), lambda b, t: (b, jnp.maximum(t * rb - 1, 0), 0)),
            pl.BlockSpec((1, tm, D), lambda b, t: (b, t, 0)),
            pl.BlockSpec((1, 8, D), lambda b, t: (b, jnp.minimum((t + 1) * rb, nrb - 1), 0)),
            pl.BlockSpec((1, D), lambda b, t: (0, 0)),
            pl.BlockSpec((D, D_IN), lambda b, t: (0, 0)),
            pl.BlockSpec((3, N_HY), lambda b, t: (0, 0)),
            pl.BlockSpec((1, N_HY), lambda b, t: (0, 0)),
        ],
        out_specs=[
            pl.BlockSpec((1, tm, D_HYENA), lambda b, t: (b, t, 0)),
            pl.BlockSpec((1, tm, D_HYENA), lambda b, t: (b, t, 0)),
            pl.BlockSpec((1, tm, N_ATT), lambda b, t: (b, t, 0)),
        ],
        out_shape=[
            jax.ShapeDtypeStruct((B, L, D_HYENA), jnp.bfloat16),
            jax.ShapeDtypeStruct((B, L, D_HYENA), jnp.bfloat16),
            jax.ShapeDtypeStruct((B, L, N_ATT), jnp.bfloat16),
        ],
        compiler_params=pltpu.CompilerParams(
            dimension_semantics=("parallel", "arbitrary"), vmem_limit_bytes=VMEM_LIMIT),
        name="inproj",
    )(x, x, x, pre_g[None, :], w_cat, w_short, b_short[None, :])


def _alibi_slopes():
    return [float(v) for v in
            np.exp2(-8.0 * np.arange(1, N_HEADS + 1, dtype=np.float32) / N_HEADS).astype(np.float32)]


def _attn_bias_tables():
    c = np.arange(BLOCK)[:, None]
    r = np.arange(BLOCK)[None, :]
    dist = np.abs(c - r).astype(np.float32)
    slopes = np.asarray(_alibi_slopes(), np.float32)[:, None, None] * np.float32(LOG2E)
    return np.concatenate([-slopes * dist, -slopes * (WINDOW - dist)], axis=1).astype(np.float32)


def _attn_kernel(sink_ref, bias_ref, za_ref, ya_ref, kpl_ref, vt_ref, ot_ref):
    L = za_ref.shape[1]
    nb = L // BLOCK
    o_g = D_ATTN
    o_k = 2 * D_ATTN
    o_v = o_k + KV_DIM
    f32 = jnp.float32
    bf16 = jnp.bfloat16
    slopes = _alibi_slopes()
    group = Q_PER_KV * HEAD_DIM

    lane = lax.broadcasted_iota(jnp.int32, (BLOCK, KV_DIM), 1)
    lo = lane < HEAD_DIM
    ztile = jnp.zeros((BLOCK, KV_DIM), bf16)
    for idx in range(2 * N_KV_HEADS):
        kpl_ref[idx, 0] = ztile
        kpl_ref[idx, nb + 1] = ztile
    vt_ref[0] = ztile
    vt_ref[nb + 1] = ztile

    def prep(j, carry):
        start = pl.multiple_of(j * BLOCK, BLOCK)
        kc = za_ref[0, pl.ds(start, BLOCK), o_k:o_k + KV_DIM].astype(f32)
        ksw = pltpu.roll(kc, HEAD_DIM, axis=1)
        kpl_ref[0, j + 1] = jnp.where(lo, kc, 0.0).astype(bf16)
        kpl_ref[1, j + 1] = jnp.where(lo, 0.0, ksw).astype(bf16)
        kpl_ref[2, j + 1] = jnp.where(lo, ksw, 0.0).astype(bf16)
        kpl_ref[3, j + 1] = jnp.where(lo, 0.0, kc).astype(bf16)
        vc = za_ref[0, pl.ds(start, BLOCK), o_v:o_v + KV_DIM].astype(f32)
        vt_ref[j + 1] = vc.T.astype(bf16)
        return carry

    lax.fori_loop(0, nb, prep, 0)

    c_i = lax.broadcasted_iota(jnp.int32, (BLOCK, BLOCK), 0)
    r_i = lax.broadcasted_iota(jnp.int32, (BLOCK, BLOCK), 1)
    upper = c_i >= r_i
    diag = c_i == r_i

    def block(i, first, last):
        start = pl.multiple_of(i * BLOCK, BLOCK)
        qblk = za_ref[0, pl.ds(start, BLOCK), 0:D_ATTN]
        for kv in range(N_KV_HEADS):
            base = kv * group
            qs = jnp.concatenate([qblk[:, base:base + 2 * HEAD_DIM],
                                  qblk[:, base + 2 * HEAD_DIM:base + group]], axis=0)
            kp = jnp.concatenate([kpl_ref[2 * kv + half, i + t]
                                  for half in range(2) for t in range(3)], axis=0)
            s_all = lax.dot_general(kp, qs, (((1,), (1,)), ((), ())),
                                    preferred_element_type=f32)
            rows = slice(kv * HEAD_DIM, (kv + 1) * HEAD_DIM)
            vt3 = vt_ref[i + 2, rows, :]
            vtw = jnp.concatenate([vt_ref[i, rows, :], vt_ref[i + 1, rows, :], vt3], axis=1)
            pts, invs, pxs = [], [], []
            for g in range(Q_PER_KV):
                half, pair = g % 2, g // 2
                h = kv * Q_PER_KV + g
                blk = s_all[half * 3 * BLOCK:(half + 1) * 3 * BLOCK, pair * BLOCK:(pair + 1) * BLOCK]
                s1, s2, s3 = blk[0:BLOCK], blk[BLOCK:2 * BLOCK], blk[2 * BLOCK:]
                if first:
                    mrg = jnp.where(upper, NEG_INF, s3)
                elif last:
                    mrg = jnp.where(upper, s1, NEG_INF)
                else:
                    mrg = jnp.where(upper, s1, s3)
                a_mid = s2 + bias_ref[h, 0:BLOCK, :]
                a_mrg = mrg + bias_ref[h, BLOCK:, :]
                sink2 = sink_ref[h] * LOG2E
                m = jnp.max(jnp.maximum(a_mid, a_mrg), axis=0, keepdims=True)
                m = jnp.maximum(m, sink2)
                if not last:
                    s_x = (jnp.sum(jnp.where(diag, s3, 0.0), axis=0, keepdims=True)
                           - slopes[h] * LOG2E * WINDOW)
                    m = jnp.maximum(m, s_x)
                p_mid = jnp.exp2(a_mid - m)
                p_mrg = jnp.exp2(a_mrg - m)
                den = jnp.sum(p_mid + p_mrg, axis=0, keepdims=True) + jnp.exp2(sink2 - m)
                if not last:
                    p_x = jnp.exp2(s_x - m)
                    den = den + p_x
                    pxs.append(p_x)
                invs.append(1.0 / den)
                p1 = jnp.where(upper, p_mrg, 0.0)
                p3 = jnp.where(upper, 0.0, p_mrg)
                pts.append(jnp.concatenate([p1, p_mid, p3], axis=0).astype(bf16))
            pt = jnp.concatenate(pts, axis=1)
            ot = jnp.dot(vtw, pt, preferred_element_type=f32)
            for g in range(Q_PER_KV):
                h = kv * Q_PER_KV + g
                o = ot[:, g * BLOCK:(g + 1) * BLOCK]
                if not last:
                    o = o + vt3.astype(f32) * pxs[g]
                ot_ref[h * HEAD_DIM:(h + 1) * HEAD_DIM, :] = o * invs[g]
        gblk = za_ref[0, pl.ds(start, BLOCK), o_g:o_g + D_ATTN].astype(f32)
        y = ot_ref[...].T
        ya_ref[0, pl.ds(start, BLOCK), :] = (y * (gblk * jax.nn.sigmoid(gblk))).astype(ya_ref.dtype)

    def middle(i, carry):
        block(i, False, False)
        return carry

    block(0, True, False)
    lax.fori_loop(1, nb - 1, middle, 0)
    block(nb - 1, False, True)


def _attention(za, sink):
    B, L, _ = za.shape
    assert WINDOW == BLOCK and L // BLOCK >= 3
    nkt = L // BLOCK + 2
    bias = jnp.asarray(_attn_bias_tables())
    return pl.pallas_call(
        _attn_kernel,
        grid=(B,),
        in_specs=[
            pl.BlockSpec(memory_space=pltpu.SMEM),
            pl.BlockSpec((N_HEADS, 2 * BLOCK, BLOCK), lambda b: (0, 0, 0)),
            pl.BlockSpec((1, L, N_ATT), lambda b: (b, 0, 0)),
        ],
        out_specs=pl.BlockSpec((1, L, D_ATTN), lambda b: (b, 0, 0)),
        out_shape=jax.ShapeDtypeStruct((B, L, D_ATTN), jnp.bfloat16),
        scratch_shapes=[
            pltpu.VMEM((2 * N_KV_HEADS, nkt, BLOCK, KV_DIM), jnp.bfloat16),
            pltpu.VMEM((nkt, KV_DIM, BLOCK), jnp.bfloat16),
            pltpu.VMEM((D_ATTN, BLOCK), jnp.float32),
        ],
        compiler_params=pltpu.CompilerParams(
            dimension_semantics=("parallel",), vmem_limit_bytes=VMEM_LIMIT),
        name="swa_attention",
    )(sink, bias, za)


def _hyena_kernel(k_ref, d_ref, v_ref, y_ref, s_ref, acc_ref):
    cg = v_ref.shape[0]
    rows = v_ref.shape[1]
    n2 = k_ref.shape[1]
    nblk = n2 // (2 * TBLK)
    bsz = rows // nblk
    off = TBLK * (nblk - 1) + SHIFT_ROWS

    for ci in range(cg):
        kb = jnp.broadcast_to(k_ref[ci:ci + 1, :], (SHIFT_ROWS, n2))
        s_ref[...] = pltpu.roll(kb, off, axis=1, stride=1, stride_axis=0).astype(s_ref.dtype)
        acc_ref[...] = d_ref[ci:ci + 1, :] * v_ref[ci].astype(jnp.float32)
        for d in range(-(nblk - 1), nblk):
            x0 = TBLK * (d + nblk - 1)
            w = jnp.concatenate(
                [s_ref[:, x0 + SHIFT_ROWS:x0 + SHIFT_ROWS + TBLK], s_ref[:, x0:x0 + TBLK]], axis=0)
            n = (nblk - abs(d)) * bsz
            src = 0 if d >= 0 else -d * bsz
            dst = d * bsz if d >= 0 else 0
            acc_ref[dst:dst + n, :] += jnp.dot(v_ref[ci, src:src + n, :], w,
                                               preferred_element_type=jnp.float32)
        y_ref[ci] = acc_ref[...].astype(y_ref.dtype)


def _hyena_conv(kt, hyena_d, v_rows, cg=8):
    C, rows, _ = v_rows.shape
    n2 = kt.shape[1]
    return pl.pallas_call(
        _hyena_kernel,
        grid=(C // cg,),
        in_specs=[
            pl.BlockSpec((cg, n2), lambda c: (c, 0)),
            pl.BlockSpec((cg, 1), lambda c: (c, 0)),
            pl.BlockSpec((cg, rows, TBLK), lambda c: (c, 0, 0)),
        ],
        out_specs=pl.BlockSpec((cg, rows, TBLK), lambda c: (c, 0, 0)),
        out_shape=jax.ShapeDtypeStruct((C, rows, TBLK), jnp.float32),
        scratch_shapes=[
            pltpu.VMEM((SHIFT_ROWS, n2), jnp.bfloat16),
            pltpu.VMEM((rows, TBLK), jnp.float32),
        ],
        compiler_params=pltpu.CompilerParams(
            dimension_semantics=("parallel",), vmem_limit_bytes=VMEM_LIMIT),
        name="hyena_conv",
    )(kt, hyena_d[:, None], v_rows)


def _outproj_kernel(x_ref, yh_ref, x0g_ref, ya_ref, w_ref, g_ref, o_ref):
    yh = (yh_ref[0] * x0g_ref[0].astype(jnp.float32)).astype(jnp.bfloat16)
    y = jnp.dot(yh, w_ref[0:D_HYENA, :], preferred_element_type=jnp.float32)
    y = y + jnp.dot(ya_ref[0], w_ref[D_HYENA:, :], preferred_element_type=jnp.float32)
    ms = jnp.mean(y * y, axis=-1, keepdims=True)
    o_ref[0] = x_ref[0] + y * lax.rsqrt(ms + RMS_EPS) * g_ref[...]


def _outproj(x, yh, x0g, ya, w_out, post_g, tm=512):
    B, L, D = x.shape
    return pl.pallas_call(
        _outproj_kernel,
        grid=(B, L // tm),
        in_specs=[
            pl.BlockSpec((1, tm, D), lambda b, t: (b, t, 0)),
            pl.BlockSpec((1, tm, D_HYENA), lambda b, t: (b, t, 0)),
            pl.BlockSpec((1, tm, D_HYENA), lambda b, t: (b, t, 0)),
            pl.BlockSpec((1, tm, D_ATTN), lambda b, t: (b, t, 0)),
            pl.BlockSpec((D_HYENA + D_ATTN, D), lambda b, t: (0, 0)),
            pl.BlockSpec((1, D), lambda b, t: (0, 0)),
        ],
        out_specs=pl.BlockSpec((1, tm, D), lambda b, t: (b, t, 0)),
        out_shape=jax.ShapeDtypeStruct((B, L, D), jnp.float32),
        compiler_params=pltpu.CompilerParams(
            dimension_semantics=("parallel", "parallel"), vmem_limit_bytes=VMEM_LIMIT),
        name="outproj",
    )(x, yh, x0g, ya, w_out, post_g[None, :])


def _layer(x, pre_g, w_in, w_short, b_short, w_f1, b_f1, w_f2, b_f2, w_f3, b_f3, w_f4,
           sin_freq, hyena_d, attn_sink, w_out, post_g):
    B, L, _ = x.shape
    nblk = L // TBLK
    o_q = 4 * D_HYENA
    o_k = o_q + D_ATTN
    o_v = o_k + KV_DIM
    o_ag = o_v + KV_DIM
    w_cat = jnp.concatenate(
        [w_in[:, :o_q], w_in[:, o_q:o_k], w_in[:, o_ag:], w_in[:, o_k:o_v], w_in[:, o_v:o_ag]],
        axis=1).astype(jnp.bfloat16)

    kt = _hyena_filter(L, w_f1, b_f1, w_f2, b_f2, w_f3, b_f3, w_f4, sin_freq)
    v, x0g, za = _inproj(x, pre_g, w_cat, w_short, b_short)
    ya = _attention(za, attn_sink)
    v_rows = v.reshape(B, nblk, TBLK, D_HYENA).transpose(3, 1, 0, 2).reshape(D_HYENA, nblk * B, TBLK)
    y_rows = _hyena_conv(kt, hyena_d, v_rows)
    yh = y_rows.reshape(D_HYENA, nblk, B, TBLK).transpose(2, 1, 3, 0).reshape(B, L, D_HYENA)
    return _outproj(x, yh, x0g, ya, w_out.astype(jnp.bfloat16), post_g)


def kernel(x, pre_g, w_in, w_short, b_short, w_f1, b_f1, w_f2, b_f2, w_f3, b_f3, w_f4, sin_freq, hyena_d, attn_sink, w_out, post_g):
    depth = pre_g.shape[0]
    for l in range(depth):
        x = _layer(x, pre_g[l], w_in[l], w_short[l], b_short[l], w_f1[l], b_f1[l], w_f2[l],
                   b_f2[l], w_f3[l], b_f3[l], w_f4[l], sin_freq[l], hyena_d[l], attn_sink[l],
                   w_out[l], post_g[l])
    return x
```

```python
import functools
import math

import jax
import jax.numpy as jnp
import numpy as np
from jax import lax
from jax.experimental import pallas as pl
from jax.experimental.pallas import tpu as pltpu

D_MODEL = 1024
D_HYENA = 512
D_ATTN = 512
N_HEADS = 8
HEAD_DIM = 64
N_KV_HEADS = 2
Q_PER_KV = N_HEADS // N_KV_HEADS
KV_DIM = N_KV_HEADS * HEAD_DIM
WINDOW = 128
BLOCK = 128
FILTER_ORDER = 64
N_BANDS = 16
POS_EMB_DIM = 1 + 2 * N_BANDS
POS_EMB_PAD = 40
DECAY_TARGET = 1e-2
FAST_DECAY_PCT = 0.3
SLOW_DECAY_PCT = 1.5
RMS_EPS = 1e-6
NEG_INF = -1e30
LOG2E = math.log2(math.e)
Q_SCALE_LOG2 = HEAD_DIM ** -0.5 * LOG2E

N_HY = 3 * D_HYENA
O_GH = N_HY
O_ATT = O_GH + D_HYENA
N_ATT = 2 * D_ATTN + 2 * KV_DIM
D_IN = O_ATT + N_ATT

HALO = 16

TBLK = 256
SHIFT_ROWS = 128

VMEM_LIMIT = 56 * 1024 * 1024

_HI = lax.Precision.HIGHEST


def _filter_consts(L):
    t = np.arange(L, dtype=np.float32)
    t_norm = t / np.float32(max(L - 1, 1))
    w = np.float32(2.0 * math.pi) * t / np.float32(L)
    bands = np.linspace(1e-4, N_BANDS - 1, N_BANDS).astype(np.float32)
    ang = w[:, None] * bands[None, :]
    z = np.concatenate([t_norm[:, None], np.cos(ang), -np.sin(ang)], axis=-1)
    min_decay = math.log(DECAY_TARGET) / SLOW_DECAY_PCT
    max_decay = math.log(DECAY_TARGET) / FAST_DECAY_PCT
    deltas = np.abs(np.linspace(min_decay, max_decay, D_HYENA)).astype(np.float32)
    decay = np.exp(-t_norm[:, None] * deltas[None, :]).astype(np.float32)
    idx = (L - np.arange(L)) % L
    z_rev = z[idx]
    decay_rev = decay[idx]
    pad = np.zeros((POS_EMB_PAD - POS_EMB_DIM, L), np.float32)
    zt = np.concatenate([z.T, pad], axis=0)
    zt_rev = np.concatenate([z_rev.T, pad], axis=0)
    return zt, zt_rev, np.ascontiguousarray(decay.T), np.ascontiguousarray(decay_rev.T)


def _filter_kernel(zt_ref, ztr_ref, dec_ref, decr_ref, w1_ref, b1_ref, w2_ref, b2_ref,
                   w3_ref, b3_ref, w4_ref, fr_ref, k_ref):
    L = zt_ref.shape[1]

    def mlp(z):
        h = jnp.dot(w1_ref[...], z, precision=_HI, preferred_element_type=jnp.float32)
        h = jnp.sin(fr_ref[:, 0:1] * (h + b1_ref[...]))
        h = jnp.dot(w2_ref[...], h, precision=_HI, preferred_element_type=jnp.float32)
        h = jnp.sin(fr_ref[:, 1:2] * (h + b2_ref[...]))
        h = jnp.dot(w3_ref[...], h, precision=_HI, preferred_element_type=jnp.float32)
        return jnp.sin(fr_ref[:, 2:3] * (h + b3_ref[...]))

    hf = jnp.dot(w4_ref[0:D_HYENA, :], mlp(zt_ref[...]), precision=_HI,
                 preferred_element_type=jnp.float32) * dec_ref[...]
    hb = jnp.dot(w4_ref[D_HYENA:, :], mlp(ztr_ref[...]), precision=_HI,
                 preferred_element_type=jnp.float32) * decr_ref[...]
    col = lax.broadcasted_iota(jnp.int32, hb.shape, 1)
    hb = jnp.where(col == 0, 0.0, hb)
    ss = jnp.sum(hf * hf, axis=1, keepdims=True) + jnp.sum(hb * hb, axis=1, keepdims=True)
    inv = lax.rsqrt(ss + 1e-12)
    k_ref[:, 0:L] = hf * inv
    k_ref[:, L:2 * L] = hb * inv


def _hyena_filter(L, w_f1, b_f1, w_f2, b_f2, w_f3, b_f3, w_f4, sin_freq):
    zt, ztr, dec, decr = _filter_consts(L)
    w1t = jnp.pad(w_f1.T, ((0, 0), (0, POS_EMB_PAD - POS_EMB_DIM)))
    args = (jnp.asarray(zt), jnp.asarray(ztr), jnp.asarray(dec), jnp.asarray(decr),
            w1t, b_f1[:, None], w_f2.T, b_f2[:, None], w_f3.T, b_f3[:, None],
            w_f4.T, sin_freq.T)
    return pl.pallas_call(
        _filter_kernel,
        out_shape=jax.ShapeDtypeStruct((D_HYENA, 2 * L), jnp.float32),
        compiler_params=pltpu.CompilerParams(vmem_limit_bytes=VMEM_LIMIT),
        name="hyena_filter",
    )(*args)


def _inproj_kernel(xp_ref, x_ref, xn_ref, g_ref, w_ref, wsh_ref, bsh_ref,
                   v_ref, x0g_ref, za_ref):
    t = pl.program_id(1)
    nt = pl.num_programs(1)
    tm = x_ref.shape[1]

    def norm(xv):
        ms = jnp.mean(xv * xv, axis=-1, keepdims=True)
        return (xv * lax.rsqrt(ms + RMS_EPS) * g_ref[...]).astype(jnp.bfloat16)

    h = norm(x_ref[0])
    hp = jnp.where(t > 0, norm(xp_ref[0]), 0.0).astype(jnp.bfloat16)
    hn = jnp.where(t < nt - 1, norm(xn_ref[0]), 0.0).astype(jnp.bfloat16)
    hall = jnp.concatenate([hp, h, hn], axis=0)
    zh = jnp.dot(hall, w_ref[:, 0:N_HY], preferred_element_type=jnp.float32)
    z = jnp.dot(h, w_ref[:, N_HY:], preferred_element_type=jnp.float32)

    rows = tm + 2 * HALO
    u_prev = pltpu.roll(zh, 1, axis=0)[HALO:HALO + tm]
    u = zh[HALO:HALO + tm]
    u_next = pltpu.roll(zh, rows - 1, axis=0)[HALO:HALO + tm]
    uc = (u_prev * wsh_ref[0:1, :] + u * wsh_ref[1:2, :] + u_next * wsh_ref[2:3, :]
          + bsh_ref[...])
    x0 = uc[:, 0:D_HYENA]
    x1 = uc[:, D_HYENA:2 * D_HYENA]
    vv = uc[:, 2 * D_HYENA:]
    gh = z[:, 0:D_HYENA]
    v_ref[0] = (vv * x1).astype(v_ref.dtype)
    x0g_ref[0] = (x0 * (gh * jax.nn.sigmoid(gh))).astype(x0g_ref.dtype)
    o_q = O_ATT - N_HY
    za_ref[0, :, 0:D_ATTN] = (z[:, o_q:o_q + D_ATTN] * Q_SCALE_LOG2).astype(za_ref.dtype)
    za_ref[0, :, D_ATTN:] = z[:, o_q + D_ATTN:].astype(za_ref.dtype)


def _inproj(x, pre_g, w_cat, w_short, b_short, tm=512):
    B, L, D = x.shape
    rb = tm // HALO
    nrb = L // HALO
    grid = (B, L // tm)
    return pl.pallas_call(
        _inproj_kernel,
        grid=grid,
        in_specs=[
            pl.BlockSpec((1, HALO, D), lambda b, t: (b, jnp.maximum(t * rb - 1, 0), 0)),
            pl.BlockSpec((1, tm, D), lambda b, t: (b, t, 0)),
            pl.BlockSpec((1, HALO, D), lambda b, t: (b, jnp.minimum((t + 1) * rb, nrb - 1), 0)),
            pl.BlockSpec((1, D), lambda b, t: (0, 0)),
            pl.BlockSpec((D, D_IN), lambda b, t: (0, 0)),
            pl.BlockSpec((3, N_HY), lambda b, t: (0, 0)),
            pl.BlockSpec((1, N_HY), lambda b, t: (0, 0)),
        ],
        out_specs=[
            pl.BlockSpec((1, tm, D_HYENA), lambda b, t: (b, t, 0)),
            pl.BlockSpec((1, tm, D_HYENA), lambda b, t: (b, t, 0)),
            pl.BlockSpec((1, tm, N_ATT), lambda b, t: (b, t, 0)),
        ],
        out_shape=[
            jax.ShapeDtypeStruct((B, L, D_HYENA), jnp.bfloat16),
            jax.ShapeDtypeStruct((B, L, D_HYENA), jnp.bfloat16),
            jax.ShapeDtypeStruct((B, L, N_ATT), jnp.bfloat16),
        ],
        compiler_params=pltpu.CompilerParams(
            dimension_semantics=("parallel", "arbitrary"), vmem_limit_bytes=VMEM_LIMIT),
        name="inproj",
    )(x, x, x, pre_g[None, :], w_cat, w_short, b_short[None, :])


def _alibi_slopes():
    return [float(v) for v in
            np.exp2(-8.0 * np.arange(1, N_HEADS + 1, dtype=np.float32) / N_HEADS).astype(np.float32)]


def _attn_bias_tables():
    c = np.arange(BLOCK)[:, None]
    r = np.arange(BLOCK)[None, :]
    dist = np.abs(c - r).astype(np.float32)
    slopes = np.asarray(_alibi_slopes(), np.float32)[:, None, None] * np.float32(LOG2E)
    return np.concatenate([-slopes * dist, -slopes * (WINDOW - dist)], axis=1).astype(np.float32)


def _attn_kernel(sink_ref, bias_ref, za_ref, ya_ref, kpl_ref, vt_ref, s_ref, ot_ref):
    L = za_ref.shape[1]
    nb = L // BLOCK
    o_g = D_ATTN
    o_k = 2 * D_ATTN
    o_v = o_k + KV_DIM
    f32 = jnp.float32
    bf16 = jnp.bfloat16
    slopes = _alibi_slopes()
    group = Q_PER_KV * HEAD_DIM

    lane = lax.broadcasted_iota(jnp.int32, (BLOCK, KV_DIM), 1)
    lo = lane < HEAD_DIM
    ztile = jnp.zeros((BLOCK, KV_DIM), bf16)
    for idx in range(2 * N_KV_HEADS):
        kpl_ref[idx, 0] = ztile
        kpl_ref[idx, nb + 1] = ztile
    vt_ref[0] = ztile
    vt_ref[nb + 1] = ztile

    def prep(j, carry):
        start = pl.multiple_of(j * BLOCK, BLOCK)
        kc = za_ref[0, pl.ds(start, BLOCK), o_k:o_k + KV_DIM].astype(f32)
        ksw = pltpu.roll(kc, HEAD_DIM, axis=1)
        kpl_ref[0, j + 1] = jnp.where(lo, kc, 0.0).astype(bf16)
        kpl_ref[1, j + 1] = jnp.where(lo, 0.0, ksw).astype(bf16)
        kpl_ref[2, j + 1] = jnp.where(lo, ksw, 0.0).astype(bf16)
        kpl_ref[3, j + 1] = jnp.where(lo, 0.0, kc).astype(bf16)
        vc = za_ref[0, pl.ds(start, BLOCK), o_v:o_v + KV_DIM].astype(f32)
        vt_ref[j + 1] = vc.T.astype(bf16)
        return carry

    lax.fori_loop(0, nb, prep, 0)

    c_i = lax.broadcasted_iota(jnp.int32, (BLOCK, BLOCK), 0)
    r_i = lax.broadcasted_iota(jnp.int32, (BLOCK, BLOCK), 1)
    upper = c_i >= r_i
    diag = c_i == r_i

    def scores(i, slot):
        start = pl.multiple_of(i * BLOCK, BLOCK)
        qblk = za_ref[0, pl.ds(start, BLOCK), 0:D_ATTN]
        for kv in range(N_KV_HEADS):
            base = kv * group
            qs = jnp.concatenate([qblk[:, base:base + 2 * HEAD_DIM],
                                  qblk[:, base + 2 * HEAD_DIM:base + group]], axis=0)
            kp = jnp.concatenate([kpl_ref[2 * kv + half, i + t]
                                  for half in range(2) for t in range(3)], axis=0)
            s_ref[slot, kv] = lax.dot_general(kp, qs, (((1,), (1,)), ((), ())),
                                              preferred_element_type=f32)

    def softmax_pv(i, slot, first, last):
        for kv in range(N_KV_HEADS):
            rows = slice(kv * HEAD_DIM, (kv + 1) * HEAD_DIM)
            vt3 = vt_ref[i + 2, rows, :]
            vtw = jnp.concatenate([vt_ref[i, rows, :], vt_ref[i + 1, rows, :], vt3], axis=1)
            pts, invs, pxs = [], [], []
            for g in range(Q_PER_KV):
                half, pair = g % 2, g // 2
                h = kv * Q_PER_KV + g
                r0 = half * 3 * BLOCK
                cols = slice(pair * BLOCK, (pair + 1) * BLOCK)
                s1 = s_ref[slot, kv, r0:r0 + BLOCK, cols]
                s2 = s_ref[slot, kv, r0 + BLOCK:r0 + 2 * BLOCK, cols]
                s3 = s_ref[slot, kv, r0 + 2 * BLOCK:r0 + 3 * BLOCK, cols]
                if first:
                    mrg = jnp.where(upper, NEG_INF, s3)
                elif last:
                    mrg = jnp.where(upper, s1, NEG_INF)
                else:
                    mrg = jnp.where(upper, s1, s3)
                a_mid = s2 + bias_ref[h, 0:BLOCK, :]
                a_mrg = mrg + bias_ref[h, BLOCK:, :]
                sink2 = sink_ref[h] * LOG2E
                m = jnp.max(jnp.maximum(a_mid, a_mrg), axis=0, keepdims=True)
                m = jnp.maximum(m, sink2)
                if not last:
                    s_x = (jnp.sum(jnp.where(diag, s3, 0.0), axis=0, keepdims=True)
                           - slopes[h] * LOG2E * WINDOW)
                    m = jnp.maximum(m, s_x)
                p_mid = jnp.exp2(a_mid - m)
                p_mrg = jnp.exp2(a_mrg - m)
                den = jnp.sum(p_mid + p_mrg, axis=0, keepdims=True) + jnp.exp2(sink2 - m)
                if not last:
                    p_x = jnp.exp2(s_x - m)
                    den = den + p_x
                    pxs.append(p_x)
                invs.append(1.0 / den)
                p1 = jnp.where(upper, p_mrg, 0.0)
                p3 = jnp.where(upper, 0.0, p_mrg)
                pts.append(jnp.concatenate([p1, p_mid, p3], axis=0).astype(bf16))
            pt = jnp.concatenate(pts, axis=1)
            ot = jnp.dot(vtw, pt, preferred_element_type=f32)
            for g in range(Q_PER_KV):
                h = kv * Q_PER_KV + g
                o = ot[:, g * BLOCK:(g + 1) * BLOCK]
                if not last:
                    o = o + vt3.astype(f32) * pxs[g]
                ot_ref[slot, h * HEAD_DIM:(h + 1) * HEAD_DIM, :] = o * invs[g]

    def finish(i, slot):
        start = pl.multiple_of(i * BLOCK, BLOCK)
        gblk = za_ref[0, pl.ds(start, BLOCK), o_g:o_g + D_ATTN].astype(f32)
        y = ot_ref[slot].T
        ya_ref[0, pl.ds(start, BLOCK), :] = (y * (gblk * jax.nn.sigmoid(gblk))).astype(ya_ref.dtype)

    def two_blocks(j, carry):
        i = 2 * j + 1
        scores(i + 1, 0)
        softmax_pv(i, 1, False, False)
        finish(i - 1, 0)
        scores(i + 2, 1)
        softmax_pv(i + 1, 0, False, False)
        finish(i, 1)
        return carry

    scores(0, 0)
    scores(1, 1)
    softmax_pv(0, 0, True, False)
    lax.fori_loop(0, (nb - 2) // 2, two_blocks, 0)
    softmax_pv(nb - 1, 1, False, True)
    finish(nb - 2, 0)
    finish(nb - 1, 1)


def _attention(za, sink):
    B, L, _ = za.shape
    nb = L // BLOCK
    assert WINDOW == BLOCK and nb >= 4 and nb % 2 == 0
    nkt = nb + 2
    bias = jnp.asarray(_attn_bias_tables())
    return pl.pallas_call(
        _attn_kernel,
        grid=(B,),
        in_specs=[
            pl.BlockSpec(memory_space=pltpu.SMEM),
            pl.BlockSpec((N_HEADS, 2 * BLOCK, BLOCK), lambda b: (0, 0, 0)),
            pl.BlockSpec((1, L, N_ATT), lambda b: (b, 0, 0)),
        ],
        out_specs=pl.BlockSpec((1, L, D_ATTN), lambda b: (b, 0, 0)),
        out_shape=jax.ShapeDtypeStruct((B, L, D_ATTN), jnp.bfloat16),
        scratch_shapes=[
            pltpu.VMEM((2 * N_KV_HEADS, nkt, BLOCK, KV_DIM), jnp.bfloat16),
            pltpu.VMEM((nkt, KV_DIM, BLOCK), jnp.bfloat16),
            pltpu.VMEM((2, N_KV_HEADS, 6 * BLOCK, 2 * BLOCK), jnp.float32),
            pltpu.VMEM((2, D_ATTN, BLOCK), jnp.float32),
        ],
        compiler_params=pltpu.CompilerParams(
            dimension_semantics=("parallel",), vmem_limit_bytes=VMEM_LIMIT),
        name="swa_attention",
    )(sink, bias, za)


def _hyena_kernel(k_ref, d_ref, v_ref, y_ref, s_ref, acc_ref):
    cg = v_ref.shape[0]
    rows = v_ref.shape[1]
    n2 = k_ref.shape[1]
    nblk = n2 // (2 * TBLK)
    bsz = rows // nblk
    off = TBLK * (nblk - 1) + SHIFT_ROWS

    for ci in range(cg):
        kb = jnp.broadcast_to(k_ref[ci:ci + 1, :], (SHIFT_ROWS, n2))
        s_ref[...] = pltpu.roll(kb, off, axis=1, stride=1, stride_axis=0).astype(s_ref.dtype)
        acc_ref[...] = d_ref[ci:ci + 1, :] * v_ref[ci].astype(jnp.float32)
        for d in range(-(nblk - 1), nblk):
            x0 = TBLK * (d + nblk - 1)
            w = jnp.concatenate(
                [s_ref[:, x0 + SHIFT_ROWS:x0 + SHIFT_ROWS + TBLK], s_ref[:, x0:x0 + TBLK]], axis=0)
            n = (nblk - abs(d)) * bsz
            src = 0 if d >= 0 else -d * bsz
            dst = d * bsz if d >= 0 else 0
            acc_ref[dst:dst + n, :] += jnp.dot(v_ref[ci, src:src + n, :], w,
                                               preferred_element_type=jnp.float32)
        y_ref[ci] = acc_ref[...].astype(y_ref.dtype)


def _hyena_conv(kt, hyena_d, v_rows, cg=8):
    C, rows, _ = v_rows.shape
    n2 = kt.shape[1]
    return pl.pallas_call(
        _hyena_kernel,
        grid=(C // cg,),
        in_specs=[
            pl.BlockSpec((cg, n2), lambda c: (c, 0)),
            pl.BlockSpec((cg, 1), lambda c: (c, 0)),
            pl.BlockSpec((cg, rows, TBLK), lambda c: (c, 0, 0)),
        ],
        out_specs=pl.BlockSpec((cg, rows, TBLK), lambda c: (c, 0, 0)),
        out_shape=jax.ShapeDtypeStruct((C, rows, TBLK), jnp.bfloat16),
        scratch_shapes=[
            pltpu.VMEM((SHIFT_ROWS, n2), jnp.bfloat16),
            pltpu.VMEM((rows, TBLK), jnp.float32),
        ],
        compiler_params=pltpu.CompilerParams(
            dimension_semantics=("parallel",), vmem_limit_bytes=VMEM_LIMIT),
        name="hyena_conv",
    )(kt, hyena_d[:, None], v_rows)


def _outproj_kernel(x_ref, yh_ref, x0g_ref, ya_ref, w_ref, g_ref, o_ref):
    yh = (yh_ref[0].astype(jnp.float32) * x0g_ref[0].astype(jnp.float32)).astype(jnp.bfloat16)
    y = jnp.dot(jnp.concatenate([yh, ya_ref[0]], axis=1), w_ref[...],
                preferred_element_type=jnp.float32)
    ms = jnp.mean(y * y, axis=-1, keepdims=True)
    o_ref[0] = x_ref[0] + y * lax.rsqrt(ms + RMS_EPS) * g_ref[...]


def _outproj(x, yh, x0g, ya, w_out, post_g, tm=1024):
    B, L, D = x.shape
    return pl.pallas_call(
        _outproj_kernel,
        grid=(B, L // tm),
        in_specs=[
            pl.BlockSpec((1, tm, D), lambda b, t: (b, t, 0)),
            pl.BlockSpec((1, tm, D_HYENA), lambda b, t: (b, t, 0)),
            pl.BlockSpec((1, tm, D_HYENA), lambda b, t: (b, t, 0)),
            pl.BlockSpec((1, tm, D_ATTN), lambda b, t: (b, t, 0)),
            pl.BlockSpec((D_HYENA + D_ATTN, D), lambda b, t: (0, 0)),
            pl.BlockSpec((1, D), lambda b, t: (0, 0)),
        ],
        out_specs=pl.BlockSpec((1, tm, D), lambda b, t: (b, t, 0)),
        out_shape=jax.ShapeDtypeStruct((B, L, D), jnp.float32),
        compiler_params=pltpu.CompilerParams(
            dimension_semantics=("parallel", "parallel"), vmem_limit_bytes=VMEM_LIMIT),
        name="outproj",
    )(x, yh, x0g, ya, w_out, post_g[None, :])


def _layer(x, pre_g, w_in, w_short, b_short, w_f1, b_f1, w_f2, b_f2, w_f3, b_f3, w_f4,
           sin_freq, hyena_d, attn_sink, w_out, post_g):
    B, L, _ = x.shape
    nblk = L // TBLK
    o_q = 4 * D_HYENA
    o_k = o_q + D_ATTN
    o_v = o_k + KV_DIM
    o_ag = o_v + KV_DIM
    w_cat = jnp.concatenate(
        [w_in[:, :o_q], w_in[:, o_q:o_k], w_in[:, o_ag:], w_in[:, o_k:o_v], w_in[:, o_v:o_ag]],
        axis=1).astype(jnp.bfloat16)

    kt = _hyena_filter(L, w_f1, b_f1, w_f2, b_f2, w_f3, b_f3, w_f4, sin_freq)
    v, x0g, za = _inproj(x, pre_g, w_cat, w_short, b_short)
    ya = _attention(za, attn_sink)
    v_rows = v.reshape(B, nblk, TBLK, D_HYENA).transpose(3, 1, 0, 2).reshape(D_HYENA, nblk * B, TBLK)
    y_rows = _hyena_conv(kt, hyena_d, v_rows)
    yh = y_rows.reshape(D_HYENA, nblk, B, TBLK).transpose(2, 1, 3, 0).reshape(B, L, D_HYENA)
    return _outproj(x, yh, x0g, ya, w_out.astype(jnp.bfloat16), post_g)


def kernel(x, pre_g, w_in, w_short, b_short, w_f1, b_f1, w_f2, b_f2, w_f3, b_f3, w_f4, sin_freq, hyena_d, attn_sink, w_out, post_g):
    depth = pre_g.shape[0]
    for l in range(depth):
        x = _layer(x, pre_g[l], w_in[l], w_short[l], b_short[l], w_f1[l], b_f1[l], w_f2[l],
                   b_f2[l], w_f3[l], b_f3[l], w_f4[l], sin_freq[l], hyena_d[l], attn_sink[l],
                   w_out[l], post_g[l])
    return x
```

```python
import functools
import math

import jax
import jax.numpy as jnp
import numpy as np
from jax import lax
from jax.experimental import pallas as pl
from jax.experimental.pallas import tpu as pltpu

D_MODEL = 1024
D_HYENA = 512
D_ATTN = 512
N_HEADS = 8
HEAD_DIM = 64
N_KV_HEADS = 2
Q_PER_KV = N_HEADS // N_KV_HEADS
KV_DIM = N_KV_HEADS * HEAD_DIM
WINDOW = 128
BLOCK = 128
FILTER_ORDER = 64
N_BANDS = 16
POS_EMB_DIM = 1 + 2 * N_BANDS
POS_EMB_PAD = 40
DECAY_TARGET = 1e-2
FAST_DECAY_PCT = 0.3
SLOW_DECAY_PCT = 1.5
RMS_EPS = 1e-6
NEG_INF = -1e30
LOG2E = math.log2(math.e)
Q_SCALE_LOG2 = HEAD_DIM ** -0.5 * LOG2E

N_HY = 3 * D_HYENA
O_GH = N_HY
O_ATT = O_GH + D_HYENA
N_ATT = 2 * D_ATTN + 2 * KV_DIM
D_IN = O_ATT + N_ATT

HALO = 16

TBLK = 256
SHIFT_ROWS = 128
HY_UNROLL = 4

VMEM_LIMIT = 56 * 1024 * 1024

_HI = lax.Precision.HIGHEST


def _filter_consts(L):
    t = np.arange(L, dtype=np.float32)
    t_norm = t / np.float32(max(L - 1, 1))
    w = np.float32(2.0 * math.pi) * t / np.float32(L)
    bands = np.linspace(1e-4, N_BANDS - 1, N_BANDS).astype(np.float32)
    ang = w[:, None] * bands[None, :]
    z = np.concatenate([t_norm[:, None], np.cos(ang), -np.sin(ang)], axis=-1)
    min_decay = math.log(DECAY_TARGET) / SLOW_DECAY_PCT
    max_decay = math.log(DECAY_TARGET) / FAST_DECAY_PCT
    deltas = np.abs(np.linspace(min_decay, max_decay, D_HYENA)).astype(np.float32)
    decay = np.exp(-t_norm[:, None] * deltas[None, :]).astype(np.float32)
    idx = (L - np.arange(L)) % L
    z_rev = z[idx]
    decay_rev = decay[idx]
    pad = np.zeros((POS_EMB_PAD - POS_EMB_DIM, L), np.float32)
    zt = np.concatenate([z.T, pad], axis=0)
    zt_rev = np.concatenate([z_rev.T, pad], axis=0)
    return zt, zt_rev, np.ascontiguousarray(decay.T), np.ascontiguousarray(decay_rev.T)


def _filter_kernel(zt_ref, ztr_ref, dec_ref, decr_ref, w1_ref, b1_ref, w2_ref, b2_ref,
                   w3_ref, b3_ref, w4_ref, fr_ref, k_ref):
    L = zt_ref.shape[1]

    def mlp(z):
        h = jnp.dot(w1_ref[...], z, precision=_HI, preferred_element_type=jnp.float32)
        h = jnp.sin(fr_ref[:, 0:1] * (h + b1_ref[...]))
        h = jnp.dot(w2_ref[...], h, precision=_HI, preferred_element_type=jnp.float32)
        h = jnp.sin(fr_ref[:, 1:2] * (h + b2_ref[...]))
        h = jnp.dot(w3_ref[...], h, precision=_HI, preferred_element_type=jnp.float32)
        return jnp.sin(fr_ref[:, 2:3] * (h + b3_ref[...]))

    hf = jnp.dot(w4_ref[0:D_HYENA, :], mlp(zt_ref[...]), precision=_HI,
                 preferred_element_type=jnp.float32) * dec_ref[...]
    hb = jnp.dot(w4_ref[D_HYENA:, :], mlp(ztr_ref[...]), precision=_HI,
                 preferred_element_type=jnp.float32) * decr_ref[...]
    col = lax.broadcasted_iota(jnp.int32, hb.shape, 1)
    hb = jnp.where(col == 0, 0.0, hb)
    ss = jnp.sum(hf * hf, axis=1, keepdims=True) + jnp.sum(hb * hb, axis=1, keepdims=True)
    inv = lax.rsqrt(ss + 1e-12)
    k_ref[:, 0:L] = hf * inv
    k_ref[:, L:2 * L] = hb * inv


def _hyena_filter(L, w_f1, b_f1, w_f2, b_f2, w_f3, b_f3, w_f4, sin_freq):
    zt, ztr, dec, decr = _filter_consts(L)
    w1t = jnp.pad(w_f1.T, ((0, 0), (0, POS_EMB_PAD - POS_EMB_DIM)))
    args = (jnp.asarray(zt), jnp.asarray(ztr), jnp.asarray(dec), jnp.asarray(decr),
            w1t, b_f1[:, None], w_f2.T, b_f2[:, None], w_f3.T, b_f3[:, None],
            w_f4.T, sin_freq.T)
    return pl.pallas_call(
        _filter_kernel,
        out_shape=jax.ShapeDtypeStruct((D_HYENA, 2 * L), jnp.float32),
        compiler_params=pltpu.CompilerParams(vmem_limit_bytes=VMEM_LIMIT),
        name="hyena_filter",
    )(*args)


def _inproj_kernel(xp_ref, x_ref, xn_ref, g_ref, w_ref, wsh_ref, bsh_ref,
                   v_ref, x0g_ref, za_ref):
    t = pl.program_id(1)
    nt = pl.num_programs(1)
    tm = x_ref.shape[1]

    def norm(xv):
        ms = jnp.mean(xv * xv, axis=-1, keepdims=True)
        return (xv * lax.rsqrt(ms + RMS_EPS) * g_ref[...]).astype(jnp.bfloat16)

    h = norm(x_ref[0])
    hp = jnp.where(t > 0, norm(xp_ref[0]), 0.0).astype(jnp.bfloat16)
    hn = jnp.where(t < nt - 1, norm(xn_ref[0]), 0.0).astype(jnp.bfloat16)
    hall = jnp.concatenate([hp, h, hn], axis=0)
    zh = jnp.dot(hall, w_ref[:, 0:N_HY], preferred_element_type=jnp.float32)
    z = jnp.dot(h, w_ref[:, N_HY:], preferred_element_type=jnp.float32)

    rows = tm + 2 * HALO
    u_prev = pltpu.roll(zh, 1, axis=0)[HALO:HALO + tm]
    u = zh[HALO:HALO + tm]
    u_next = pltpu.roll(zh, rows - 1, axis=0)[HALO:HALO + tm]
    uc = (u_prev * wsh_ref[0:1, :] + u * wsh_ref[1:2, :] + u_next * wsh_ref[2:3, :]
          + bsh_ref[...])
    x0 = uc[:, 0:D_HYENA]
    x1 = uc[:, D_HYENA:2 * D_HYENA]
    vv = uc[:, 2 * D_HYENA:]
    gh = z[:, 0:D_HYENA]
    v_ref[0] = (vv * x1).astype(v_ref.dtype)
    x0g_ref[0] = (x0 * (gh * jax.nn.sigmoid(gh))).astype(x0g_ref.dtype)
    o_q = O_ATT - N_HY
    za_ref[0, :, 0:D_ATTN] = (z[:, o_q:o_q + D_ATTN] * Q_SCALE_LOG2).astype(za_ref.dtype)
    ga = z[:, o_q + D_ATTN:o_q + 2 * D_ATTN]
    za_ref[0, :, D_ATTN:2 * D_ATTN] = (ga * jax.nn.sigmoid(ga)).astype(za_ref.dtype)
    za_ref[0, :, 2 * D_ATTN:] = z[:, o_q + 2 * D_ATTN:].astype(za_ref.dtype)


def _inproj(x, pre_g, w_cat, w_short, b_short, tm=512):
    B, L, D = x.shape
    rb = tm // HALO
    nrb = L // HALO
    grid = (B, L // tm)
    return pl.pallas_call(
        _inproj_kernel,
        grid=grid,
        in_specs=[
            pl.BlockSpec((1, HALO, D), lambda b, t: (b, jnp.maximum(t * rb - 1, 0), 0)),
            pl.BlockSpec((1, tm, D), lambda b, t: (b, t, 0)),
            pl.BlockSpec((1, HALO, D), lambda b, t: (b, jnp.minimum((t + 1) * rb, nrb - 1), 0)),
            pl.BlockSpec((1, D), lambda b, t: (0, 0)),
            pl.BlockSpec((D, D_IN), lambda b, t: (0, 0)),
            pl.BlockSpec((3, N_HY), lambda b, t: (0, 0)),
            pl.BlockSpec((1, N_HY), lambda b, t: (0, 0)),
        ],
        out_specs=[
            pl.BlockSpec((1, tm, D_HYENA), lambda b, t: (b, t, 0)),
            pl.BlockSpec((1, tm, D_HYENA), lambda b, t: (b, t, 0)),
            pl.BlockSpec((1, tm, N_ATT), lambda b, t: (b, t, 0)),
        ],
        out_shape=[
            jax.ShapeDtypeStruct((B, L, D_HYENA), jnp.bfloat16),
            jax.ShapeDtypeStruct((B, L, D_HYENA), jnp.bfloat16),
            jax.ShapeDtypeStruct((B, L, N_ATT), jnp.bfloat16),
        ],
        compiler_params=pltpu.CompilerParams(
            dimension_semantics=("parallel", "arbitrary"), vmem_limit_bytes=VMEM_LIMIT),
        name="inproj",
    )(x, x, x, pre_g[None, :], w_cat, w_short, b_short[None, :])


def _alibi_slopes():
    return [float(v) for v in
            np.exp2(-8.0 * np.arange(1, N_HEADS + 1, dtype=np.float32) / N_HEADS).astype(np.float32)]


def _attn_bias_tables():
    c = np.arange(BLOCK)[:, None]
    r = np.arange(BLOCK)[None, :]
    dist = np.abs(c - r).astype(np.float32)
    slopes = np.asarray(_alibi_slopes(), np.float32)[:, None, None] * np.float32(LOG2E)
    return np.concatenate([-slopes * dist, -slopes * (WINDOW - dist)], axis=1).astype(np.float32)


def _attn_kernel(sink_ref, bias_ref, za_ref, ya_ref, kpl_ref, vt_ref, s_ref, ot_ref, mask_ref):
    L = za_ref.shape[1]
    nb = L // BLOCK
    o_g = D_ATTN
    o_k = 2 * D_ATTN
    o_v = o_k + KV_DIM
    f32 = jnp.float32
    bf16 = jnp.bfloat16
    slopes = _alibi_slopes()
    group = Q_PER_KV * HEAD_DIM

    lane = lax.broadcasted_iota(jnp.int32, (BLOCK, KV_DIM), 1)
    lo = lane < HEAD_DIM
    ztile = jnp.zeros((BLOCK, KV_DIM), bf16)
    for idx in range(2 * N_KV_HEADS):
        kpl_ref[idx, 0] = ztile
        kpl_ref[idx, nb + 1] = ztile
    vt_ref[0] = ztile
    vt_ref[nb + 1] = ztile

    def prep(j, carry):
        start = pl.multiple_of(j * BLOCK, BLOCK)
        kc = za_ref[0, pl.ds(start, BLOCK), o_k:o_k + KV_DIM].astype(f32)
        ksw = pltpu.roll(kc, HEAD_DIM, axis=1)
        kpl_ref[0, j + 1] = jnp.where(lo, kc, 0.0).astype(bf16)
        kpl_ref[1, j + 1] = jnp.where(lo, 0.0, ksw).astype(bf16)
        kpl_ref[2, j + 1] = jnp.where(lo, ksw, 0.0).astype(bf16)
        kpl_ref[3, j + 1] = jnp.where(lo, 0.0, kc).astype(bf16)
        vc = za_ref[0, pl.ds(start, BLOCK), o_v:o_v + KV_DIM].astype(f32)
        vt_ref[j + 1] = vc.T.astype(bf16)
        return carry

    lax.fori_loop(0, nb, prep, 0, unroll=4)

    c_i = lax.broadcasted_iota(jnp.int32, (BLOCK, BLOCK), 0)
    r_i = lax.broadcasted_iota(jnp.int32, (BLOCK, BLOCK), 1)
    upper = c_i >= r_i
    diag = c_i == r_i
    mask_ref[...] = jnp.where(upper, 1.0, 0.0).astype(bf16)
    ones_rows = jnp.ones((16, 3 * BLOCK), bf16)

    def scores(i, slot):
        start = pl.multiple_of(i * BLOCK, BLOCK)
        qblk = za_ref[0, pl.ds(start, BLOCK), 0:D_ATTN]
        for kv in range(N_KV_HEADS):
            base = kv * group
            qs = jnp.concatenate([qblk[:, base:base + 2 * HEAD_DIM],
                                  qblk[:, base + 2 * HEAD_DIM:base + group]], axis=0)
            kp = jnp.concatenate([kpl_ref[2 * kv + half, i + t]
                                  for half in range(2) for t in range(3)], axis=0)
            s_ref[slot, kv] = lax.dot_general(kp, qs, (((1,), (1,)), ((), ())),
                                              preferred_element_type=f32)

    def softmax_pv(i, slot, first, last):
        for kv in range(N_KV_HEADS):
            rows = slice(kv * HEAD_DIM, (kv + 1) * HEAD_DIM)
            vt3 = vt_ref[i + 2, rows, :]
            vtw = jnp.concatenate([vt_ref[i, rows, :], vt_ref[i + 1, rows, :], vt3], axis=1)
            pts, extras, pxs = [], [], []
            for g in range(Q_PER_KV):
                half, pair = g % 2, g // 2
                h = kv * Q_PER_KV + g
                r0 = half * 3 * BLOCK
                cols = slice(pair * BLOCK, (pair + 1) * BLOCK)
                s1 = s_ref[slot, kv, r0:r0 + BLOCK, cols]
                s2 = s_ref[slot, kv, r0 + BLOCK:r0 + 2 * BLOCK, cols]
                s3 = s_ref[slot, kv, r0 + 2 * BLOCK:r0 + 3 * BLOCK, cols]
                if first:
                    mrg = jnp.where(upper, NEG_INF, s3)
                elif last:
                    mrg = jnp.where(upper, s1, NEG_INF)
                else:
                    mrg = jnp.where(upper, s1, s3)
                a_mid = s2 + bias_ref[h, 0:BLOCK, :]
                a_mrg = mrg + bias_ref[h, BLOCK:, :]
                sink2 = sink_ref[h] * LOG2E
                m = jnp.max(jnp.maximum(a_mid, a_mrg), axis=0, keepdims=True)
                m = jnp.maximum(m, sink2)
                if not last:
                    s_x = (jnp.sum(jnp.where(diag, s3, 0.0), axis=0, keepdims=True)
                           - slopes[h] * LOG2E * WINDOW)
                    m = jnp.maximum(m, s_x)
                p_mid = jnp.exp2(a_mid - m).astype(bf16)
                p_mrg = jnp.exp2(a_mrg - m).astype(bf16)
                extra = jnp.exp2(sink2 - m)
                if not last:
                    p_x = jnp.exp2(s_x - m)
                    extra = extra + p_x
                    pxs.append(p_x)
                extras.append(extra)
                p1 = p_mrg * mask_ref[...]
                p3 = p_mrg - p1
                pts.append(jnp.concatenate([p1, p_mid, p3], axis=0))
            pt = jnp.concatenate(pts, axis=1)
            ot = jnp.dot(jnp.concatenate([vtw, ones_rows], axis=0), pt,
                         preferred_element_type=f32)
            for g in range(Q_PER_KV):
                h = kv * Q_PER_KV + g
                cols = slice(g * BLOCK, (g + 1) * BLOCK)
                o = ot[0:HEAD_DIM, cols]
                if not last:
                    o = o + vt3.astype(f32) * pxs[g]
                inv = 1.0 / (ot[HEAD_DIM:HEAD_DIM + 1, cols] + extras[g])
                ot_ref[slot, h * HEAD_DIM:(h + 1) * HEAD_DIM, :] = o * inv

    def finish(i, slot):
        start = pl.multiple_of(i * BLOCK, BLOCK)
        gate = za_ref[0, pl.ds(start, BLOCK), o_g:o_g + D_ATTN].astype(f32)
        ya_ref[0, pl.ds(start, BLOCK), :] = (ot_ref[slot].T * gate).astype(ya_ref.dtype)

    def two_blocks(j, carry):
        i = 2 * j + 1
        scores(i + 1, 0)
        softmax_pv(i, 1, False, False)
        finish(i - 1, 0)
        scores(i + 2, 1)
        softmax_pv(i + 1, 0, False, False)
        finish(i, 1)
        return carry

    scores(0, 0)
    scores(1, 1)
    softmax_pv(0, 0, True, False)
    lax.fori_loop(0, (nb - 2) // 2, two_blocks, 0)
    softmax_pv(nb - 1, 1, False, True)
    finish(nb - 2, 0)
    finish(nb - 1, 1)


def _attention(za, sink):
    B, L, _ = za.shape
    nb = L // BLOCK
    assert WINDOW == BLOCK and nb >= 4 and nb % 2 == 0
    nkt = nb + 2
    bias = jnp.asarray(_attn_bias_tables())
    return pl.pallas_call(
        _attn_kernel,
        grid=(B,),
        in_specs=[
            pl.BlockSpec(memory_space=pltpu.SMEM),
            pl.BlockSpec((N_HEADS, 2 * BLOCK, BLOCK), lambda b: (0, 0, 0)),
            pl.BlockSpec((1, L, N_ATT), lambda b: (b, 0, 0)),
        ],
        out_specs=pl.BlockSpec((1, L, D_ATTN), lambda b: (b, 0, 0)),
        out_shape=jax.ShapeDtypeStruct((B, L, D_ATTN), jnp.bfloat16),
        scratch_shapes=[
            pltpu.VMEM((2 * N_KV_HEADS, nkt, BLOCK, KV_DIM), jnp.bfloat16),
            pltpu.VMEM((nkt, KV_DIM, BLOCK), jnp.bfloat16),
            pltpu.VMEM((2, N_KV_HEADS, 6 * BLOCK, 2 * BLOCK), jnp.float32),
            pltpu.VMEM((2, D_ATTN, BLOCK), jnp.float32),
            pltpu.VMEM((BLOCK, BLOCK), jnp.bfloat16),
        ],
        compiler_params=pltpu.CompilerParams(
            dimension_semantics=("parallel",), vmem_limit_bytes=VMEM_LIMIT),
        name="swa_attention",
    )(sink, bias, za)


def _hyena_kernel(k_ref, d_ref, v_ref, y_ref, *scratch):
    tables = scratch[:HY_UNROLL]
    accs = scratch[HY_UNROLL:]
    step = pl.program_id(0)
    cg = v_ref.shape[0]
    rows = v_ref.shape[1]
    nchan, n2 = k_ref.shape
    nblk = n2 // (2 * TBLK)
    bsz = rows // nblk
    off = TBLK * (nblk - 1) + SHIFT_ROWS

    def build_table(chan, s_ref):
        krow = k_ref[pl.ds(jnp.minimum(chan, nchan - 1), 1), :]
        kb = jnp.broadcast_to(krow, (SHIFT_ROWS, n2))
        s_ref[...] = pltpu.roll(kb, off, axis=1, stride=1, stride_axis=0).astype(s_ref.dtype)

    def convolve(ci, s_ref, acc_ref):
        acc_ref[...] = d_ref[pl.ds(ci, 1), :] * v_ref[ci].astype(jnp.float32)
        for d in range(-(nblk - 1), nblk):
            x0 = TBLK * (d + nblk - 1)
            w = jnp.concatenate(
                [s_ref[:, x0 + SHIFT_ROWS:x0 + SHIFT_ROWS + TBLK], s_ref[:, x0:x0 + TBLK]], axis=0)
            n = (nblk - abs(d)) * bsz
            src = 0 if d >= 0 else -d * bsz
            dst = d * bsz if d >= 0 else 0
            acc_ref[dst:dst + n, :] += jnp.dot(v_ref[ci, pl.ds(src, n), :], w,
                                               preferred_element_type=jnp.float32)
        y_ref[ci] = acc_ref[...].astype(y_ref.dtype)

    @pl.when(step == 0)
    def _():
        for u in range(HY_UNROLL):
            build_table(u, tables[u])

    def body(it, carry):
        for u in range(HY_UNROLL):
            convolve(it * HY_UNROLL + u, tables[u], accs[u])
        for u in range(HY_UNROLL):
            build_table(step * cg + (it + 1) * HY_UNROLL + u, tables[u])
        return carry

    lax.fori_loop(0, cg // HY_UNROLL, body, 0)


def _hyena_conv(kt, hyena_d, v_rows, cg=8):
    C, rows, _ = v_rows.shape
    n2 = kt.shape[1]
    return pl.pallas_call(
        _hyena_kernel,
        grid=(C // cg,),
        in_specs=[
            pl.BlockSpec((C, n2), lambda c: (0, 0)),
            pl.BlockSpec((cg, 1), lambda c: (c, 0)),
            pl.BlockSpec((cg, rows, TBLK), lambda c: (c, 0, 0)),
        ],
        out_specs=pl.BlockSpec((cg, rows, TBLK), lambda c: (c, 0, 0)),
        out_shape=jax.ShapeDtypeStruct((C, rows, TBLK), jnp.bfloat16),
        scratch_shapes=(
            [pltpu.VMEM((SHIFT_ROWS, n2), jnp.bfloat16)] * HY_UNROLL
            + [pltpu.VMEM((rows, TBLK), jnp.float32)] * HY_UNROLL),
        compiler_params=pltpu.CompilerParams(
            dimension_semantics=("arbitrary",), vmem_limit_bytes=VMEM_LIMIT),
        name="hyena_conv",
    )(kt, hyena_d[:, None], v_rows)


def _outproj_kernel(x_ref, yh_ref, x0g_ref, ya_ref, w_ref, g_ref, o_ref):
    yh = (yh_ref[0].astype(jnp.float32) * x0g_ref[0].astype(jnp.float32)).astype(jnp.bfloat16)
    y = jnp.dot(jnp.concatenate([yh, ya_ref[0]], axis=1), w_ref[...],
                preferred_element_type=jnp.float32)
    ms = jnp.mean(y * y, axis=-1, keepdims=True)
    o_ref[0] = x_ref[0] + y * lax.rsqrt(ms + RMS_EPS) * g_ref[...]


def _outproj(x, yh, x0g, ya, w_out, post_g, tm=1024):
    B, L, D = x.shape
    return pl.pallas_call(
        _outproj_kernel,
        grid=(B, L // tm),
        in_specs=[
            pl.BlockSpec((1, tm, D), lambda b, t: (b, t, 0)),
            pl.BlockSpec((1, tm, D_HYENA), lambda b, t: (b, t, 0)),
            pl.BlockSpec((1, tm, D_HYENA), lambda b, t: (b, t, 0)),
            pl.BlockSpec((1, tm, D_ATTN), lambda b, t: (b, t, 0)),
            pl.BlockSpec((D_HYENA + D_ATTN, D), lambda b, t: (0, 0)),
            pl.BlockSpec((1, D), lambda b, t: (0, 0)),
        ],
        out_specs=pl.BlockSpec((1, tm, D), lambda b, t: (b, t, 0)),
        out_shape=jax.ShapeDtypeStruct((B, L, D), jnp.float32),
        compiler_params=pltpu.CompilerParams(
            dimension_semantics=("parallel", "parallel"), vmem_limit_bytes=VMEM_LIMIT),
        name="outproj",
    )(x, yh, x0g, ya, w_out, post_g[None, :])


def _layer(x, pre_g, w_in, w_short, b_short, w_f1, b_f1, w_f2, b_f2, w_f3, b_f3, w_f4,
           sin_freq, hyena_d, attn_sink, w_out, post_g):
    B, L, _ = x.shape
    nblk = L // TBLK
    o_q = 4 * D_HYENA
    o_k = o_q + D_ATTN
    o_v = o_k + KV_DIM
    o_ag = o_v + KV_DIM
    w_cat = jnp.concatenate(
        [w_in[:, :o_q], w_in[:, o_q:o_k], w_in[:, o_ag:], w_in[:, o_k:o_v], w_in[:, o_v:o_ag]],
        axis=1).astype(jnp.bfloat16)

    kt = _hyena_filter(L, w_f1, b_f1, w_f2, b_f2, w_f3, b_f3, w_f4, sin_freq)
    v, x0g, za = _inproj(x, pre_g, w_cat, w_short, b_short)
    ya = _attention(za, attn_sink)
    v_rows = v.reshape(B, nblk, TBLK, D_HYENA).transpose(3, 1, 0, 2).reshape(D_HYENA, nblk * B, TBLK)
    y_rows = _hyena_conv(kt, hyena_d, v_rows)
    yh = y_rows.reshape(D_HYENA, nblk, B, TBLK).transpose(2, 1, 3, 0).reshape(B, L, D_HYENA)
    return _outproj(x, yh, x0g, ya, w_out.astype(jnp.bfloat16), post_g)


def kernel(x, pre_g, w_in, w_short, b_short, w_f1, b_f1, w_f2, b_f2, w_f3, b_f3, w_f4, sin_freq, hyena_d, attn_sink, w_out, post_g):
    depth = pre_g.shape[0]
    for l in range(depth):
        x = _layer(x, pre_g[l], w_in[l], w_short[l], b_short[l], w_f1[l], b_f1[l], w_f2[l],
                   b_f2[l], w_f3[l], b_f3[l], w_f4[l], sin_freq[l], hyena_d[l], attn_sink[l],
                   w_out[l], post_g[l])
    return x
```

```python
import functools
import math

import jax
import jax.numpy as jnp
import numpy as np
from jax import lax
from jax.experimental import pallas as pl
from jax.experimental.pallas import tpu as pltpu

D_MODEL = 1024
D_HYENA = 512
D_ATTN = 512
N_HEADS = 8
HEAD_DIM = 64
N_KV_HEADS = 2
Q_PER_KV = N_HEADS // N_KV_HEADS
KV_DIM = N_KV_HEADS * HEAD_DIM
WINDOW = 128
BLOCK = 128
FILTER_ORDER = 64
N_BANDS = 16
POS_EMB_DIM = 1 + 2 * N_BANDS
POS_EMB_PAD = 40
DECAY_TARGET = 1e-2
FAST_DECAY_PCT = 0.3
SLOW_DECAY_PCT = 1.5
RMS_EPS = 1e-6
NEG_INF = -1e30
PV_HEADS = 2
LOG2E = math.log2(math.e)
Q_SCALE_LOG2 = HEAD_DIM ** -0.5 * LOG2E

N_HY = 3 * D_HYENA
O_GH = N_HY
O_ATT = O_GH + D_HYENA
N_ATT = 2 * D_ATTN + 2 * KV_DIM
D_IN = O_ATT + N_ATT

LANES = 128
HALO = 16

TBLK = 256
SHIFT_ROWS = 128
HY_UNROLL = 4

VMEM_LIMIT = 56 * 1024 * 1024

_HI = lax.Precision.HIGHEST


def _filter_consts(L):
    t = np.arange(L, dtype=np.float32)
    t_norm = t / np.float32(max(L - 1, 1))
    w = np.float32(2.0 * math.pi) * t / np.float32(L)
    bands = np.linspace(1e-4, N_BANDS - 1, N_BANDS).astype(np.float32)
    ang = w[:, None] * bands[None, :]
    z = np.concatenate([t_norm[:, None], np.cos(ang), -np.sin(ang)], axis=-1)
    min_decay = math.log(DECAY_TARGET) / SLOW_DECAY_PCT
    max_decay = math.log(DECAY_TARGET) / FAST_DECAY_PCT
    deltas = np.abs(np.linspace(min_decay, max_decay, D_HYENA)).astype(np.float32)
    decay = np.exp(-t_norm[:, None] * deltas[None, :]).astype(np.float32)
    pad = np.zeros((POS_EMB_PAD - POS_EMB_DIM, L), np.float32)
    zt = np.concatenate([z.T, pad], axis=0)
    flip = np.eye(LANES, dtype=np.float32)[::-1]
    return zt, np.ascontiguousarray(decay.T), np.ascontiguousarray(flip)


def _filter_kernel(zt_ref, dec_ref, flip_ref, w1_ref, b1_ref, w2_ref, b2_ref,
                   w3_ref, b3_ref, w4_ref, fr_ref, k_ref):
    L = zt_ref.shape[1]
    f32 = jnp.float32
    h = jnp.dot(w1_ref[...], zt_ref[...], precision=_HI, preferred_element_type=f32)
    h = jnp.sin(fr_ref[:, 0:1] * (h + b1_ref[...]))
    h = jnp.dot(w2_ref[...], h, precision=_HI, preferred_element_type=f32)
    h = jnp.sin(fr_ref[:, 1:2] * (h + b2_ref[...]))
    h = jnp.dot(w3_ref[...], h, precision=_HI, preferred_element_type=f32)
    h = jnp.sin(fr_ref[:, 2:3] * (h + b3_ref[...]))
    hf = jnp.dot(w4_ref[0:D_HYENA, :], h, precision=_HI, preferred_element_type=f32) * dec_ref[...]
    hb = jnp.dot(w4_ref[D_HYENA:, :], h, precision=_HI, preferred_element_type=f32) * dec_ref[...]
    nt = L // LANES
    rev = jnp.concatenate(
        [jnp.dot(hb[:, (nt - 1 - j) * LANES:(nt - j) * LANES], flip_ref[...], precision=_HI,
                 preferred_element_type=f32) for j in range(nt)], axis=1)
    col = lax.broadcasted_iota(jnp.int32, rev.shape, 1)
    hb2 = jnp.where(col == 0, 0.0, pltpu.roll(rev, 1, axis=1))
    ss = jnp.sum(hf * hf, axis=1, keepdims=True) + jnp.sum(hb2 * hb2, axis=1, keepdims=True)
    inv = lax.rsqrt(ss + 1e-12)
    k_ref[:, 0:L] = hf * inv
    k_ref[:, L:2 * L] = hb2 * inv


def _hyena_filter(L, w_f1, b_f1, w_f2, b_f2, w_f3, b_f3, w_f4, sin_freq):
    zt, dec, flip = _filter_consts(L)
    w1t = jnp.pad(w_f1.T, ((0, 0), (0, POS_EMB_PAD - POS_EMB_DIM)))
    args = (jnp.asarray(zt), jnp.asarray(dec), jnp.asarray(flip),
            w1t, b_f1[:, None], w_f2.T, b_f2[:, None], w_f3.T, b_f3[:, None],
            w_f4.T, sin_freq.T)
    return pl.pallas_call(
        _filter_kernel,
        out_shape=jax.ShapeDtypeStruct((D_HYENA, 2 * L), jnp.float32),
        compiler_params=pltpu.CompilerParams(vmem_limit_bytes=VMEM_LIMIT),
        name="hyena_filter",
    )(*args)


def _inproj_kernel(xp_ref, x_ref, xn_ref, g_ref, w_ref, wsh_ref, bsh_ref,
                   v_ref, x0g_ref, za_ref):
    t = pl.program_id(1)
    nt = pl.num_programs(1)
    tm = x_ref.shape[1]

    def norm(xv):
        ms = jnp.mean(xv * xv, axis=-1, keepdims=True)
        return (xv * lax.rsqrt(ms + RMS_EPS) * g_ref[...]).astype(jnp.bfloat16)

    h = norm(x_ref[0])
    hp = jnp.where(t > 0, norm(xp_ref[0]), 0.0).astype(jnp.bfloat16)
    hn = jnp.where(t < nt - 1, norm(xn_ref[0]), 0.0).astype(jnp.bfloat16)
    hall = jnp.concatenate([hp, h, hn], axis=0)
    zh = jnp.dot(hall, w_ref[:, 0:N_HY], preferred_element_type=jnp.float32)
    z = jnp.dot(h, w_ref[:, N_HY:], preferred_element_type=jnp.float32)

    rows = tm + 2 * HALO
    u_prev = pltpu.roll(zh, 1, axis=0)[HALO:HALO + tm]
    u = zh[HALO:HALO + tm]
    u_next = pltpu.roll(zh, rows - 1, axis=0)[HALO:HALO + tm]
    uc = (u_prev * wsh_ref[0:1, :] + u * wsh_ref[1:2, :] + u_next * wsh_ref[2:3, :]
          + bsh_ref[...])
    x0 = uc[:, 0:D_HYENA]
    x1 = uc[:, D_HYENA:2 * D_HYENA]
    vv = uc[:, 2 * D_HYENA:]
    gh = z[:, 0:D_HYENA]
    v_ref[0] = (vv * x1).astype(v_ref.dtype)
    x0g_ref[0] = (x0 * (gh * jax.nn.sigmoid(gh))).astype(x0g_ref.dtype)
    o_q = O_ATT - N_HY
    za_ref[0, :, 0:D_ATTN] = (z[:, o_q:o_q + D_ATTN] * Q_SCALE_LOG2).astype(za_ref.dtype)
    ga = z[:, o_q + D_ATTN:o_q + 2 * D_ATTN]
    za_ref[0, :, D_ATTN:2 * D_ATTN] = (ga * jax.nn.sigmoid(ga)).astype(za_ref.dtype)
    za_ref[0, :, 2 * D_ATTN:] = z[:, o_q + 2 * D_ATTN:].astype(za_ref.dtype)


def _inproj(x, pre_g, w_cat, w_short, b_short, tm=1024):
    B, L, D = x.shape
    rb = tm // HALO
    nrb = L // HALO
    grid = (B, L // tm)
    return pl.pallas_call(
        _inproj_kernel,
        grid=grid,
        in_specs=[
            pl.BlockSpec((1, HALO, D), lambda b, t: (b, jnp.maximum(t * rb - 1, 0), 0)),
            pl.BlockSpec((1, tm, D), lambda b, t: (b, t, 0)),
            pl.BlockSpec((1, HALO, D), lambda b, t: (b, jnp.minimum((t + 1) * rb, nrb - 1), 0)),
            pl.BlockSpec((1, D), lambda b, t: (0, 0)),
            pl.BlockSpec((D, D_IN), lambda b, t: (0, 0)),
            pl.BlockSpec((3, N_HY), lambda b, t: (0, 0)),
            pl.BlockSpec((1, N_HY), lambda b, t: (0, 0)),
        ],
        out_specs=[
            pl.BlockSpec((1, tm, D_HYENA), lambda b, t: (b, t, 0)),
            pl.BlockSpec((1, tm, D_HYENA), lambda b, t: (b, t, 0)),
            pl.BlockSpec((1, tm, N_ATT), lambda b, t: (b, t, 0)),
        ],
        out_shape=[
            jax.ShapeDtypeStruct((B, L, D_HYENA), jnp.bfloat16),
            jax.ShapeDtypeStruct((B, L, D_HYENA), jnp.bfloat16),
            jax.ShapeDtypeStruct((B, L, N_ATT), jnp.bfloat16),
        ],
        compiler_params=pltpu.CompilerParams(
            dimension_semantics=("parallel", "arbitrary"), vmem_limit_bytes=VMEM_LIMIT),
        name="inproj",
    )(x, x, x, pre_g[None, :], w_cat, w_short, b_short[None, :])


def _alibi_slopes():
    return [float(v) for v in
            np.exp2(-8.0 * np.arange(1, N_HEADS + 1, dtype=np.float32) / N_HEADS).astype(np.float32)]


def _attn_bias_tables():
    c = np.arange(BLOCK)[:, None]
    r = np.arange(BLOCK)[None, :]
    dist = np.abs(c - r).astype(np.float32)
    slopes = np.asarray(_alibi_slopes(), np.float32)[:, None, None] * np.float32(LOG2E)
    return np.concatenate([-slopes * dist, -slopes * (WINDOW - dist)], axis=1).astype(np.float32)


def _attn_kernel(sink_ref, bias_ref, za_ref, ya_ref, kpl_ref, vt_ref, s_ref, ot_ref, mask_ref):
    L = za_ref.shape[1]
    nb = L // BLOCK
    o_g = D_ATTN
    o_k = 2 * D_ATTN
    o_v = o_k + KV_DIM
    f32 = jnp.float32
    bf16 = jnp.bfloat16
    slopes = _alibi_slopes()
    group = Q_PER_KV * HEAD_DIM

    lane = lax.broadcasted_iota(jnp.int32, (BLOCK, KV_DIM), 1)
    lo = lane < HEAD_DIM
    ztile = jnp.zeros((BLOCK, KV_DIM), bf16)
    for idx in range(2 * N_KV_HEADS):
        kpl_ref[idx, 0] = ztile
        kpl_ref[idx, nb + 1] = ztile
    vt_ref[0] = ztile
    vt_ref[nb + 1] = ztile

    def prep(j, carry):
        start = pl.multiple_of(j * BLOCK, BLOCK)
        kc = za_ref[0, pl.ds(start, BLOCK), o_k:o_k + KV_DIM].astype(f32)
        ksw = pltpu.roll(kc, HEAD_DIM, axis=1)
        kpl_ref[0, j + 1] = jnp.where(lo, kc, 0.0).astype(bf16)
        kpl_ref[1, j + 1] = jnp.where(lo, 0.0, ksw).astype(bf16)
        kpl_ref[2, j + 1] = jnp.where(lo, ksw, 0.0).astype(bf16)
        kpl_ref[3, j + 1] = jnp.where(lo, 0.0, kc).astype(bf16)
        vc = za_ref[0, pl.ds(start, BLOCK), o_v:o_v + KV_DIM].astype(f32)
        vt_ref[j + 1] = vc.T.astype(bf16)
        return carry

    lax.fori_loop(0, nb, prep, 0, unroll=4)

    c_i = lax.broadcasted_iota(jnp.int32, (BLOCK, BLOCK), 0)
    r_i = lax.broadcasted_iota(jnp.int32, (BLOCK, BLOCK), 1)
    upper = c_i >= r_i
    diag = c_i == r_i
    mask_ref[...] = jnp.where(upper, 1.0, 0.0).astype(bf16)
    ones_rows = jnp.ones((16, 3 * BLOCK), bf16)

    def scores(i, slot):
        start = pl.multiple_of(i * BLOCK, BLOCK)
        qblk = za_ref[0, pl.ds(start, BLOCK), 0:D_ATTN]
        for kv in range(N_KV_HEADS):
            base = kv * group
            qs = jnp.concatenate([qblk[:, base:base + 2 * HEAD_DIM],
                                  qblk[:, base + 2 * HEAD_DIM:base + group]], axis=0)
            kp = jnp.concatenate([kpl_ref[2 * kv + half, i + t]
                                  for half in range(2) for t in range(3)], axis=0)
            s_ref[slot, kv] = lax.dot_general(kp, qs, (((1,), (1,)), ((), ())),
                                              preferred_element_type=f32)

    def softmax_pv(i, slot, first, last):
        for kv in range(N_KV_HEADS):
            rows = slice(kv * HEAD_DIM, (kv + 1) * HEAD_DIM)
            vt3 = vt_ref[i + 2, rows, :]
            vtw = jnp.concatenate([vt_ref[i, rows, :], vt_ref[i + 1, rows, :], vt3], axis=1)
            pts, extras, pxs = [], [], []
            for g in range(Q_PER_KV):
                half, pair = g % 2, g // 2
                h = kv * Q_PER_KV + g
                r0 = half * 3 * BLOCK
                cols = slice(pair * BLOCK, (pair + 1) * BLOCK)
                s1 = s_ref[slot, kv, r0:r0 + BLOCK, cols]
                s2 = s_ref[slot, kv, r0 + BLOCK:r0 + 2 * BLOCK, cols]
                s3 = s_ref[slot, kv, r0 + 2 * BLOCK:r0 + 3 * BLOCK, cols]
                if first:
                    mrg = jnp.where(upper, NEG_INF, s3)
                elif last:
                    mrg = jnp.where(upper, s1, NEG_INF)
                else:
                    mrg = jnp.where(upper, s1, s3)
                a_mid = s2 + bias_ref[h, 0:BLOCK, :]
                a_mrg = mrg + bias_ref[h, BLOCK:, :]
                sink2 = sink_ref[h] * LOG2E
                m = jnp.max(jnp.maximum(a_mid, a_mrg), axis=0, keepdims=True)
                m = jnp.maximum(m, sink2)
                if not last:
                    s_x = (jnp.sum(jnp.where(diag, s3, 0.0), axis=0, keepdims=True)
                           - slopes[h] * LOG2E * WINDOW)
                    m = jnp.maximum(m, s_x)
                p_mid = jnp.exp2(a_mid - m).astype(bf16)
                p_mrg = jnp.exp2(a_mrg - m).astype(bf16)
                extra = jnp.exp2(sink2 - m)
                if not last:
                    p_x = jnp.exp2(s_x - m)
                    extra = extra + p_x
                    pxs.append(p_x)
                extras.append(extra)
                p1 = p_mrg * mask_ref[...]
                p3 = p_mrg - p1
                pts.append(jnp.concatenate([p1, p_mid, p3], axis=0))
                if len(pts) < PV_HEADS:
                    continue
                pt = jnp.concatenate(pts, axis=1)
                ot = jnp.dot(jnp.concatenate([vtw, ones_rows], axis=0), pt,
                             preferred_element_type=f32)
                for n in range(PV_HEADS):
                    hh = h - (PV_HEADS - 1) + n
                    cols = slice(n * BLOCK, (n + 1) * BLOCK)
                    o = ot[0:HEAD_DIM, cols]
                    if not last:
                        o = o + vt3.astype(f32) * pxs[n]
                    inv = 1.0 / (ot[HEAD_DIM:HEAD_DIM + 1, cols] + extras[n])
                    ot_ref[slot, hh * HEAD_DIM:(hh + 1) * HEAD_DIM, :] = o * inv
                pts, extras, pxs = [], [], []

    def finish(i, slot):
        start = pl.multiple_of(i * BLOCK, BLOCK)
        gate = za_ref[0, pl.ds(start, BLOCK), o_g:o_g + D_ATTN].astype(f32)
        ya_ref[0, pl.ds(start, BLOCK), :] = (ot_ref[slot].T * gate).astype(ya_ref.dtype)

    def two_blocks(j, carry):
        i = 2 * j + 1
        scores(i + 1, 0)
        softmax_pv(i, 1, False, False)
        finish(i - 1, 0)
        scores(i + 2, 1)
        softmax_pv(i + 1, 0, False, False)
        finish(i, 1)
        return carry

    scores(0, 0)
    scores(1, 1)
    softmax_pv(0, 0, True, False)
    lax.fori_loop(0, (nb - 2) // 2, two_blocks, 0)
    softmax_pv(nb - 1, 1, False, True)
    finish(nb - 2, 0)
    finish(nb - 1, 1)


def _attention(za, sink):
    B, L, _ = za.shape
    nb = L // BLOCK
    assert WINDOW == BLOCK and nb >= 4 and nb % 2 == 0
    nkt = nb + 2
    bias = jnp.asarray(_attn_bias_tables())
    return pl.pallas_call(
        _attn_kernel,
        grid=(B,),
        in_specs=[
            pl.BlockSpec(memory_space=pltpu.SMEM),
            pl.BlockSpec((N_HEADS, 2 * BLOCK, BLOCK), lambda b: (0, 0, 0)),
            pl.BlockSpec((1, L, N_ATT), lambda b: (b, 0, 0)),
        ],
        out_specs=pl.BlockSpec((1, L, D_ATTN), lambda b: (b, 0, 0)),
        out_shape=jax.ShapeDtypeStruct((B, L, D_ATTN), jnp.bfloat16),
        scratch_shapes=[
            pltpu.VMEM((2 * N_KV_HEADS, nkt, BLOCK, KV_DIM), jnp.bfloat16),
            pltpu.VMEM((nkt, KV_DIM, BLOCK), jnp.bfloat16),
            pltpu.VMEM((2, N_KV_HEADS, 6 * BLOCK, 2 * BLOCK), jnp.float32),
            pltpu.VMEM((2, D_ATTN, BLOCK), jnp.float32),
            pltpu.VMEM((BLOCK, BLOCK), jnp.bfloat16),
        ],
        compiler_params=pltpu.CompilerParams(
            dimension_semantics=("parallel",), vmem_limit_bytes=VMEM_LIMIT),
        name="swa_attention",
    )(sink, bias, za)


def _hyena_kernel(k_ref, d_ref, v_ref, gate_ref, y_ref, *scratch):
    tables = scratch[:HY_UNROLL]
    accs = scratch[HY_UNROLL:]
    step = pl.program_id(0)
    cg = v_ref.shape[0]
    rows = v_ref.shape[1]
    nchan, n2 = k_ref.shape
    nblk = n2 // (2 * TBLK)
    bsz = rows // nblk
    off = TBLK * (nblk - 1) + SHIFT_ROWS

    def build_table(chan, s_ref):
        krow = k_ref[pl.ds(jnp.minimum(chan, nchan - 1), 1), :]
        kb = jnp.broadcast_to(krow, (SHIFT_ROWS, n2))
        s_ref[...] = pltpu.roll(kb, off, axis=1, stride=1, stride_axis=0).astype(s_ref.dtype)

    def convolve(ci, s_ref, acc_ref):
        acc_ref[...] = d_ref[pl.ds(ci, 1), :] * v_ref[ci].astype(jnp.float32)
        for d in range(-(nblk - 1), nblk):
            x0 = TBLK * (d + nblk - 1)
            w = jnp.concatenate(
                [s_ref[:, x0 + SHIFT_ROWS:x0 + SHIFT_ROWS + TBLK], s_ref[:, x0:x0 + TBLK]], axis=0)
            n = (nblk - abs(d)) * bsz
            src = 0 if d >= 0 else -d * bsz
            dst = d * bsz if d >= 0 else 0
            acc_ref[dst:dst + n, :] += jnp.dot(v_ref[ci, pl.ds(src, n), :], w,
                                               preferred_element_type=jnp.float32)
        y_ref[ci] = (acc_ref[...] * gate_ref[ci].astype(jnp.float32)).astype(y_ref.dtype)

    @pl.when(step == 0)
    def _():
        for u in range(HY_UNROLL):
            build_table(u, tables[u])

    def body(it, carry):
        for u in range(HY_UNROLL):
            convolve(it * HY_UNROLL + u, tables[u], accs[u])
        for u in range(HY_UNROLL):
            build_table(step * cg + (it + 1) * HY_UNROLL + u, tables[u])
        return carry

    lax.fori_loop(0, cg // HY_UNROLL, body, 0)


def _hyena_conv(kt, hyena_d, v_rows, gate_rows, cg=8):
    C, rows, _ = v_rows.shape
    n2 = kt.shape[1]
    return pl.pallas_call(
        _hyena_kernel,
        grid=(C // cg,),
        in_specs=[
            pl.BlockSpec((C, n2), lambda c: (0, 0)),
            pl.BlockSpec((cg, 1), lambda c: (c, 0)),
            pl.BlockSpec((cg, rows, TBLK), lambda c: (c, 0, 0)),
            pl.BlockSpec((cg, rows, TBLK), lambda c: (c, 0, 0)),
        ],
        out_specs=pl.BlockSpec((cg, rows, TBLK), lambda c: (c, 0, 0)),
        out_shape=jax.ShapeDtypeStruct((C, rows, TBLK), jnp.bfloat16),
        scratch_shapes=(
            [pltpu.VMEM((SHIFT_ROWS, n2), jnp.bfloat16)] * HY_UNROLL
            + [pltpu.VMEM((rows, TBLK), jnp.float32)] * HY_UNROLL),
        compiler_params=pltpu.CompilerParams(
            dimension_semantics=("arbitrary",), vmem_limit_bytes=VMEM_LIMIT),
        name="hyena_conv",
    )(kt, hyena_d[:, None], v_rows, gate_rows)


def _outproj_kernel(x_ref, yh_ref, ya_ref, w_ref, g_ref, o_ref):
    y = jnp.dot(jnp.concatenate([yh_ref[0], ya_ref[0]], axis=1), w_ref[...],
                preferred_element_type=jnp.float32)
    ms = jnp.mean(y * y, axis=-1, keepdims=True)
    o_ref[0] = x_ref[0] + y * lax.rsqrt(ms + RMS_EPS) * g_ref[...]


def _outproj(x, yh, ya, w_out, post_g, tm=1024):
    B, L, D = x.shape
    return pl.pallas_call(
        _outproj_kernel,
        grid=(B, L // tm),
        in_specs=[
            pl.BlockSpec((1, tm, D), lambda b, t: (b, t, 0)),
            pl.BlockSpec((1, tm, D_HYENA), lambda b, t: (b, t, 0)),
            pl.BlockSpec((1, tm, D_ATTN), lambda b, t: (b, t, 0)),
            pl.BlockSpec((D_HYENA + D_ATTN, D), lambda b, t: (0, 0)),
            pl.BlockSpec((1, D), lambda b, t: (0, 0)),
        ],
        out_specs=pl.BlockSpec((1, tm, D), lambda b, t: (b, t, 0)),
        out_shape=jax.ShapeDtypeStruct((B, L, D), jnp.float32),
        compiler_params=pltpu.CompilerParams(
            dimension_semantics=("parallel", "parallel"), vmem_limit_bytes=VMEM_LIMIT),
        name="outproj",
    )(x, yh, ya, w_out, post_g[None, :])


def _layer(x, pre_g, w_in, w_short, b_short, w_f1, b_f1, w_f2, b_f2, w_f3, b_f3, w_f4,
           sin_freq, hyena_d, attn_sink, w_out, post_g):
    B, L, _ = x.shape
    nblk = L // TBLK
    o_q = 4 * D_HYENA
    o_k = o_q + D_ATTN
    o_v = o_k + KV_DIM
    o_ag = o_v + KV_DIM
    w_cat = jnp.concatenate(
        [w_in[:, :o_q], w_in[:, o_q:o_k], w_in[:, o_ag:], w_in[:, o_k:o_v], w_in[:, o_v:o_ag]],
        axis=1).astype(jnp.bfloat16)

    kt = _hyena_filter(L, w_f1, b_f1, w_f2, b_f2, w_f3, b_f3, w_f4, sin_freq)
    v, x0g, za = _inproj(x, pre_g, w_cat, w_short, b_short)
    ya = _attention(za, attn_sink)
    def to_rows(a):
        return a.reshape(B, nblk, TBLK, D_HYENA).transpose(3, 1, 0, 2).reshape(D_HYENA, nblk * B, TBLK)

    y_rows = _hyena_conv(kt, hyena_d, to_rows(v), to_rows(x0g))
    yh = y_rows.reshape(D_HYENA, nblk, B, TBLK).transpose(2, 1, 3, 0).reshape(B, L, D_HYENA)
    return _outproj(x, yh, ya, w_out.astype(jnp.bfloat16), post_g)


def kernel(x, pre_g, w_in, w_short, b_short, w_f1, b_f1, w_f2, b_f2, w_f3, b_f3, w_f4, sin_freq, hyena_d, attn_sink, w_out, post_g):
    depth = pre_g.shape[0]
    for l in range(depth):
        x = _layer(x, pre_g[l], w_in[l], w_short[l], b_short[l], w_f1[l], b_f1[l], w_f2[l],
                   b_f2[l], w_f3[l], b_f3[l], w_f4[l], sin_freq[l], hyena_d[l], attn_sink[l],
                   w_out[l], post_g[l])
    return x
```

```python
import functools
import math

import jax
import jax.numpy as jnp
import numpy as np
from jax import lax
from jax.experimental import pallas as pl
from jax.experimental.pallas import tpu as pltpu

D_MODEL = 1024
D_HYENA = 512
D_ATTN = 512
N_HEADS = 8
HEAD_DIM = 64
N_KV_HEADS = 2
Q_PER_KV = N_HEADS // N_KV_HEADS
KV_DIM = N_KV_HEADS * HEAD_DIM
WINDOW = 128
BLOCK = 128
FILTER_ORDER = 64
N_BANDS = 16
POS_EMB_DIM = 1 + 2 * N_BANDS
POS_EMB_PAD = 40
DECAY_TARGET = 1e-2
FAST_DECAY_PCT = 0.3
SLOW_DECAY_PCT = 1.5
RMS_EPS = 1e-6
NEG_INF = -1e30
PV_HEADS = 2
LOG2E = math.log2(math.e)
Q_SCALE_LOG2 = HEAD_DIM ** -0.5 * LOG2E

N_HY = 3 * D_HYENA
O_GH = N_HY
O_ATT = O_GH + D_HYENA
N_ATT = 2 * D_ATTN + 2 * KV_DIM
D_IN = O_ATT + N_ATT

LANES = 128
HALO = 16

TBLK = 256
SHIFT_ROWS = 128
HY_UNROLL = 4

VMEM_LIMIT = 56 * 1024 * 1024

_HI = lax.Precision.HIGHEST


def _filter_consts(L):
    t = np.arange(L, dtype=np.float32)
    t_norm = t / np.float32(max(L - 1, 1))
    w = np.float32(2.0 * math.pi) * t / np.float32(L)
    bands = np.linspace(1e-4, N_BANDS - 1, N_BANDS).astype(np.float32)
    ang = w[:, None] * bands[None, :]
    z = np.concatenate([t_norm[:, None], np.cos(ang), -np.sin(ang)], axis=-1)
    min_decay = math.log(DECAY_TARGET) / SLOW_DECAY_PCT
    max_decay = math.log(DECAY_TARGET) / FAST_DECAY_PCT
    deltas = np.abs(np.linspace(min_decay, max_decay, D_HYENA)).astype(np.float32)
    decay = np.exp(-t_norm[:, None] * deltas[None, :]).astype(np.float32)
    pad = np.zeros((POS_EMB_PAD - POS_EMB_DIM, L), np.float32)
    zt = np.concatenate([z.T, pad], axis=0)
    flip = np.eye(LANES, dtype=np.float32)[::-1]
    return zt, np.ascontiguousarray(decay.T), np.ascontiguousarray(flip)


def _filter_kernel(zt_ref, dec_ref, flip_ref, w1_ref, b1_ref, w2_ref, b2_ref,
                   w3_ref, b3_ref, w4_ref, fr_ref, k_ref):
    L = zt_ref.shape[1]
    f32 = jnp.float32
    h = jnp.dot(w1_ref[...], zt_ref[...], precision=_HI, preferred_element_type=f32)
    h = jnp.sin(fr_ref[:, 0:1] * (h + b1_ref[...]))
    h = jnp.dot(w2_ref[...], h, precision=_HI, preferred_element_type=f32)
    h = jnp.sin(fr_ref[:, 1:2] * (h + b2_ref[...]))
    h = jnp.dot(w3_ref[...], h, precision=_HI, preferred_element_type=f32)
    h = jnp.sin(fr_ref[:, 2:3] * (h + b3_ref[...]))
    hf = jnp.dot(w4_ref[0:D_HYENA, :], h, precision=_HI, preferred_element_type=f32) * dec_ref[...]
    hb = jnp.dot(w4_ref[D_HYENA:, :], h, precision=_HI, preferred_element_type=f32) * dec_ref[...]
    nt = L // LANES
    rev = jnp.concatenate(
        [jnp.dot(hb[:, (nt - 1 - j) * LANES:(nt - j) * LANES], flip_ref[...], precision=_HI,
                 preferred_element_type=f32) for j in range(nt)], axis=1)
    col = lax.broadcasted_iota(jnp.int32, rev.shape, 1)
    hb2 = jnp.where(col == 0, 0.0, pltpu.roll(rev, 1, axis=1))
    ss = jnp.sum(hf * hf, axis=1, keepdims=True) + jnp.sum(hb2 * hb2, axis=1, keepdims=True)
    inv = lax.rsqrt(ss + 1e-12)
    k_ref[:, 0:L] = hf * inv
    k_ref[:, L:2 * L] = hb2 * inv


def _hyena_filter(L, w_f1, b_f1, w_f2, b_f2, w_f3, b_f3, w_f4, sin_freq):
    zt, dec, flip = _filter_consts(L)
    w1t = jnp.pad(w_f1.T, ((0, 0), (0, POS_EMB_PAD - POS_EMB_DIM)))
    args = (jnp.asarray(zt), jnp.asarray(dec), jnp.asarray(flip),
            w1t, b_f1[:, None], w_f2.T, b_f2[:, None], w_f3.T, b_f3[:, None],
            w_f4.T, sin_freq.T)
    return pl.pallas_call(
        _filter_kernel,
        out_shape=jax.ShapeDtypeStruct((D_HYENA, 2 * L), jnp.float32),
        compiler_params=pltpu.CompilerParams(vmem_limit_bytes=VMEM_LIMIT),
        name="hyena_filter",
    )(*args)


def _inproj_kernel(xp_ref, x_ref, xn_ref, g_ref, w_ref, wsh_ref, bsh_ref,
                   v_ref, x0g_ref, za_ref):
    t = pl.program_id(1)
    nt = pl.num_programs(1)
    tm = x_ref.shape[1]

    def norm(xv):
        ms = jnp.mean(xv * xv, axis=-1, keepdims=True)
        return (xv * lax.rsqrt(ms + RMS_EPS) * g_ref[...]).astype(jnp.bfloat16)

    h = norm(x_ref[0])
    hp = jnp.where(t > 0, norm(xp_ref[0]), 0.0).astype(jnp.bfloat16)
    hn = jnp.where(t < nt - 1, norm(xn_ref[0]), 0.0).astype(jnp.bfloat16)
    hall = jnp.concatenate([hp, h, hn], axis=0)
    zh = jnp.dot(hall, w_ref[:, 0:N_HY], preferred_element_type=jnp.float32)
    z = jnp.dot(h, w_ref[:, N_HY:], preferred_element_type=jnp.float32)

    rows = tm + 2 * HALO
    u_prev = pltpu.roll(zh, 1, axis=0)[HALO:HALO + tm]
    u = zh[HALO:HALO + tm]
    u_next = pltpu.roll(zh, rows - 1, axis=0)[HALO:HALO + tm]
    uc = (u_prev * wsh_ref[0:1, :] + u * wsh_ref[1:2, :] + u_next * wsh_ref[2:3, :]
          + bsh_ref[...])
    x0 = uc[:, 0:D_HYENA]
    x1 = uc[:, D_HYENA:2 * D_HYENA]
    vv = uc[:, 2 * D_HYENA:]
    gh = z[:, 0:D_HYENA]
    v_ref[0] = (vv * x1).astype(v_ref.dtype)
    x0g_ref[0] = (x0 * (gh * jax.nn.sigmoid(gh))).astype(x0g_ref.dtype)
    o_q = O_ATT - N_HY
    za_ref[0, :, 0:D_ATTN] = (z[:, o_q:o_q + D_ATTN] * Q_SCALE_LOG2).astype(za_ref.dtype)
    ga = z[:, o_q + D_ATTN:o_q + 2 * D_ATTN]
    za_ref[0, :, D_ATTN:2 * D_ATTN] = (ga * jax.nn.sigmoid(ga)).astype(za_ref.dtype)
    za_ref[0, :, 2 * D_ATTN:] = z[:, o_q + 2 * D_ATTN:].astype(za_ref.dtype)


def _inproj(x, pre_g, w_cat, w_short, b_short, tm=1024):
    B, L, D = x.shape
    rb = tm // HALO
    nrb = L // HALO
    grid = (B, L // tm)
    return pl.pallas_call(
        _inproj_kernel,
        grid=grid,
        in_specs=[
            pl.BlockSpec((1, HALO, D), lambda b, t: (b, jnp.maximum(t * rb - 1, 0), 0)),
            pl.BlockSpec((1, tm, D), lambda b, t: (b, t, 0)),
            pl.BlockSpec((1, HALO, D), lambda b, t: (b, jnp.minimum((t + 1) * rb, nrb - 1), 0)),
            pl.BlockSpec((1, D), lambda b, t: (0, 0)),
            pl.BlockSpec((D, D_IN), lambda b, t: (0, 0)),
            pl.BlockSpec((3, N_HY), lambda b, t: (0, 0)),
            pl.BlockSpec((1, N_HY), lambda b, t: (0, 0)),
        ],
        out_specs=[
            pl.BlockSpec((1, tm, D_HYENA), lambda b, t: (b, t, 0)),
            pl.BlockSpec((1, tm, D_HYENA), lambda b, t: (b, t, 0)),
            pl.BlockSpec((1, tm, N_ATT), lambda b, t: (b, t, 0)),
        ],
        out_shape=[
            jax.ShapeDtypeStruct((B, L, D_HYENA), jnp.bfloat16),
            jax.ShapeDtypeStruct((B, L, D_HYENA), jnp.bfloat16),
            jax.ShapeDtypeStruct((B, L, N_ATT), jnp.bfloat16),
        ],
        compiler_params=pltpu.CompilerParams(
            dimension_semantics=("parallel", "arbitrary"), vmem_limit_bytes=VMEM_LIMIT),
        name="inproj",
    )(x, x, x, pre_g[None, :], w_cat, w_short, b_short[None, :])


def _alibi_slopes():
    return [float(v) for v in
            np.exp2(-8.0 * np.arange(1, N_HEADS + 1, dtype=np.float32) / N_HEADS).astype(np.float32)]


def _attn_bias_tables():
    c = np.arange(BLOCK)[:, None]
    r = np.arange(BLOCK)[None, :]
    dist = np.abs(c - r).astype(np.float32)
    slopes = np.asarray(_alibi_slopes(), np.float32)[:, None, None] * np.float32(LOG2E)
    return np.concatenate([-slopes * dist, -slopes * (WINDOW - dist)], axis=1).astype(np.float32)


def _attn_kernel(sink_ref, bias_ref, za_ref, ya_ref, kpl_ref, vt_ref, s_ref, ot_ref, mask_ref):
    L = za_ref.shape[1]
    nb = L // BLOCK
    o_g = D_ATTN
    o_k = 2 * D_ATTN
    o_v = o_k + KV_DIM
    f32 = jnp.float32
    bf16 = jnp.bfloat16
    slopes = _alibi_slopes()
    group = Q_PER_KV * HEAD_DIM

    lane = lax.broadcasted_iota(jnp.int32, (BLOCK, KV_DIM), 1)
    lo = lane < HEAD_DIM
    ztile = jnp.zeros((BLOCK, KV_DIM), bf16)
    for idx in range(2 * N_KV_HEADS):
        kpl_ref[idx, 0] = ztile
        kpl_ref[idx, nb + 1] = ztile
    vt_ref[0] = ztile
    vt_ref[nb + 1] = ztile

    def prep(j, carry):
        start = pl.multiple_of(j * BLOCK, BLOCK)
        kc = za_ref[0, pl.ds(start, BLOCK), o_k:o_k + KV_DIM].astype(f32)
        ksw = pltpu.roll(kc, HEAD_DIM, axis=1)
        kpl_ref[0, j + 1] = jnp.where(lo, kc, 0.0).astype(bf16)
        kpl_ref[1, j + 1] = jnp.where(lo, 0.0, ksw).astype(bf16)
        kpl_ref[2, j + 1] = jnp.where(lo, ksw, 0.0).astype(bf16)
        kpl_ref[3, j + 1] = jnp.where(lo, 0.0, kc).astype(bf16)
        vc = za_ref[0, pl.ds(start, BLOCK), o_v:o_v + KV_DIM].astype(f32)
        vt_ref[j + 1] = vc.T.astype(bf16)
        return carry

    lax.fori_loop(0, nb, prep, 0, unroll=4)

    c_i = lax.broadcasted_iota(jnp.int32, (BLOCK, BLOCK), 0)
    r_i = lax.broadcasted_iota(jnp.int32, (BLOCK, BLOCK), 1)
    upper = c_i >= r_i
    diag = c_i == r_i
    mask_ref[...] = jnp.where(upper, 1.0, 0.0).astype(bf16)
    ones_rows = jnp.ones((16, 3 * BLOCK), bf16)

    def scores(i, slot):
        start = pl.multiple_of(i * BLOCK, BLOCK)
        qblk = za_ref[0, pl.ds(start, BLOCK), 0:D_ATTN]
        for kv in range(N_KV_HEADS):
            base = kv * group
            qs = jnp.concatenate([qblk[:, base:base + 2 * HEAD_DIM],
                                  qblk[:, base + 2 * HEAD_DIM:base + group]], axis=0)
            kp = jnp.concatenate([kpl_ref[2 * kv + half, i + t]
                                  for half in range(2) for t in range(3)], axis=0)
            s_ref[slot, kv] = lax.dot_general(kp, qs, (((1,), (1,)), ((), ())),
                                              preferred_element_type=f32)

    def softmax_pv(i, slot, first, last):
        for kv in range(N_KV_HEADS):
            rows = slice(kv * HEAD_DIM, (kv + 1) * HEAD_DIM)
            vt3 = vt_ref[i + 2, rows, :]
            vtw = jnp.concatenate([vt_ref[i, rows, :], vt_ref[i + 1, rows, :], vt3], axis=1)
            pts, extras, pxs = [], [], []
            for g in range(Q_PER_KV):
                half, pair = g % 2, g // 2
                h = kv * Q_PER_KV + g
                r0 = half * 3 * BLOCK
                cols = slice(pair * BLOCK, (pair + 1) * BLOCK)
                s1 = s_ref[slot, kv, r0:r0 + BLOCK, cols]
                s2 = s_ref[slot, kv, r0 + BLOCK:r0 + 2 * BLOCK, cols]
                s3 = s_ref[slot, kv, r0 + 2 * BLOCK:r0 + 3 * BLOCK, cols]
                if first:
                    mrg = jnp.where(upper, NEG_INF, s3)
                elif last:
                    mrg = jnp.where(upper, s1, NEG_INF)
                else:
                    mrg = jnp.where(upper, s1, s3)
                a_mid = s2 + bias_ref[h, 0:BLOCK, :]
                a_mrg = mrg + bias_ref[h, BLOCK:, :]
                sink2 = sink_ref[h] * LOG2E
                m = jnp.max(jnp.maximum(a_mid, a_mrg), axis=0, keepdims=True)
                m = jnp.maximum(m, sink2)
                if not last:
                    s_x = (jnp.sum(jnp.where(diag, s3, 0.0), axis=0, keepdims=True)
                           - slopes[h] * LOG2E * WINDOW)
                    m = jnp.maximum(m, s_x)
                p_mid = jnp.exp2(a_mid - m).astype(bf16)
                p_mrg = jnp.exp2(a_mrg - m).astype(bf16)
                extra = jnp.exp2(sink2 - m)
                if not last:
                    p_x = jnp.exp2(s_x - m)
                    extra = extra + p_x
                    pxs.append(p_x)
                extras.append(extra)
                p1 = p_mrg * mask_ref[...]
                p3 = p_mrg - p1
                pts.append(jnp.concatenate([p1, p_mid, p3], axis=0))
                if len(pts) < PV_HEADS:
                    continue
                pt = jnp.concatenate(pts, axis=1)
                ot = jnp.dot(jnp.concatenate([vtw, ones_rows], axis=0), pt,
                             preferred_element_type=f32)
                for n in range(PV_HEADS):
                    hh = h - (PV_HEADS - 1) + n
                    cols = slice(n * BLOCK, (n + 1) * BLOCK)
                    o = ot[0:HEAD_DIM, cols]
                    if not last:
                        o = o + vt3.astype(f32) * pxs[n]
                    inv = 1.0 / (ot[HEAD_DIM:HEAD_DIM + 1, cols] + extras[n])
                    ot_ref[slot, hh * HEAD_DIM:(hh + 1) * HEAD_DIM, :] = o * inv
                pts, extras, pxs = [], [], []

    def finish(i, slot):
        start = pl.multiple_of(i * BLOCK, BLOCK)
        gate = za_ref[0, pl.ds(start, BLOCK), o_g:o_g + D_ATTN].astype(f32)
        ya_ref[0, pl.ds(start, BLOCK), :] = (ot_ref[slot].T * gate).astype(ya_ref.dtype)

    def two_blocks(j, carry):
        i = 2 * j + 1
        scores(i + 1, 0)
        softmax_pv(i, 1, False, False)
        finish(i - 1, 0)
        scores(i + 2, 1)
        softmax_pv(i + 1, 0, False, False)
        finish(i, 1)
        return carry

    scores(0, 0)
    scores(1, 1)
    softmax_pv(0, 0, True, False)
    lax.fori_loop(0, (nb - 2) // 2, two_blocks, 0)
    softmax_pv(nb - 1, 1, False, True)
    finish(nb - 2, 0)
    finish(nb - 1, 1)


def _attention(za, sink):
    B, L, _ = za.shape
    nb = L // BLOCK
    assert WINDOW == BLOCK and nb >= 4 and nb % 2 == 0
    nkt = nb + 2
    bias = jnp.asarray(_attn_bias_tables())
    return pl.pallas_call(
        _attn_kernel,
        grid=(B,),
        in_specs=[
            pl.BlockSpec(memory_space=pltpu.SMEM),
            pl.BlockSpec((N_HEADS, 2 * BLOCK, BLOCK), lambda b: (0, 0, 0)),
            pl.BlockSpec((1, L, N_ATT), lambda b: (b, 0, 0)),
        ],
        out_specs=pl.BlockSpec((1, L, D_ATTN), lambda b: (b, 0, 0)),
        out_shape=jax.ShapeDtypeStruct((B, L, D_ATTN), jnp.bfloat16),
        scratch_shapes=[
            pltpu.VMEM((2 * N_KV_HEADS, nkt, BLOCK, KV_DIM), jnp.bfloat16),
            pltpu.VMEM((nkt, KV_DIM, BLOCK), jnp.bfloat16),
            pltpu.VMEM((2, N_KV_HEADS, 6 * BLOCK, 2 * BLOCK), jnp.float32),
            pltpu.VMEM((2, D_ATTN, BLOCK), jnp.float32),
            pltpu.VMEM((BLOCK, BLOCK), jnp.bfloat16),
        ],
        compiler_params=pltpu.CompilerParams(
            dimension_semantics=("parallel",), vmem_limit_bytes=VMEM_LIMIT),
        name="swa_attention",
    )(sink, bias, za)


def _hyena_kernel(k_ref, d_ref, v_ref, y_ref, *scratch):
    tables = scratch[:HY_UNROLL]
    accs = scratch[HY_UNROLL:]
    step = pl.program_id(0)
    cg = v_ref.shape[0]
    rows = v_ref.shape[1]
    nchan, n2 = k_ref.shape
    nblk = n2 // (2 * TBLK)
    bsz = rows // nblk
    off = TBLK * (nblk - 1) + SHIFT_ROWS

    def build_table(chan, s_ref):
        krow = k_ref[pl.ds(jnp.minimum(chan, nchan - 1), 1), :]
        kb = jnp.broadcast_to(krow, (SHIFT_ROWS, n2))
        s_ref[...] = pltpu.roll(kb, off, axis=1, stride=1, stride_axis=0).astype(s_ref.dtype)

    def convolve(ci, s_ref, acc_ref):
        acc_ref[...] = d_ref[pl.ds(ci, 1), :] * v_ref[ci].astype(jnp.float32)
        for d in range(-(nblk - 1), nblk):
            x0 = TBLK * (d + nblk - 1)
            w = jnp.concatenate(
                [s_ref[:, x0 + SHIFT_ROWS:x0 + SHIFT_ROWS + TBLK], s_ref[:, x0:x0 + TBLK]], axis=0)
            n = (nblk - abs(d)) * bsz
            src = 0 if d >= 0 else -d * bsz
            dst = d * bsz if d >= 0 else 0
            acc_ref[dst:dst + n, :] += jnp.dot(v_ref[ci, pl.ds(src, n), :], w,
                                               preferred_element_type=jnp.float32)
        y_ref[ci] = acc_ref[...].astype(y_ref.dtype)

    @pl.when(step == 0)
    def _():
        for u in range(HY_UNROLL):
            build_table(u, tables[u])

    def body(it, carry):
        for u in range(HY_UNROLL):
            convolve(it * HY_UNROLL + u, tables[u], accs[u])
        for u in range(HY_UNROLL):
            build_table(step * cg + (it + 1) * HY_UNROLL + u, tables[u])
        return carry

    lax.fori_loop(0, cg // HY_UNROLL, body, 0)


def _hyena_conv(kt, hyena_d, v_rows, cg=16):
    C, rows, _ = v_rows.shape
    n2 = kt.shape[1]
    return pl.pallas_call(
        _hyena_kernel,
        grid=(C // cg,),
        in_specs=[
            pl.BlockSpec((C, n2), lambda c: (0, 0)),
            pl.BlockSpec((cg, 1), lambda c: (c, 0)),
            pl.BlockSpec((cg, rows, TBLK), lambda c: (c, 0, 0)),
        ],
        out_specs=pl.BlockSpec((cg, rows, TBLK), lambda c: (c, 0, 0)),
        out_shape=jax.ShapeDtypeStruct((C, rows, TBLK), jnp.bfloat16),
        scratch_shapes=(
            [pltpu.VMEM((SHIFT_ROWS, n2), jnp.bfloat16)] * HY_UNROLL
            + [pltpu.VMEM((rows, TBLK), jnp.float32)] * HY_UNROLL),
        compiler_params=pltpu.CompilerParams(
            dimension_semantics=("arbitrary",), vmem_limit_bytes=VMEM_LIMIT),
        name="hyena_conv",
    )(kt, hyena_d[:, None], v_rows)


def _outproj_kernel(x_ref, yh_ref, x0g_ref, ya_ref, w_ref, g_ref, o_ref):
    yh = (yh_ref[0].astype(jnp.float32) * x0g_ref[0].astype(jnp.float32)).astype(jnp.bfloat16)
    y = jnp.dot(jnp.concatenate([yh, ya_ref[0]], axis=1), w_ref[...],
                preferred_element_type=jnp.float32)
    ms = jnp.mean(y * y, axis=-1, keepdims=True)
    o_ref[0] = x_ref[0] + y * lax.rsqrt(ms + RMS_EPS) * g_ref[...]


def _outproj(x, yh, x0g, ya, w_out, post_g, tm=1024):
    B, L, D = x.shape
    return pl.pallas_call(
        _outproj_kernel,
        grid=(B, L // tm),
        in_specs=[
            pl.BlockSpec((1, tm, D), lambda b, t: (b, t, 0)),
            pl.BlockSpec((1, tm, D_HYENA), lambda b, t: (b, t, 0)),
            pl.BlockSpec((1, tm, D_HYENA), lambda b, t: (b, t, 0)),
            pl.BlockSpec((1, tm, D_ATTN), lambda b, t: (b, t, 0)),
            pl.BlockSpec((D_HYENA + D_ATTN, D), lambda b, t: (0, 0)),
            pl.BlockSpec((1, D), lambda b, t: (0, 0)),
        ],
        out_specs=pl.BlockSpec((1, tm, D), lambda b, t: (b, t, 0)),
        out_shape=jax.ShapeDtypeStruct((B, L, D), jnp.float32),
        compiler_params=pltpu.CompilerParams(
            dimension_semantics=("parallel", "parallel"), vmem_limit_bytes=VMEM_LIMIT),
        name="outproj",
    )(x, yh, x0g, ya, w_out, post_g[None, :])


def _layer(x, pre_g, w_in, w_short, b_short, w_f1, b_f1, w_f2, b_f2, w_f3, b_f3, w_f4,
           sin_freq, hyena_d, attn_sink, w_out, post_g):
    B, L, _ = x.shape
    nblk = L // TBLK
    o_q = 4 * D_HYENA
    o_k = o_q + D_ATTN
    o_v = o_k + KV_DIM
    o_ag = o_v + KV_DIM
    w_cat = jnp.concatenate(
        [w_in[:, :o_q], w_in[:, o_q:o_k], w_in[:, o_ag:], w_in[:, o_k:o_v], w_in[:, o_v:o_ag]],
        axis=1).astype(jnp.bfloat16)

    kt = _hyena_filter(L, w_f1, b_f1, w_f2, b_f2, w_f3, b_f3, w_f4, sin_freq)
    v, x0g, za = _inproj(x, pre_g, w_cat, w_short, b_short)
    ya = _attention(za, attn_sink)
    def to_rows(a):
        return a.reshape(B, nblk, TBLK, D_HYENA).transpose(3, 1, 0, 2).reshape(D_HYENA, nblk * B, TBLK)

    y_rows = _hyena_conv(kt, hyena_d, to_rows(v))
    yh = y_rows.reshape(D_HYENA, nblk, B, TBLK).transpose(2, 1, 3, 0).reshape(B, L, D_HYENA)
    return _outproj(x, yh, x0g, ya, w_out.astype(jnp.bfloat16), post_g)


def kernel(x, pre_g, w_in, w_short, b_short, w_f1, b_f1, w_f2, b_f2, w_f3, b_f3, w_f4, sin_freq, hyena_d, attn_sink, w_out, post_g):
    depth = pre_g.shape[0]
    for l in range(depth):
        x = _layer(x, pre_g[l], w_in[l], w_short[l], b_short[l], w_f1[l], b_f1[l], w_f2[l],
                   b_f2[l], w_f3[l], b_f3[l], w_f4[l], sin_freq[l], hyena_d[l], attn_sink[l],
                   w_out[l], post_g[l])
    return x
```

```python
import functools
import math

import jax
import jax.numpy as jnp
import numpy as np
from jax import lax
from jax.experimental import pallas as pl
from jax.experimental.pallas import tpu as pltpu

D_MODEL = 1024
D_HYENA = 512
D_ATTN = 512
N_HEADS = 8
HEAD_DIM = 64
N_KV_HEADS = 2
Q_PER_KV = N_HEADS // N_KV_HEADS
KV_DIM = N_KV_HEADS * HEAD_DIM
WINDOW = 128
BLOCK = 128
FILTER_ORDER = 64
N_BANDS = 16
POS_EMB_DIM = 1 + 2 * N_BANDS
POS_EMB_PAD = 40
DECAY_TARGET = 1e-2
FAST_DECAY_PCT = 0.3
SLOW_DECAY_PCT = 1.5
RMS_EPS = 1e-6
NEG_INF = -1e30
PV_HEADS = 2
LOG2E = math.log2(math.e)
Q_SCALE_LOG2 = HEAD_DIM ** -0.5 * LOG2E

N_HY = 3 * D_HYENA
O_GH = N_HY
O_ATT = O_GH + D_HYENA
N_ATT = 2 * D_ATTN + 2 * KV_DIM
D_IN = O_ATT + N_ATT

LANES = 128
HALO = 16

TBLK = 256
SHIFT_ROWS = 128
HY_UNROLL = 4

VMEM_LIMIT = 56 * 1024 * 1024

_HI = lax.Precision.HIGHEST


def _filter_consts(L):
    t = np.arange(L, dtype=np.float32)
    t_norm = t / np.float32(max(L - 1, 1))
    w = np.float32(2.0 * math.pi) * t / np.float32(L)
    bands = np.linspace(1e-4, N_BANDS - 1, N_BANDS).astype(np.float32)
    ang = w[:, None] * bands[None, :]
    z = np.concatenate([t_norm[:, None], np.cos(ang), -np.sin(ang)], axis=-1)
    min_decay = math.log(DECAY_TARGET) / SLOW_DECAY_PCT
    max_decay = math.log(DECAY_TARGET) / FAST_DECAY_PCT
    deltas = np.abs(np.linspace(min_decay, max_decay, D_HYENA)).astype(np.float32)
    decay = np.exp(-t_norm[:, None] * deltas[None, :]).astype(np.float32)
    pad = np.zeros((POS_EMB_PAD - POS_EMB_DIM, L), np.float32)
    zt = np.concatenate([z.T, pad], axis=0)
    flip = np.eye(LANES, dtype=np.float32)[::-1]
    return zt, np.ascontiguousarray(decay.T), np.ascontiguousarray(flip)


def _filter_kernel(zt_ref, dec_ref, flip_ref, w1_ref, b1_ref, w2_ref, b2_ref,
                   w3_ref, b3_ref, w4_ref, fr_ref, k_ref):
    L = zt_ref.shape[1]
    f32 = jnp.float32
    h = jnp.dot(w1_ref[...], zt_ref[...], precision=_HI, preferred_element_type=f32)
    h = jnp.sin(fr_ref[:, 0:1] * (h + b1_ref[...]))
    h = jnp.dot(w2_ref[...], h, precision=_HI, preferred_element_type=f32)
    h = jnp.sin(fr_ref[:, 1:2] * (h + b2_ref[...]))
    h = jnp.dot(w3_ref[...], h, precision=_HI, preferred_element_type=f32)
    h = jnp.sin(fr_ref[:, 2:3] * (h + b3_ref[...]))
    hf = jnp.dot(w4_ref[0:D_HYENA, :], h, precision=_HI, preferred_element_type=f32) * dec_ref[...]
    hb = jnp.dot(w4_ref[D_HYENA:, :], h, precision=_HI, preferred_element_type=f32) * dec_ref[...]
    nt = L // LANES
    rev = jnp.concatenate(
        [jnp.dot(hb[:, (nt - 1 - j) * LANES:(nt - j) * LANES], flip_ref[...], precision=_HI,
                 preferred_element_type=f32) for j in range(nt)], axis=1)
    col = lax.broadcasted_iota(jnp.int32, rev.shape, 1)
    hb2 = jnp.where(col == 0, 0.0, pltpu.roll(rev, 1, axis=1))
    ss = jnp.sum(hf * hf, axis=1, keepdims=True) + jnp.sum(hb2 * hb2, axis=1, keepdims=True)
    inv = lax.rsqrt(ss + 1e-12)
    k_ref[:, 0:L] = hf * inv
    k_ref[:, L:2 * L] = hb2 * inv


def _hyena_filter(L, w_f1, b_f1, w_f2, b_f2, w_f3, b_f3, w_f4, sin_freq):
    zt, dec, flip = _filter_consts(L)
    w1t = jnp.pad(w_f1.T, ((0, 0), (0, POS_EMB_PAD - POS_EMB_DIM)))
    args = (jnp.asarray(zt), jnp.asarray(dec), jnp.asarray(flip),
            w1t, b_f1[:, None], w_f2.T, b_f2[:, None], w_f3.T, b_f3[:, None],
            w_f4.T, sin_freq.T)
    return pl.pallas_call(
        _filter_kernel,
        out_shape=jax.ShapeDtypeStruct((D_HYENA, 2 * L), jnp.float32),
        compiler_params=pltpu.CompilerParams(vmem_limit_bytes=VMEM_LIMIT),
        name="hyena_filter",
    )(*args)


def _inproj_kernel(xp_ref, x_ref, xn_ref, g_ref, w_ref, wsh_ref, bsh_ref,
                   v_ref, x0g_ref, za_ref):
    t = pl.program_id(1)
    nt = pl.num_programs(1)
    tm = x_ref.shape[1]

    def norm(xv):
        ms = jnp.mean(xv * xv, axis=-1, keepdims=True)
        return (xv * lax.rsqrt(ms + RMS_EPS) * g_ref[...]).astype(jnp.bfloat16)

    h = norm(x_ref[0])
    hp = jnp.where(t > 0, norm(xp_ref[0]), 0.0).astype(jnp.bfloat16)
    hn = jnp.where(t < nt - 1, norm(xn_ref[0]), 0.0).astype(jnp.bfloat16)
    hall = jnp.concatenate([hp, h, hn], axis=0)
    zh = jnp.dot(hall, w_ref[:, 0:N_HY], preferred_element_type=jnp.float32)
    z = jnp.dot(h, w_ref[:, N_HY:], preferred_element_type=jnp.float32)

    rows = tm + 2 * HALO
    u_prev = pltpu.roll(zh, 1, axis=0)[HALO:HALO + tm]
    u = zh[HALO:HALO + tm]
    u_next = pltpu.roll(zh, rows - 1, axis=0)[HALO:HALO + tm]
    uc = (u_prev * wsh_ref[0:1, :] + u * wsh_ref[1:2, :] + u_next * wsh_ref[2:3, :]
          + bsh_ref[...])
    x0 = uc[:, 0:D_HYENA]
    x1 = uc[:, D_HYENA:2 * D_HYENA]
    vv = uc[:, 2 * D_HYENA:]
    gh = z[:, 0:D_HYENA]
    v_ref[0] = (vv * x1).astype(v_ref.dtype)
    x0g_ref[0] = (x0 * (gh * jax.nn.sigmoid(gh))).astype(x0g_ref.dtype)
    o_q = O_ATT - N_HY
    za_ref[0, :, 0:D_ATTN] = (z[:, o_q:o_q + D_ATTN] * Q_SCALE_LOG2).astype(za_ref.dtype)
    ga = z[:, o_q + D_ATTN:o_q + 2 * D_ATTN]
    za_ref[0, :, D_ATTN:2 * D_ATTN] = (ga * jax.nn.sigmoid(ga)).astype(za_ref.dtype)
    za_ref[0, :, 2 * D_ATTN:] = z[:, o_q + 2 * D_ATTN:].astype(za_ref.dtype)


def _inproj(x, pre_g, w_cat, w_short, b_short, tm=1024):
    B, L, D = x.shape
    rb = tm // HALO
    nrb = L // HALO
    grid = (B, L // tm)
    return pl.pallas_call(
        _inproj_kernel,
        grid=grid,
        in_specs=[
            pl.BlockSpec((1, HALO, D), lambda b, t: (b, jnp.maximum(t * rb - 1, 0), 0)),
            pl.BlockSpec((1, tm, D), lambda b, t: (b, t, 0)),
            pl.BlockSpec((1, HALO, D), lambda b, t: (b, jnp.minimum((t + 1) * rb, nrb - 1), 0)),
            pl.BlockSpec((1, D), lambda b, t: (0, 0)),
            pl.BlockSpec((D, D_IN), lambda b, t: (0, 0)),
            pl.BlockSpec((3, N_HY), lambda b, t: (0, 0)),
            pl.BlockSpec((1, N_HY), lambda b, t: (0, 0)),
        ],
        out_specs=[
            pl.BlockSpec((1, tm, D_HYENA), lambda b, t: (b, t, 0)),
            pl.BlockSpec((1, tm, D_HYENA), lambda b, t: (b, t, 0)),
            pl.BlockSpec((1, tm, N_ATT), lambda b, t: (b, t, 0)),
        ],
        out_shape=[
            jax.ShapeDtypeStruct((B, L, D_HYENA), jnp.bfloat16),
            jax.ShapeDtypeStruct((B, L, D_HYENA), jnp.bfloat16),
            jax.ShapeDtypeStruct((B, L, N_ATT), jnp.bfloat16),
        ],
        compiler_params=pltpu.CompilerParams(
            dimension_semantics=("parallel", "arbitrary"), vmem_limit_bytes=VMEM_LIMIT),
        name="inproj",
    )(x, x, x, pre_g[None, :], w_cat, w_short, b_short[None, :])


def _alibi_slopes():
    return [float(v) for v in
            np.exp2(-8.0 * np.arange(1, N_HEADS + 1, dtype=np.float32) / N_HEADS).astype(np.float32)]


def _attn_bias_tables():
    c = np.arange(BLOCK)[:, None]
    r = np.arange(BLOCK)[None, :]
    dist = np.abs(c - r).astype(np.float32)
    slopes = np.asarray(_alibi_slopes(), np.float32)[:, None, None] * np.float32(LOG2E)
    return np.concatenate([-slopes * dist, -slopes * (WINDOW - dist)], axis=1).astype(np.float32)


def _back_kernel(sink_ref, bias_ref, za_ref, x_ref, yh0_ref, yh1_ref, x0g_ref, w_ref, g_ref,
                 o_ref, kpl_ref, vt_ref, s_ref, ot_ref, mask_ref, ya_ref):
    t = pl.program_id(1)
    tm = x_ref.shape[1]
    L = za_ref.shape[1]
    nb = L // BLOCK
    nbs = tm // BLOCK
    i0 = t * nbs
    o_g = D_ATTN
    o_k = 2 * D_ATTN
    o_v = o_k + KV_DIM
    f32 = jnp.float32
    bf16 = jnp.bfloat16
    slopes = _alibi_slopes()
    group = Q_PER_KV * HEAD_DIM

    c_i = lax.broadcasted_iota(jnp.int32, (BLOCK, BLOCK), 0)
    r_i = lax.broadcasted_iota(jnp.int32, (BLOCK, BLOCK), 1)
    upper = c_i >= r_i
    diag = c_i == r_i
    ones_rows = jnp.ones((16, 3 * BLOCK), bf16)

    @pl.when(t == 0)
    def _():
        lane = lax.broadcasted_iota(jnp.int32, (BLOCK, KV_DIM), 1)
        lo = lane < HEAD_DIM
        ztile = jnp.zeros((BLOCK, KV_DIM), bf16)
        for idx in range(2 * N_KV_HEADS):
            kpl_ref[idx, 0] = ztile
            kpl_ref[idx, nb + 1] = ztile
        vt_ref[0] = ztile
        vt_ref[nb + 1] = ztile
        mask_ref[...] = jnp.where(upper, 1.0, 0.0).astype(bf16)

        def prep(j, carry):
            start = pl.multiple_of(j * BLOCK, BLOCK)
            kc = za_ref[0, pl.ds(start, BLOCK), o_k:o_k + KV_DIM].astype(f32)
            ksw = pltpu.roll(kc, HEAD_DIM, axis=1)
            kpl_ref[0, j + 1] = jnp.where(lo, kc, 0.0).astype(bf16)
            kpl_ref[1, j + 1] = jnp.where(lo, 0.0, ksw).astype(bf16)
            kpl_ref[2, j + 1] = jnp.where(lo, ksw, 0.0).astype(bf16)
            kpl_ref[3, j + 1] = jnp.where(lo, 0.0, kc).astype(bf16)
            vc = za_ref[0, pl.ds(start, BLOCK), o_v:o_v + KV_DIM].astype(f32)
            vt_ref[j + 1] = vc.T.astype(bf16)
            return carry

        lax.fori_loop(0, nb, prep, 0, unroll=4)

    def scores(i, slot):
        start = pl.multiple_of(i * BLOCK, BLOCK)
        qblk = za_ref[0, pl.ds(start, BLOCK), 0:D_ATTN]
        for kv in range(N_KV_HEADS):
            base = kv * group
            qs = jnp.concatenate([qblk[:, base:base + 2 * HEAD_DIM],
                                  qblk[:, base + 2 * HEAD_DIM:base + group]], axis=0)
            kp = jnp.concatenate([kpl_ref[2 * kv + half, i + tt]
                                  for half in range(2) for tt in range(3)], axis=0)
            s_ref[slot, kv] = lax.dot_general(kp, qs, (((1,), (1,)), ((), ())),
                                              preferred_element_type=f32)

    def softmax_pv(i, slot, edge):
        pen1 = jnp.where(i == 0, NEG_INF, 0.0).astype(f32)
        pen3 = jnp.where(i == nb - 1, NEG_INF, 0.0).astype(f32)
        for kv in range(N_KV_HEADS):
            rows = slice(kv * HEAD_DIM, (kv + 1) * HEAD_DIM)
            vt3 = vt_ref[i + 2, rows, :]
            vtw = jnp.concatenate([vt_ref[i, rows, :], vt_ref[i + 1, rows, :], vt3], axis=1)
            pts, extras, pxs = [], [], []
            for g in range(Q_PER_KV):
                half, pair = g % 2, g // 2
                h = kv * Q_PER_KV + g
                r0 = half * 3 * BLOCK
                cols = slice(pair * BLOCK, (pair + 1) * BLOCK)
                s1 = s_ref[slot, kv, r0:r0 + BLOCK, cols]
                s2 = s_ref[slot, kv, r0 + BLOCK:r0 + 2 * BLOCK, cols]
                s3 = s_ref[slot, kv, r0 + 2 * BLOCK:r0 + 3 * BLOCK, cols]
                if edge == "lo":
                    mrg = jnp.where(upper, s1 + pen1, s3)
                elif edge == "hi":
                    mrg = jnp.where(upper, s1, s3 + pen3)
                else:
                    mrg = jnp.where(upper, s1, s3)
                a_mid = s2 + bias_ref[h, 0:BLOCK, :]
                a_mrg = mrg + bias_ref[h, BLOCK:, :]
                sink2 = sink_ref[h] * LOG2E
                s_x = (jnp.sum(jnp.where(diag, s3, 0.0), axis=0, keepdims=True)
                       - slopes[h] * LOG2E * WINDOW)
                if edge == "hi":
                    s_x = s_x + pen3
                m = jnp.max(jnp.maximum(a_mid, a_mrg), axis=0, keepdims=True)
                m = jnp.maximum(jnp.maximum(m, sink2), s_x)
                p_mid = jnp.exp2(a_mid - m).astype(bf16)
                p_mrg = jnp.exp2(a_mrg - m).astype(bf16)
                p_x = jnp.exp2(s_x - m)
                pxs.append(p_x)
                extras.append(jnp.exp2(sink2 - m) + p_x)
                p1 = p_mrg * mask_ref[...]
                p3 = p_mrg - p1
                pts.append(jnp.concatenate([p1, p_mid, p3], axis=0))
                if len(pts) < PV_HEADS:
                    continue
                pt = jnp.concatenate(pts, axis=1)
                ot = jnp.dot(jnp.concatenate([vtw, ones_rows], axis=0), pt,
                             preferred_element_type=f32)
                for n in range(PV_HEADS):
                    hh = h - (PV_HEADS - 1) + n
                    cols = slice(n * BLOCK, (n + 1) * BLOCK)
                    o = ot[0:HEAD_DIM, cols] + vt3.astype(f32) * pxs[n]
                    inv = 1.0 / (ot[HEAD_DIM:HEAD_DIM + 1, cols] + extras[n])
                    ot_ref[slot, hh * HEAD_DIM:(hh + 1) * HEAD_DIM, :] = o * inv
                pts, extras, pxs = [], [], []

    def finish(i, slot):
        start = pl.multiple_of(i * BLOCK, BLOCK)
        local = pl.multiple_of((i - i0) * BLOCK, BLOCK)
        gate = za_ref[0, pl.ds(start, BLOCK), o_g:o_g + D_ATTN].astype(f32)
        ya_ref[pl.ds(local, BLOCK), :] = (ot_ref[slot].T * gate).astype(ya_ref.dtype)

    def two_blocks(j, carry):
        i = i0 + 2 * j + 1
        scores(i + 1, 0)
        softmax_pv(i, 1, None)
        finish(i - 1, 0)
        scores(i + 2, 1)
        softmax_pv(i + 1, 0, None)
        finish(i, 1)
        return carry

    scores(i0, 0)
    scores(i0 + 1, 1)
    softmax_pv(i0, 0, "lo")
    lax.fori_loop(0, (nbs - 2) // 2, two_blocks, 0)
    softmax_pv(i0 + nbs - 1, 1, "hi")
    finish(i0 + nbs - 2, 0)
    finish(i0 + nbs - 1, 1)

    x0g = x0g_ref[0].astype(f32)
    yh = jnp.concatenate([yh0_ref[0], yh1_ref[0]], axis=1).astype(f32)
    y = jnp.dot(jnp.concatenate([(yh * x0g).astype(bf16), ya_ref[...]], axis=1), w_ref[...],
                preferred_element_type=f32)
    ms = jnp.mean(y * y, axis=-1, keepdims=True)
    o_ref[0] = x_ref[0] + y * lax.rsqrt(ms + RMS_EPS) * g_ref[...]


def _back(za, sink, x, yh0, yh1, x0g, w_out, post_g, tm=1024):
    B, L, D = x.shape
    nb = L // BLOCK
    nbs = tm // BLOCK
    assert WINDOW == BLOCK and nbs >= 4 and nbs % 2 == 0 and L % tm == 0
    nkt = nb + 2
    half = D_HYENA // 2
    bias = jnp.asarray(_attn_bias_tables())
    return pl.pallas_call(
        _back_kernel,
        grid=(B, L // tm),
        in_specs=[
            pl.BlockSpec(memory_space=pltpu.SMEM),
            pl.BlockSpec((N_HEADS, 2 * BLOCK, BLOCK), lambda b, t: (0, 0, 0)),
            pl.BlockSpec((1, L, N_ATT), lambda b, t: (b, 0, 0)),
            pl.BlockSpec((1, tm, D), lambda b, t: (b, t, 0)),
            pl.BlockSpec((1, tm, half), lambda b, t: (b, t, 0)),
            pl.BlockSpec((1, tm, half), lambda b, t: (b, t, 0)),
            pl.BlockSpec((1, tm, D_HYENA), lambda b, t: (b, t, 0)),
            pl.BlockSpec((D_HYENA + D_ATTN, D), lambda b, t: (0, 0)),
            pl.BlockSpec((1, D), lambda b, t: (0, 0)),
        ],
        out_specs=pl.BlockSpec((1, tm, D), lambda b, t: (b, t, 0)),
        out_shape=jax.ShapeDtypeStruct((B, L, D), jnp.float32),
        scratch_shapes=[
            pltpu.VMEM((2 * N_KV_HEADS, nkt, BLOCK, KV_DIM), jnp.bfloat16),
            pltpu.VMEM((nkt, KV_DIM, BLOCK), jnp.bfloat16),
            pltpu.VMEM((2, N_KV_HEADS, 6 * BLOCK, 2 * BLOCK), jnp.float32),
            pltpu.VMEM((2, D_ATTN, BLOCK), jnp.float32),
            pltpu.VMEM((BLOCK, BLOCK), jnp.bfloat16),
            pltpu.VMEM((tm, D_ATTN), jnp.bfloat16),
        ],
        compiler_params=pltpu.CompilerParams(
            dimension_semantics=("parallel", "arbitrary"), vmem_limit_bytes=VMEM_LIMIT),
        name="attn_outproj",
    )(sink, bias, za, x, yh0, yh1, x0g, w_out, post_g[None, :])


def _hyena_kernel(k_ref, d_ref, v_ref, y_ref, *scratch):
    tables = scratch[:HY_UNROLL]
    accs = scratch[HY_UNROLL:]
    step = pl.program_id(0)
    cg = v_ref.shape[0]
    rows = v_ref.shape[1]
    nchan, n2 = k_ref.shape
    nblk = n2 // (2 * TBLK)
    bsz = rows // nblk
    off = TBLK * (nblk - 1) + SHIFT_ROWS

    def build_table(chan, s_ref):
        krow = k_ref[pl.ds(jnp.minimum(chan, nchan - 1), 1), :]
        kb = jnp.broadcast_to(krow, (SHIFT_ROWS, n2))
        s_ref[...] = pltpu.roll(kb, off, axis=1, stride=1, stride_axis=0).astype(s_ref.dtype)

    def convolve(ci, s_ref, acc_ref):
        acc_ref[...] = d_ref[pl.ds(ci, 1), :] * v_ref[ci].astype(jnp.float32)
        for d in range(-(nblk - 1), nblk):
            x0 = TBLK * (d + nblk - 1)
            w = jnp.concatenate(
                [s_ref[:, x0 + SHIFT_ROWS:x0 + SHIFT_ROWS + TBLK], s_ref[:, x0:x0 + TBLK]], axis=0)
            n = (nblk - abs(d)) * bsz
            src = 0 if d >= 0 else -d * bsz
            dst = d * bsz if d >= 0 else 0
            acc_ref[dst:dst + n, :] += jnp.dot(v_ref[ci, pl.ds(src, n), :], w,
                                               preferred_element_type=jnp.float32)
        y_ref[ci] = acc_ref[...].astype(y_ref.dtype)

    @pl.when(step == 0)
    def _():
        for u in range(HY_UNROLL):
            build_table(u, tables[u])

    def body(it, carry):
        for u in range(HY_UNROLL):
            convolve(it * HY_UNROLL + u, tables[u], accs[u])
        for u in range(HY_UNROLL):
            build_table(step * cg + (it + 1) * HY_UNROLL + u, tables[u])
        return carry

    lax.fori_loop(0, cg // HY_UNROLL, body, 0)


def _hyena_conv(kt, hyena_d, v_rows, cg=16):
    C, rows, _ = v_rows.shape
    n2 = kt.shape[1]
    return pl.pallas_call(
        _hyena_kernel,
        grid=(C // cg,),
        in_specs=[
            pl.BlockSpec((C, n2), lambda c: (0, 0)),
            pl.BlockSpec((cg, 1), lambda c: (c, 0)),
            pl.BlockSpec((cg, rows, TBLK), lambda c: (c, 0, 0)),
        ],
        out_specs=pl.BlockSpec((cg, rows, TBLK), lambda c: (c, 0, 0)),
        out_shape=jax.ShapeDtypeStruct((C, rows, TBLK), jnp.bfloat16),
        scratch_shapes=(
            [pltpu.VMEM((SHIFT_ROWS, n2), jnp.bfloat16)] * HY_UNROLL
            + [pltpu.VMEM((rows, TBLK), jnp.float32)] * HY_UNROLL),
        compiler_params=pltpu.CompilerParams(
            dimension_semantics=("arbitrary",), vmem_limit_bytes=VMEM_LIMIT),
        name="hyena_conv",
    )(kt, hyena_d[:, None], v_rows)


def _layer(x, pre_g, w_in, w_short, b_short, w_f1, b_f1, w_f2, b_f2, w_f3, b_f3, w_f4,
           sin_freq, hyena_d, attn_sink, w_out, post_g):
    B, L, _ = x.shape
    nblk = L // TBLK
    o_q = 4 * D_HYENA
    o_k = o_q + D_ATTN
    o_v = o_k + KV_DIM
    o_ag = o_v + KV_DIM
    w_cat = jnp.concatenate(
        [w_in[:, :o_q], w_in[:, o_q:o_k], w_in[:, o_ag:], w_in[:, o_k:o_v], w_in[:, o_v:o_ag]],
        axis=1).astype(jnp.bfloat16)

    kt = _hyena_filter(L, w_f1, b_f1, w_f2, b_f2, w_f3, b_f3, w_f4, sin_freq)
    v, x0g, za = _inproj(x, pre_g, w_cat, w_short, b_short)
    half = D_HYENA // 2
    yh = []
    for c0 in (0, half):
        v_rows = (v[:, :, c0:c0 + half].reshape(B, nblk, TBLK, half).transpose(3, 1, 0, 2)
                  .reshape(half, nblk * B, TBLK))
        y_rows = _hyena_conv(kt[c0:c0 + half], hyena_d[c0:c0 + half], v_rows)
        yh.append(y_rows.reshape(half, nblk, B, TBLK).transpose(2, 1, 3, 0).reshape(B, L, half))
    return _back(za, attn_sink, x, yh[0], yh[1], x0g, w_out.astype(jnp.bfloat16), post_g)


def kernel(x, pre_g, w_in, w_short, b_short, w_f1, b_f1, w_f2, b_f2, w_f3, b_f3, w_f4, sin_freq, hyena_d, attn_sink, w_out, post_g):
    depth = pre_g.shape[0]
    for l in range(depth):
        x = _layer(x, pre_g[l], w_in[l], w_short[l], b_short[l], w_f1[l], b_f1[l], w_f2[l],
                   b_f2[l], w_f3[l], b_f3[l], w_f4[l], sin_freq[l], hyena_d[l], attn_sink[l],
                   w_out[l], post_g[l])
    return x
```

```python
import functools
import math

import jax
import jax.numpy as jnp
import numpy as np
from jax import lax
from jax.experimental import pallas as pl
from jax.experimental.pallas import tpu as pltpu

D_MODEL = 1024
D_HYENA = 512
D_ATTN = 512
N_HEADS = 8
HEAD_DIM = 64
N_KV_HEADS = 2
Q_PER_KV = N_HEADS // N_KV_HEADS
KV_DIM = N_KV_HEADS * HEAD_DIM
WINDOW = 128
BLOCK = 128
FILTER_ORDER = 64
N_BANDS = 16
POS_EMB_DIM = 1 + 2 * N_BANDS
POS_EMB_PAD = 40
DECAY_TARGET = 1e-2
FAST_DECAY_PCT = 0.3
SLOW_DECAY_PCT = 1.5
RMS_EPS = 1e-6
NEG_INF = -1e30
PV_HEADS = 2
LOG2E = math.log2(math.e)
Q_SCALE_LOG2 = HEAD_DIM ** -0.5 * LOG2E

N_HY = 3 * D_HYENA
O_GH = N_HY
O_ATT = O_GH + D_HYENA
N_ATT = 2 * D_ATTN + 2 * KV_DIM
D_IN = O_ATT + N_ATT

LANES = 128
HALO = 16

TBLK = 256
SHIFT_ROWS = 128
HY_UNROLL = 4
HY_PARTS = 4
HY_PART_CH = D_HYENA // HY_PARTS

VMEM_LIMIT = 56 * 1024 * 1024

_HI = lax.Precision.HIGHEST


def _filter_consts(L):
    t = np.arange(L, dtype=np.float32)
    t_norm = t / np.float32(max(L - 1, 1))
    w = np.float32(2.0 * math.pi) * t / np.float32(L)
    bands = np.linspace(1e-4, N_BANDS - 1, N_BANDS).astype(np.float32)
    ang = w[:, None] * bands[None, :]
    z = np.concatenate([t_norm[:, None], np.cos(ang), -np.sin(ang)], axis=-1)
    min_decay = math.log(DECAY_TARGET) / SLOW_DECAY_PCT
    max_decay = math.log(DECAY_TARGET) / FAST_DECAY_PCT
    deltas = np.abs(np.linspace(min_decay, max_decay, D_HYENA)).astype(np.float32)
    decay = np.exp(-t_norm[:, None] * deltas[None, :]).astype(np.float32)
    pad = np.zeros((POS_EMB_PAD - POS_EMB_DIM, L), np.float32)
    zt = np.concatenate([z.T, pad], axis=0)
    flip = np.eye(LANES, dtype=np.float32)[::-1]
    return zt, np.ascontiguousarray(decay.T), np.ascontiguousarray(flip)


def _filter_kernel(zt_ref, dec_ref, flip_ref, w1_ref, b1_ref, w2_ref, b2_ref,
                   w3_ref, b3_ref, w4_ref, fr_ref, k_ref):
    L = zt_ref.shape[1]
    f32 = jnp.float32
    h = jnp.dot(w1_ref[...], zt_ref[...], precision=_HI, preferred_element_type=f32)
    h = jnp.sin(fr_ref[:, 0:1] * (h + b1_ref[...]))
    h = jnp.dot(w2_ref[...], h, precision=_HI, preferred_element_type=f32)
    h = jnp.sin(fr_ref[:, 1:2] * (h + b2_ref[...]))
    h = jnp.dot(w3_ref[...], h, precision=_HI, preferred_element_type=f32)
    h = jnp.sin(fr_ref[:, 2:3] * (h + b3_ref[...]))
    hf = jnp.dot(w4_ref[0:D_HYENA, :], h, precision=_HI, preferred_element_type=f32) * dec_ref[...]
    hb = jnp.dot(w4_ref[D_HYENA:, :], h, precision=_HI, preferred_element_type=f32) * dec_ref[...]
    nt = L // LANES
    rev = jnp.concatenate(
        [jnp.dot(hb[:, (nt - 1 - j) * LANES:(nt - j) * LANES], flip_ref[...], precision=_HI,
                 preferred_element_type=f32) for j in range(nt)], axis=1)
    col = lax.broadcasted_iota(jnp.int32, rev.shape, 1)
    hb2 = jnp.where(col == 0, 0.0, pltpu.roll(rev, 1, axis=1))
    ss = jnp.sum(hf * hf, axis=1, keepdims=True) + jnp.sum(hb2 * hb2, axis=1, keepdims=True)
    inv = lax.rsqrt(ss + 1e-12)
    k_ref[:, 0:L] = hf * inv
    k_ref[:, L:2 * L] = hb2 * inv


def _hyena_filter(L, w_f1, b_f1, w_f2, b_f2, w_f3, b_f3, w_f4, sin_freq):
    zt, dec, flip = _filter_consts(L)
    w1t = jnp.pad(w_f1.T, ((0, 0), (0, POS_EMB_PAD - POS_EMB_DIM)))
    args = (jnp.asarray(zt), jnp.asarray(dec), jnp.asarray(flip),
            w1t, b_f1[:, None], w_f2.T, b_f2[:, None], w_f3.T, b_f3[:, None],
            w_f4.T, sin_freq.T)
    return pl.pallas_call(
        _filter_kernel,
        out_shape=jax.ShapeDtypeStruct((D_HYENA, 2 * L), jnp.float32),
        compiler_params=pltpu.CompilerParams(vmem_limit_bytes=VMEM_LIMIT),
        name="hyena_filter",
    )(*args)


def _inproj_kernel(xp_ref, x_ref, xn_ref, g_ref, w_ref, wsh_ref, bsh_ref, *out_refs):
    v_refs = out_refs[:HY_PARTS]
    x0g_ref, za_ref = out_refs[HY_PARTS:]
    t = pl.program_id(1)
    nt = pl.num_programs(1)
    tm = x_ref.shape[1]

    def norm(xv):
        ms = jnp.mean(xv * xv, axis=-1, keepdims=True)
        return (xv * lax.rsqrt(ms + RMS_EPS) * g_ref[...]).astype(jnp.bfloat16)

    h = norm(x_ref[0])
    hp = jnp.where(t > 0, norm(xp_ref[0]), 0.0).astype(jnp.bfloat16)
    hn = jnp.where(t < nt - 1, norm(xn_ref[0]), 0.0).astype(jnp.bfloat16)
    hall = jnp.concatenate([hp, h, hn], axis=0)
    zh = jnp.dot(hall, w_ref[:, 0:N_HY], preferred_element_type=jnp.float32)
    z = jnp.dot(h, w_ref[:, N_HY:], preferred_element_type=jnp.float32)

    rows = tm + 2 * HALO
    u_prev = pltpu.roll(zh, 1, axis=0)[HALO:HALO + tm]
    u = zh[HALO:HALO + tm]
    u_next = pltpu.roll(zh, rows - 1, axis=0)[HALO:HALO + tm]
    uc = (u_prev * wsh_ref[0:1, :] + u * wsh_ref[1:2, :] + u_next * wsh_ref[2:3, :]
          + bsh_ref[...])
    x0 = uc[:, 0:D_HYENA]
    x1 = uc[:, D_HYENA:2 * D_HYENA]
    vv = uc[:, 2 * D_HYENA:]
    gh = z[:, 0:D_HYENA]
    vx = (vv * x1).astype(v_refs[0].dtype)
    for p, v_ref in enumerate(v_refs):
        v_ref[0] = vx[:, p * HY_PART_CH:(p + 1) * HY_PART_CH]
    x0g_ref[0] = (x0 * (gh * jax.nn.sigmoid(gh))).astype(x0g_ref.dtype)
    o_q = O_ATT - N_HY
    za_ref[0, :, 0:D_ATTN] = (z[:, o_q:o_q + D_ATTN] * Q_SCALE_LOG2).astype(za_ref.dtype)
    ga = z[:, o_q + D_ATTN:o_q + 2 * D_ATTN]
    za_ref[0, :, D_ATTN:2 * D_ATTN] = (ga * jax.nn.sigmoid(ga)).astype(za_ref.dtype)
    za_ref[0, :, 2 * D_ATTN:] = z[:, o_q + 2 * D_ATTN:].astype(za_ref.dtype)


def _inproj(x, pre_g, w_cat, w_short, b_short, tm=1024):
    B, L, D = x.shape
    rb = tm // HALO
    nrb = L // HALO
    grid = (B, L // tm)
    return pl.pallas_call(
        _inproj_kernel,
        grid=grid,
        in_specs=[
            pl.BlockSpec((1, HALO, D), lambda b, t: (b, jnp.maximum(t * rb - 1, 0), 0)),
            pl.BlockSpec((1, tm, D), lambda b, t: (b, t, 0)),
            pl.BlockSpec((1, HALO, D), lambda b, t: (b, jnp.minimum((t + 1) * rb, nrb - 1), 0)),
            pl.BlockSpec((1, D), lambda b, t: (0, 0)),
            pl.BlockSpec((D, D_IN), lambda b, t: (0, 0)),
            pl.BlockSpec((3, N_HY), lambda b, t: (0, 0)),
            pl.BlockSpec((1, N_HY), lambda b, t: (0, 0)),
        ],
        out_specs=(
            [pl.BlockSpec((1, tm, HY_PART_CH), lambda b, t: (b, t, 0))] * HY_PARTS
            + [pl.BlockSpec((1, tm, D_HYENA), lambda b, t: (b, t, 0)),
               pl.BlockSpec((1, tm, N_ATT), lambda b, t: (b, t, 0))]),
        out_shape=(
            [jax.ShapeDtypeStruct((B, L, HY_PART_CH), jnp.bfloat16)] * HY_PARTS
            + [jax.ShapeDtypeStruct((B, L, D_HYENA), jnp.bfloat16),
               jax.ShapeDtypeStruct((B, L, N_ATT), jnp.bfloat16)]),
        compiler_params=pltpu.CompilerParams(
            dimension_semantics=("parallel", "arbitrary"), vmem_limit_bytes=VMEM_LIMIT),
        name="inproj",
    )(x, x, x, pre_g[None, :], w_cat, w_short, b_short[None, :])


def _alibi_slopes():
    return [float(v) for v in
            np.exp2(-8.0 * np.arange(1, N_HEADS + 1, dtype=np.float32) / N_HEADS).astype(np.float32)]


def _attn_bias_tables():
    c = np.arange(BLOCK)[:, None]
    r = np.arange(BLOCK)[None, :]
    dist = np.abs(c - r).astype(np.float32)
    slopes = np.asarray(_alibi_slopes(), np.float32)[:, None, None] * np.float32(LOG2E)
    return np.concatenate([-slopes * dist, -slopes * (WINDOW - dist)], axis=1).astype(np.float32)


def _back_kernel(sink_ref, bias_ref, za_ref, x_ref, *refs):
    yh_refs = refs[:HY_PARTS]
    (x0g_ref, w_ref, g_ref, o_ref,
     kpl_ref, vt_ref, s_ref, ot_ref, mask_ref, ya_ref) = refs[HY_PARTS:]
    t = pl.program_id(1)
    tm = x_ref.shape[1]
    L = za_ref.shape[1]
    nb = L // BLOCK
    nbs = tm // BLOCK
    i0 = t * nbs
    o_g = D_ATTN
    o_k = 2 * D_ATTN
    o_v = o_k + KV_DIM
    f32 = jnp.float32
    bf16 = jnp.bfloat16
    slopes = _alibi_slopes()
    group = Q_PER_KV * HEAD_DIM

    c_i = lax.broadcasted_iota(jnp.int32, (BLOCK, BLOCK), 0)
    r_i = lax.broadcasted_iota(jnp.int32, (BLOCK, BLOCK), 1)
    upper = c_i >= r_i
    diag = c_i == r_i
    ones_rows = jnp.ones((16, 3 * BLOCK), bf16)

    @pl.when(t == 0)
    def _():
        lane = lax.broadcasted_iota(jnp.int32, (BLOCK, KV_DIM), 1)
        lo = lane < HEAD_DIM
        ztile = jnp.zeros((BLOCK, KV_DIM), bf16)
        for idx in range(2 * N_KV_HEADS):
            kpl_ref[idx, 0] = ztile
            kpl_ref[idx, nb + 1] = ztile
        vt_ref[0] = ztile
        vt_ref[nb + 1] = ztile
        mask_ref[...] = jnp.where(upper, 1.0, 0.0).astype(bf16)

        def prep(j, carry):
            start = pl.multiple_of(j * BLOCK, BLOCK)
            kc = za_ref[0, pl.ds(start, BLOCK), o_k:o_k + KV_DIM].astype(f32)
            ksw = pltpu.roll(kc, HEAD_DIM, axis=1)
            kpl_ref[0, j + 1] = jnp.where(lo, kc, 0.0).astype(bf16)
            kpl_ref[1, j + 1] = jnp.where(lo, 0.0, ksw).astype(bf16)
            kpl_ref[2, j + 1] = jnp.where(lo, ksw, 0.0).astype(bf16)
            kpl_ref[3, j + 1] = jnp.where(lo, 0.0, kc).astype(bf16)
            vc = za_ref[0, pl.ds(start, BLOCK), o_v:o_v + KV_DIM].astype(f32)
            vt_ref[j + 1] = vc.T.astype(bf16)
            return carry

        lax.fori_loop(0, nb, prep, 0, unroll=4)

    def scores(i, slot):
        start = pl.multiple_of(i * BLOCK, BLOCK)
        qblk = za_ref[0, pl.ds(start, BLOCK), 0:D_ATTN]
        for kv in range(N_KV_HEADS):
            base = kv * group
            qs = jnp.concatenate([qblk[:, base:base + 2 * HEAD_DIM],
                                  qblk[:, base + 2 * HEAD_DIM:base + group]], axis=0)
            kp = jnp.concatenate([kpl_ref[2 * kv + half, i + tt]
                                  for half in range(2) for tt in range(3)], axis=0)
            s_ref[slot, kv] = lax.dot_general(kp, qs, (((1,), (1,)), ((), ())),
                                              preferred_element_type=f32)

    def softmax_pv(i, slot, edge):
        pen1 = jnp.where(i == 0, NEG_INF, 0.0).astype(f32)
        pen3 = jnp.where(i == nb - 1, NEG_INF, 0.0).astype(f32)
        for kv in range(N_KV_HEADS):
            rows = slice(kv * HEAD_DIM, (kv + 1) * HEAD_DIM)
            vt3 = vt_ref[i + 2, rows, :]
            vtw = jnp.concatenate([vt_ref[i, rows, :], vt_ref[i + 1, rows, :], vt3], axis=1)
            pts, extras, pxs = [], [], []
            for g in range(Q_PER_KV):
                half, pair = g % 2, g // 2
                h = kv * Q_PER_KV + g
                r0 = half * 3 * BLOCK
                cols = slice(pair * BLOCK, (pair + 1) * BLOCK)
                s1 = s_ref[slot, kv, r0:r0 + BLOCK, cols]
                s2 = s_ref[slot, kv, r0 + BLOCK:r0 + 2 * BLOCK, cols]
                s3 = s_ref[slot, kv, r0 + 2 * BLOCK:r0 + 3 * BLOCK, cols]
                if edge == "lo":
                    mrg = jnp.where(upper, s1 + pen1, s3)
                elif edge == "hi":
                    mrg = jnp.where(upper, s1, s3 + pen3)
                else:
                    mrg = jnp.where(upper, s1, s3)
                a_mid = s2 + bias_ref[h, 0:BLOCK, :]
                a_mrg = mrg + bias_ref[h, BLOCK:, :]
                sink2 = sink_ref[h] * LOG2E
                s_x = (jnp.sum(jnp.where(diag, s3, 0.0), axis=0, keepdims=True)
                       - slopes[h] * LOG2E * WINDOW)
                if edge == "hi":
                    s_x = s_x + pen3
                m = jnp.max(jnp.maximum(a_mid, a_mrg), axis=0, keepdims=True)
                m = jnp.maximum(jnp.maximum(m, sink2), s_x)
                p_mid = jnp.exp2(a_mid - m).astype(bf16)
                p_mrg = jnp.exp2(a_mrg - m).astype(bf16)
                p_x = jnp.exp2(s_x - m)
                pxs.append(p_x)
                extras.append(jnp.exp2(sink2 - m) + p_x)
                p1 = p_mrg * mask_ref[...]
                p3 = p_mrg - p1
                pts.append(jnp.concatenate([p1, p_mid, p3], axis=0))
                if len(pts) < PV_HEADS:
                    continue
                pt = jnp.concatenate(pts, axis=1)
                ot = jnp.dot(jnp.concatenate([vtw, ones_rows], axis=0), pt,
                             preferred_element_type=f32)
                for n in range(PV_HEADS):
                    hh = h - (PV_HEADS - 1) + n
                    cols = slice(n * BLOCK, (n + 1) * BLOCK)
                    o = ot[0:HEAD_DIM, cols] + vt3.astype(f32) * pxs[n]
                    inv = 1.0 / (ot[HEAD_DIM:HEAD_DIM + 1, cols] + extras[n])
                    ot_ref[slot, hh * HEAD_DIM:(hh + 1) * HEAD_DIM, :] = o * inv
                pts, extras, pxs = [], [], []

    def finish(i, slot):
        start = pl.multiple_of(i * BLOCK, BLOCK)
        local = pl.multiple_of((i - i0) * BLOCK, BLOCK)
        gate = za_ref[0, pl.ds(start, BLOCK), o_g:o_g + D_ATTN].astype(f32)
        ya_ref[pl.ds(local, BLOCK), :] = (ot_ref[slot].T * gate).astype(ya_ref.dtype)

    def two_blocks(j, carry):
        i = i0 + 2 * j + 1
        scores(i + 1, 0)
        softmax_pv(i, 1, None)
        finish(i - 1, 0)
        scores(i + 2, 1)
        softmax_pv(i + 1, 0, None)
        finish(i, 1)
        return carry

    scores(i0, 0)
    scores(i0 + 1, 1)
    softmax_pv(i0, 0, "lo")
    lax.fori_loop(0, (nbs - 2) // 2, two_blocks, 0)
    softmax_pv(i0 + nbs - 1, 1, "hi")
    finish(i0 + nbs - 2, 0)
    finish(i0 + nbs - 1, 1)

    x0g = x0g_ref[0].astype(f32)
    yh = jnp.concatenate([r[0] for r in yh_refs], axis=1).astype(f32)
    y = jnp.dot(jnp.concatenate([(yh * x0g).astype(bf16), ya_ref[...]], axis=1), w_ref[...],
                preferred_element_type=f32)
    ms = jnp.mean(y * y, axis=-1, keepdims=True)
    o_ref[0] = x_ref[0] + y * lax.rsqrt(ms + RMS_EPS) * g_ref[...]


def _back(za, sink, x, yh_parts, x0g, w_out, post_g, tm=1024):
    B, L, D = x.shape
    nb = L // BLOCK
    nbs = tm // BLOCK
    assert WINDOW == BLOCK and nbs >= 4 and nbs % 2 == 0 and L % tm == 0
    nkt = nb + 2
    bias = jnp.asarray(_attn_bias_tables())
    return pl.pallas_call(
        _back_kernel,
        grid=(B, L // tm),
        in_specs=[
            pl.BlockSpec(memory_space=pltpu.SMEM),
            pl.BlockSpec((N_HEADS, 2 * BLOCK, BLOCK), lambda b, t: (0, 0, 0)),
            pl.BlockSpec((1, L, N_ATT), lambda b, t: (b, 0, 0)),
            pl.BlockSpec((1, tm, D), lambda b, t: (b, t, 0)),
        ] + [pl.BlockSpec((1, tm, HY_PART_CH), lambda b, t: (b, t, 0))] * HY_PARTS + [
            pl.BlockSpec((1, tm, D_HYENA), lambda b, t: (b, t, 0)),
            pl.BlockSpec((D_HYENA + D_ATTN, D), lambda b, t: (0, 0)),
            pl.BlockSpec((1, D), lambda b, t: (0, 0)),
        ],
        out_specs=pl.BlockSpec((1, tm, D), lambda b, t: (b, t, 0)),
        out_shape=jax.ShapeDtypeStruct((B, L, D), jnp.float32),
        scratch_shapes=[
            pltpu.VMEM((2 * N_KV_HEADS, nkt, BLOCK, KV_DIM), jnp.bfloat16),
            pltpu.VMEM((nkt, KV_DIM, BLOCK), jnp.bfloat16),
            pltpu.VMEM((2, N_KV_HEADS, 6 * BLOCK, 2 * BLOCK), jnp.float32),
            pltpu.VMEM((2, D_ATTN, BLOCK), jnp.float32),
            pltpu.VMEM((BLOCK, BLOCK), jnp.bfloat16),
            pltpu.VMEM((tm, D_ATTN), jnp.bfloat16),
        ],
        compiler_params=pltpu.CompilerParams(
            dimension_semantics=("parallel", "arbitrary"), vmem_limit_bytes=VMEM_LIMIT),
        name="attn_outproj",
    )(sink, bias, za, x, *yh_parts, x0g, w_out, post_g[None, :])


def _hyena_kernel(k_ref, d_ref, v_ref, y_ref, *scratch):
    tables = scratch[:HY_UNROLL]
    accs = scratch[HY_UNROLL:]
    step = pl.program_id(0)
    cg = v_ref.shape[0]
    rows = v_ref.shape[1]
    nchan, n2 = k_ref.shape
    nblk = n2 // (2 * TBLK)
    bsz = rows // nblk
    off = TBLK * (nblk - 1) + SHIFT_ROWS

    def build_table(chan, s_ref):
        krow = k_ref[pl.ds(jnp.minimum(chan, nchan - 1), 1), :]
        kb = jnp.broadcast_to(krow, (SHIFT_ROWS, n2))
        s_ref[...] = pltpu.roll(kb, off, axis=1, stride=1, stride_axis=0).astype(s_ref.dtype)

    def convolve(ci, s_ref, acc_ref):
        acc_ref[...] = d_ref[pl.ds(ci, 1), :] * v_ref[ci].astype(jnp.float32)
        for d in range(-(nblk - 1), nblk):
            x0 = TBLK * (d + nblk - 1)
            w = jnp.concatenate(
                [s_ref[:, x0 + SHIFT_ROWS:x0 + SHIFT_ROWS + TBLK], s_ref[:, x0:x0 + TBLK]], axis=0)
            n = (nblk - abs(d)) * bsz
            src = 0 if d >= 0 else -d * bsz
            dst = d * bsz if d >= 0 else 0
            acc_ref[dst:dst + n, :] += jnp.dot(v_ref[ci, pl.ds(src, n), :], w,
                                               preferred_element_type=jnp.float32)
        y_ref[ci] = acc_ref[...].astype(y_ref.dtype)

    @pl.when(step == 0)
    def _():
        for u in range(HY_UNROLL):
            build_table(u, tables[u])

    def body(it, carry):
        for u in range(HY_UNROLL):
            convolve(it * HY_UNROLL + u, tables[u], accs[u])
        for u in range(HY_UNROLL):
            build_table(step * cg + (it + 1) * HY_UNROLL + u, tables[u])
        return carry

    lax.fori_loop(0, cg // HY_UNROLL, body, 0)


def _hyena_conv(kt, hyena_d, v_rows, part, cg=16):
    C, rows, _ = v_rows.shape
    n2 = kt.shape[1]
    steps = C // cg
    return pl.pallas_call(
        _hyena_kernel,
        grid=(steps,),
        in_specs=[
            pl.BlockSpec((C, n2), lambda c: (part, 0)),
            pl.BlockSpec((cg, 1), lambda c: (c + part * steps, 0)),
            pl.BlockSpec((cg, rows, TBLK), lambda c: (c, 0, 0)),
        ],
        out_specs=pl.BlockSpec((cg, rows, TBLK), lambda c: (c, 0, 0)),
        out_shape=jax.ShapeDtypeStruct((C, rows, TBLK), jnp.bfloat16),
        scratch_shapes=(
            [pltpu.VMEM((SHIFT_ROWS, n2), jnp.bfloat16)] * HY_UNROLL
            + [pltpu.VMEM((rows, TBLK), jnp.float32)] * HY_UNROLL),
        compiler_params=pltpu.CompilerParams(
            dimension_semantics=("arbitrary",), vmem_limit_bytes=VMEM_LIMIT),
        name="hyena_conv",
    )(kt, hyena_d[:, None], v_rows)


def _layer(x, pre_g, w_in, w_short, b_short, w_f1, b_f1, w_f2, b_f2, w_f3, b_f3, w_f4,
           sin_freq, hyena_d, attn_sink, w_out, post_g):
    B, L, _ = x.shape
    nblk = L // TBLK
    o_q = 4 * D_HYENA
    o_k = o_q + D_ATTN
    o_v = o_k + KV_DIM
    o_ag = o_v + KV_DIM
    w_cat = jnp.concatenate(
        [w_in[:, :o_q], w_in[:, o_q:o_k], w_in[:, o_ag:], w_in[:, o_k:o_v], w_in[:, o_v:o_ag]],
        axis=1).astype(jnp.bfloat16)

    kt = _hyena_filter(L, w_f1, b_f1, w_f2, b_f2, w_f3, b_f3, w_f4, sin_freq)
    *v_parts, x0g, za = _inproj(x, pre_g, w_cat, w_short, b_short)
    pc = HY_PART_CH
    yh = []
    for part, v in enumerate(v_parts):
        v_rows = v.reshape(B, nblk, TBLK, pc).transpose(3, 1, 0, 2).reshape(pc, nblk * B, TBLK)
        y_rows = _hyena_conv(kt, hyena_d, v_rows, part)
        yh.append(y_rows.reshape(pc, nblk, B, TBLK).transpose(2, 1, 3, 0).reshape(B, L, pc))
    return _back(za, attn_sink, x, yh, x0g, w_out.astype(jnp.bfloat16), post_g)


def kernel(x, pre_g, w_in, w_short, b_short, w_f1, b_f1, w_f2, b_f2, w_f3, b_f3, w_f4, sin_freq, hyena_d, attn_sink, w_out, post_g):
    depth = pre_g.shape[0]
    for l in range(depth):
        x = _layer(x, pre_g[l], w_in[l], w_short[l], b_short[l], w_f1[l], b_f1[l], w_f2[l],
                   b_f2[l], w_f3[l], b_f3[l], w_f4[l], sin_freq[l], hyena_d[l], attn_sink[l],
                   w_out[l], post_g[l])
    return x
```

```python
import functools
import math

import jax
import jax.numpy as jnp
import numpy as np
from jax import lax
from jax.experimental import pallas as pl
from jax.experimental.pallas import tpu as pltpu

D_MODEL = 1024
D_HYENA = 512
D_ATTN = 512
N_HEADS = 8
HEAD_DIM = 64
N_KV_HEADS = 2
Q_PER_KV = N_HEADS // N_KV_HEADS
KV_DIM = N_KV_HEADS * HEAD_DIM
WINDOW = 128
BLOCK = 128
FILTER_ORDER = 64
N_BANDS = 16
POS_EMB_DIM = 1 + 2 * N_BANDS
POS_EMB_PAD = 40
DECAY_TARGET = 1e-2
FAST_DECAY_PCT = 0.3
SLOW_DECAY_PCT = 1.5
RMS_EPS = 1e-6
NEG_INF = -1e30
PV_HEADS = 2
LOG2E = math.log2(math.e)
Q_SCALE_LOG2 = HEAD_DIM ** -0.5 * LOG2E

N_HY = 3 * D_HYENA
O_GH = N_HY
O_ATT = O_GH + D_HYENA
N_ATT = 2 * D_ATTN + 2 * KV_DIM
D_IN = O_ATT + N_ATT

LANES = 128
HALO = 16

TBLK = 256
SHIFT_ROWS = 128
HY_UNROLL = 4
HY_PARTS = 4
HY_PART_CH = D_HYENA // HY_PARTS

VMEM_LIMIT = 56 * 1024 * 1024

_HI = lax.Precision.HIGHEST


def _filter_consts(L):
    t = np.arange(L, dtype=np.float32)
    t_norm = t / np.float32(max(L - 1, 1))
    w = np.float32(2.0 * math.pi) * t / np.float32(L)
    bands = np.linspace(1e-4, N_BANDS - 1, N_BANDS).astype(np.float32)
    ang = w[:, None] * bands[None, :]
    z = np.concatenate([t_norm[:, None], np.cos(ang), -np.sin(ang)], axis=-1)
    min_decay = math.log(DECAY_TARGET) / SLOW_DECAY_PCT
    max_decay = math.log(DECAY_TARGET) / FAST_DECAY_PCT
    deltas = np.abs(np.linspace(min_decay, max_decay, D_HYENA)).astype(np.float32)
    decay = np.exp(-t_norm[:, None] * deltas[None, :]).astype(np.float32)
    pad = np.zeros((POS_EMB_PAD - POS_EMB_DIM, L), np.float32)
    zt = np.concatenate([z.T, pad], axis=0)
    flip = np.eye(LANES, dtype=np.float32)[::-1]
    return zt, np.ascontiguousarray(decay.T), np.ascontiguousarray(flip)


def _filter_kernel(zt_ref, dec_ref, flip_ref, w1_ref, b1_ref, w2_ref, b2_ref,
                   w3_ref, b3_ref, w4_ref, fr_ref, k_ref):
    L = zt_ref.shape[1]
    f32 = jnp.float32
    h = jnp.dot(w1_ref[...], zt_ref[...], precision=_HI, preferred_element_type=f32)
    h = jnp.sin(fr_ref[:, 0:1] * (h + b1_ref[...]))
    h = jnp.dot(w2_ref[...], h, precision=_HI, preferred_element_type=f32)
    h = jnp.sin(fr_ref[:, 1:2] * (h + b2_ref[...]))
    h = jnp.dot(w3_ref[...], h, precision=_HI, preferred_element_type=f32)
    h = jnp.sin(fr_ref[:, 2:3] * (h + b3_ref[...]))
    hf = jnp.dot(w4_ref[0:D_HYENA, :], h, precision=_HI, preferred_element_type=f32) * dec_ref[...]
    hb = jnp.dot(w4_ref[D_HYENA:, :], h, precision=_HI, preferred_element_type=f32) * dec_ref[...]
    nt = L // LANES
    rev = jnp.concatenate(
        [jnp.dot(hb[:, (nt - 1 - j) * LANES:(nt - j) * LANES], flip_ref[...], precision=_HI,
                 preferred_element_type=f32) for j in range(nt)], axis=1)
    col = lax.broadcasted_iota(jnp.int32, rev.shape, 1)
    hb2 = jnp.where(col == 0, 0.0, pltpu.roll(rev, 1, axis=1))
    ss = jnp.sum(hf * hf, axis=1, keepdims=True) + jnp.sum(hb2 * hb2, axis=1, keepdims=True)
    inv = lax.rsqrt(ss + 1e-12)
    k_ref[:, 0:L] = hf * inv
    k_ref[:, L:2 * L] = hb2 * inv


def _hyena_filter(L, w_f1, b_f1, w_f2, b_f2, w_f3, b_f3, w_f4, sin_freq):
    zt, dec, flip = _filter_consts(L)
    w1t = jnp.pad(w_f1.T, ((0, 0), (0, POS_EMB_PAD - POS_EMB_DIM)))
    args = (jnp.asarray(zt), jnp.asarray(dec), jnp.asarray(flip),
            w1t, b_f1[:, None], w_f2.T, b_f2[:, None], w_f3.T, b_f3[:, None],
            w_f4.T, sin_freq.T)
    return pl.pallas_call(
        _filter_kernel,
        out_shape=jax.ShapeDtypeStruct((D_HYENA, 2 * L), jnp.float32),
        compiler_params=pltpu.CompilerParams(vmem_limit_bytes=VMEM_LIMIT),
        name="hyena_filter",
    )(*args)


def _inproj_kernel(xp_ref, x_ref, xn_ref, g_ref, w_ref, wsh_ref, bsh_ref, *out_refs):
    v_refs = out_refs[:HY_PARTS]
    x0g_ref, za_ref = out_refs[HY_PARTS:]
    t = pl.program_id(1)
    nt = pl.num_programs(1)
    tm = x_ref.shape[1]

    def norm(xv):
        ms = jnp.mean(xv * xv, axis=-1, keepdims=True)
        return (xv * lax.rsqrt(ms + RMS_EPS) * g_ref[...]).astype(jnp.bfloat16)

    h = norm(x_ref[0])
    hp = jnp.where(t > 0, norm(xp_ref[0]), 0.0).astype(jnp.bfloat16)
    hn = jnp.where(t < nt - 1, norm(xn_ref[0]), 0.0).astype(jnp.bfloat16)
    hall = jnp.concatenate([hp, h, hn], axis=0)
    zh = jnp.dot(hall, w_ref[:, 0:N_HY], preferred_element_type=jnp.float32)
    z = jnp.dot(h, w_ref[:, N_HY:], preferred_element_type=jnp.float32)

    rows = tm + 2 * HALO
    u_prev = pltpu.roll(zh, 1, axis=0)[HALO:HALO + tm]
    u = zh[HALO:HALO + tm]
    u_next = pltpu.roll(zh, rows - 1, axis=0)[HALO:HALO + tm]
    uc = (u_prev * wsh_ref[0:1, :] + u * wsh_ref[1:2, :] + u_next * wsh_ref[2:3, :]
          + bsh_ref[...])
    x0 = uc[:, 0:D_HYENA]
    x1 = uc[:, D_HYENA:2 * D_HYENA]
    vv = uc[:, 2 * D_HYENA:]
    gh = z[:, 0:D_HYENA]
    vx = (vv * x1).astype(v_refs[0].dtype)
    for p, v_ref in enumerate(v_refs):
        v_ref[0] = vx[:, p * HY_PART_CH:(p + 1) * HY_PART_CH]
    x0g_ref[0] = (x0 * (gh * jax.nn.sigmoid(gh))).astype(x0g_ref.dtype)
    o_q = O_ATT - N_HY
    za_ref[0, :, 0:D_ATTN] = (z[:, o_q:o_q + D_ATTN] * Q_SCALE_LOG2).astype(za_ref.dtype)
    ga = z[:, o_q + D_ATTN:o_q + 2 * D_ATTN]
    za_ref[0, :, D_ATTN:2 * D_ATTN] = (ga * jax.nn.sigmoid(ga)).astype(za_ref.dtype)
    za_ref[0, :, 2 * D_ATTN:] = z[:, o_q + 2 * D_ATTN:].astype(za_ref.dtype)


def _inproj(x, pre_g, w_cat, w_short, b_short, tm=1024):
    B, L, D = x.shape
    rb = tm // HALO
    nrb = L // HALO
    grid = (B, L // tm)
    return pl.pallas_call(
        _inproj_kernel,
        grid=grid,
        in_specs=[
            pl.BlockSpec((1, HALO, D), lambda b, t: (b, jnp.maximum(t * rb - 1, 0), 0)),
            pl.BlockSpec((1, tm, D), lambda b, t: (b, t, 0)),
            pl.BlockSpec((1, HALO, D), lambda b, t: (b, jnp.minimum((t + 1) * rb, nrb - 1), 0)),
            pl.BlockSpec((1, D), lambda b, t: (0, 0)),
            pl.BlockSpec((D, D_IN), lambda b, t: (0, 0)),
            pl.BlockSpec((3, N_HY), lambda b, t: (0, 0)),
            pl.BlockSpec((1, N_HY), lambda b, t: (0, 0)),
        ],
        out_specs=(
            [pl.BlockSpec((1, tm, HY_PART_CH), lambda b, t: (b, t, 0))] * HY_PARTS
            + [pl.BlockSpec((1, tm, D_HYENA), lambda b, t: (b, t, 0)),
               pl.BlockSpec((1, tm, N_ATT), lambda b, t: (b, t, 0))]),
        out_shape=(
            [jax.ShapeDtypeStruct((B, L, HY_PART_CH), jnp.bfloat16)] * HY_PARTS
            + [jax.ShapeDtypeStruct((B, L, D_HYENA), jnp.bfloat16),
               jax.ShapeDtypeStruct((B, L, N_ATT), jnp.bfloat16)]),
        compiler_params=pltpu.CompilerParams(
            dimension_semantics=("parallel", "arbitrary"), vmem_limit_bytes=VMEM_LIMIT),
        name="inproj",
    )(x, x, x, pre_g[None, :], w_cat, w_short, b_short[None, :])


def _alibi_slopes():
    return [float(v) for v in
            np.exp2(-8.0 * np.arange(1, N_HEADS + 1, dtype=np.float32) / N_HEADS).astype(np.float32)]


def _attn_bias_tables():
    c = np.arange(BLOCK)[:, None]
    r = np.arange(BLOCK)[None, :]
    dist = np.abs(c - r).astype(np.float32)
    slopes = np.asarray(_alibi_slopes(), np.float32)[:, None, None] * np.float32(LOG2E)
    return np.concatenate([-slopes * dist, -slopes * (WINDOW - dist)], axis=1).astype(np.float32)


def _back_kernel(sink_ref, bias_ref, za_ref, x_ref, *refs):
    yh_refs = refs[:HY_PARTS]
    (x0g_ref, w_ref, g_ref, o_ref,
     kpl_ref, vt_ref, s_ref, ot_ref, mask_ref, ya_ref) = refs[HY_PARTS:]
    t = pl.program_id(1)
    tm = x_ref.shape[1]
    L = za_ref.shape[1]
    nb = L // BLOCK
    nbs = tm // BLOCK
    i0 = t * nbs
    o_g = D_ATTN
    o_k = 2 * D_ATTN
    o_v = o_k + KV_DIM
    f32 = jnp.float32
    bf16 = jnp.bfloat16
    slopes = _alibi_slopes()
    group = Q_PER_KV * HEAD_DIM

    c_i = lax.broadcasted_iota(jnp.int32, (BLOCK, BLOCK), 0)
    r_i = lax.broadcasted_iota(jnp.int32, (BLOCK, BLOCK), 1)
    upper = c_i >= r_i
    diag = c_i == r_i
    ones_rows = jnp.ones((16, 3 * BLOCK), bf16)

    @pl.when(t == 0)
    def _():
        lane = lax.broadcasted_iota(jnp.int32, (BLOCK, KV_DIM), 1)
        lo = lane < HEAD_DIM
        ztile = jnp.zeros((BLOCK, KV_DIM), bf16)
        for idx in range(2 * N_KV_HEADS):
            kpl_ref[idx, 0] = ztile
            kpl_ref[idx, nb + 1] = ztile
        vt_ref[0] = ztile
        vt_ref[nb + 1] = ztile
        mask_ref[...] = jnp.where(upper, 1.0, 0.0).astype(bf16)

        def prep(j, carry):
            start = pl.multiple_of(j * BLOCK, BLOCK)
            kc = za_ref[0, pl.ds(start, BLOCK), o_k:o_k + KV_DIM].astype(f32)
            ksw = pltpu.roll(kc, HEAD_DIM, axis=1)
            kpl_ref[0, j + 1] = jnp.where(lo, kc, 0.0).astype(bf16)
            kpl_ref[1, j + 1] = jnp.where(lo, 0.0, ksw).astype(bf16)
            kpl_ref[2, j + 1] = jnp.where(lo, ksw, 0.0).astype(bf16)
            kpl_ref[3, j + 1] = jnp.where(lo, 0.0, kc).astype(bf16)
            vc = za_ref[0, pl.ds(start, BLOCK), o_v:o_v + KV_DIM].astype(f32)
            vt_ref[j + 1] = vc.T.astype(bf16)
            return carry

        lax.fori_loop(0, nb, prep, 0, unroll=4)

    def scores(i, slot):
        start = pl.multiple_of(i * BLOCK, BLOCK)
        qblk = za_ref[0, pl.ds(start, BLOCK), 0:D_ATTN]
        for kv in range(N_KV_HEADS):
            base = kv * group
            qs = jnp.concatenate([qblk[:, base:base + 2 * HEAD_DIM],
                                  qblk[:, base + 2 * HEAD_DIM:base + group]], axis=0)
            kp = jnp.concatenate([kpl_ref[2 * kv + half, i + tt]
                                  for half in range(2) for tt in range(3)], axis=0)
            s_ref[slot, kv] = lax.dot_general(kp, qs, (((1,), (1,)), ((), ())),
                                              preferred_element_type=f32)

    def softmax_pv(i, slot, edge):
        pen1 = jnp.where(i == 0, NEG_INF, 0.0).astype(f32)
        pen3 = jnp.where(i == nb - 1, NEG_INF, 0.0).astype(f32)
        for kv in range(N_KV_HEADS):
            rows = slice(kv * HEAD_DIM, (kv + 1) * HEAD_DIM)
            vt3 = vt_ref[i + 2, rows, :]
            vtw = jnp.concatenate([vt_ref[i, rows, :], vt_ref[i + 1, rows, :], vt3], axis=1)
            pts, extras, pxs = [], [], []
            for g in range(Q_PER_KV):
                half, pair = g % 2, g // 2
                h = kv * Q_PER_KV + g
                r0 = half * 3 * BLOCK
                cols = slice(pair * BLOCK, (pair + 1) * BLOCK)
                s1 = s_ref[slot, kv, r0:r0 + BLOCK, cols]
                s2 = s_ref[slot, kv, r0 + BLOCK:r0 + 2 * BLOCK, cols]
                s3 = s_ref[slot, kv, r0 + 2 * BLOCK:r0 + 3 * BLOCK, cols]
                if edge == "lo":
                    mrg = jnp.where(upper, s1 + pen1, s3)
                elif edge == "hi":
                    mrg = jnp.where(upper, s1, s3 + pen3)
                else:
                    mrg = jnp.where(upper, s1, s3)
                a_mid = s2 + bias_ref[h, 0:BLOCK, :]
                a_mrg = mrg + bias_ref[h, BLOCK:, :]
                sink2 = sink_ref[h] * LOG2E
                s_x = (jnp.sum(jnp.where(diag, s3, 0.0), axis=0, keepdims=True)
                       - slopes[h] * LOG2E * WINDOW)
                if edge == "hi":
                    s_x = s_x + pen3
                m = jnp.max(jnp.maximum(a_mid, a_mrg), axis=0, keepdims=True)
                m = jnp.maximum(jnp.maximum(m, sink2), s_x)
                p_mid = jnp.exp2(a_mid - m).astype(bf16)
                p_mrg = jnp.exp2(a_mrg - m).astype(bf16)
                p_x = jnp.exp2(s_x - m)
                pxs.append(p_x)
                extras.append(jnp.exp2(sink2 - m) + p_x)
                p1 = p_mrg * mask_ref[...]
                p3 = p_mrg - p1
                pts.append(jnp.concatenate([p1, p_mid, p3], axis=0))
                if len(pts) < PV_HEADS:
                    continue
                pt = jnp.concatenate(pts, axis=1)
                ot = jnp.dot(jnp.concatenate([vtw, ones_rows], axis=0), pt,
                             preferred_element_type=f32)
                for n in range(PV_HEADS):
                    hh = h - (PV_HEADS - 1) + n
                    cols = slice(n * BLOCK, (n + 1) * BLOCK)
                    o = ot[0:HEAD_DIM, cols] + vt3.astype(f32) * pxs[n]
                    inv = 1.0 / (ot[HEAD_DIM:HEAD_DIM + 1, cols] + extras[n])
                    ot_ref[slot, hh * HEAD_DIM:(hh + 1) * HEAD_DIM, :] = o * inv
                pts, extras, pxs = [], [], []

    def finish(i, slot):
        start = pl.multiple_of(i * BLOCK, BLOCK)
        local = pl.multiple_of((i - i0) * BLOCK, BLOCK)
        gate = za_ref[0, pl.ds(start, BLOCK), o_g:o_g + D_ATTN].astype(f32)
        ya_ref[pl.ds(local, BLOCK), :] = (ot_ref[slot].T * gate).astype(ya_ref.dtype)

    def two_blocks(j):
        i = i0 + 2 * j + 1
        scores(i + 1, 0)
        softmax_pv(i, 1, None)
        finish(i - 1, 0)
        scores(i + 2, 1)
        softmax_pv(i + 1, 0, None)
        finish(i, 1)

    def project(c):
        r = slice(2 * c * BLOCK, 2 * (c + 1) * BLOCK)
        x0g = x0g_ref[0, r, :].astype(f32)
        yh = jnp.concatenate([ref[0, r, :] for ref in yh_refs], axis=1).astype(f32)
        y = jnp.dot(jnp.concatenate([(yh * x0g).astype(bf16), ya_ref[r, :]], axis=1), w_ref[...],
                    preferred_element_type=f32)
        ms = jnp.mean(y * y, axis=-1, keepdims=True)
        o_ref[0, r, :] = x_ref[0, r, :] + y * lax.rsqrt(ms + RMS_EPS) * g_ref[...]

    scores(i0, 0)
    scores(i0 + 1, 1)
    softmax_pv(i0, 0, "lo")
    npairs = nbs // 2
    for j in range(npairs - 1):
        two_blocks(j)
        if j >= 1:
            project(j - 1)
    softmax_pv(i0 + nbs - 1, 1, "hi")
    finish(i0 + nbs - 2, 0)
    finish(i0 + nbs - 1, 1)
    for c in range(max(npairs - 2, 0), npairs):
        project(c)


def _back(za, sink, x, yh_parts, x0g, w_out, post_g, tm=1024):
    B, L, D = x.shape
    nb = L // BLOCK
    nbs = tm // BLOCK
    assert WINDOW == BLOCK and nbs >= 4 and nbs % 2 == 0 and L % tm == 0
    nkt = nb + 2
    bias = jnp.asarray(_attn_bias_tables())
    return pl.pallas_call(
        _back_kernel,
        grid=(B, L // tm),
        in_specs=[
            pl.BlockSpec(memory_space=pltpu.SMEM),
            pl.BlockSpec((N_HEADS, 2 * BLOCK, BLOCK), lambda b, t: (0, 0, 0)),
            pl.BlockSpec((1, L, N_ATT), lambda b, t: (b, 0, 0)),
            pl.BlockSpec((1, tm, D), lambda b, t: (b, t, 0)),
        ] + [pl.BlockSpec((1, tm, HY_PART_CH), lambda b, t: (b, t, 0))] * HY_PARTS + [
            pl.BlockSpec((1, tm, D_HYENA), lambda b, t: (b, t, 0)),
            pl.BlockSpec((D_HYENA + D_ATTN, D), lambda b, t: (0, 0)),
            pl.BlockSpec((1, D), lambda b, t: (0, 0)),
        ],
        out_specs=pl.BlockSpec((1, tm, D), lambda b, t: (b, t, 0)),
        out_shape=jax.ShapeDtypeStruct((B, L, D), jnp.float32),
        scratch_shapes=[
            pltpu.VMEM((2 * N_KV_HEADS, nkt, BLOCK, KV_DIM), jnp.bfloat16),
            pltpu.VMEM((nkt, KV_DIM, BLOCK), jnp.bfloat16),
            pltpu.VMEM((2, N_KV_HEADS, 6 * BLOCK, 2 * BLOCK), jnp.float32),
            pltpu.VMEM((2, D_ATTN, BLOCK), jnp.float32),
            pltpu.VMEM((BLOCK, BLOCK), jnp.bfloat16),
            pltpu.VMEM((tm, D_ATTN), jnp.bfloat16),
        ],
        compiler_params=pltpu.CompilerParams(
            dimension_semantics=("parallel", "arbitrary"), vmem_limit_bytes=VMEM_LIMIT),
        name="attn_outproj",
    )(sink, bias, za, x, *yh_parts, x0g, w_out, post_g[None, :])


def _hyena_kernel(k_ref, d_ref, v_ref, y_ref, *scratch):
    tables = scratch[:HY_UNROLL]
    accs = scratch[HY_UNROLL:]
    step = pl.program_id(0)
    cg = v_ref.shape[0]
    rows = v_ref.shape[1]
    nchan, n2 = k_ref.shape
    nblk = n2 // (2 * TBLK)
    bsz = rows // nblk
    off = TBLK * (nblk - 1) + SHIFT_ROWS

    def build_table(chan, s_ref):
        krow = k_ref[pl.ds(jnp.minimum(chan, nchan - 1), 1), :]
        kb = jnp.broadcast_to(krow, (SHIFT_ROWS, n2))
        s_ref[...] = pltpu.roll(kb, off, axis=1, stride=1, stride_axis=0).astype(s_ref.dtype)

    def convolve(ci, s_ref, acc_ref):
        acc_ref[...] = d_ref[pl.ds(ci, 1), :] * v_ref[ci].astype(jnp.float32)
        for d in range(-(nblk - 1), nblk):
            x0 = TBLK * (d + nblk - 1)
            w = jnp.concatenate(
                [s_ref[:, x0 + SHIFT_ROWS:x0 + SHIFT_ROWS + TBLK], s_ref[:, x0:x0 + TBLK]], axis=0)
            n = (nblk - abs(d)) * bsz
            src = 0 if d >= 0 else -d * bsz
            dst = d * bsz if d >= 0 else 0
            acc_ref[dst:dst + n, :] += jnp.dot(v_ref[ci, pl.ds(src, n), :], w,
                                               preferred_element_type=jnp.float32)
        y_ref[ci] = acc_ref[...].astype(y_ref.dtype)

    @pl.when(step == 0)
    def _():
        for u in range(HY_UNROLL):
            build_table(u, tables[u])

    def body(it, carry):
        for u in range(HY_UNROLL):
            convolve(it * HY_UNROLL + u, tables[u], accs[u])
        for u in range(HY_UNROLL):
            build_table(step * cg + (it + 1) * HY_UNROLL + u, tables[u])
        return carry

    lax.fori_loop(0, cg // HY_UNROLL, body, 0)


def _hyena_conv(kt, hyena_d, v_rows, part, cg=16):
    C, rows, _ = v_rows.shape
    n2 = kt.shape[1]
    steps = C // cg
    return pl.pallas_call(
        _hyena_kernel,
        grid=(steps,),
        in_specs=[
            pl.BlockSpec((C, n2), lambda c: (part, 0)),
            pl.BlockSpec((cg, 1), lambda c: (c + part * steps, 0)),
            pl.BlockSpec((cg, rows, TBLK), lambda c: (c, 0, 0)),
        ],
        out_specs=pl.BlockSpec((cg, rows, TBLK), lambda c: (c, 0, 0)),
        out_shape=jax.ShapeDtypeStruct((C, rows, TBLK), jnp.bfloat16),
        scratch_shapes=(
            [pltpu.VMEM((SHIFT_ROWS, n2), jnp.bfloat16)] * HY_UNROLL
            + [pltpu.VMEM((rows, TBLK), jnp.float32)] * HY_UNROLL),
        compiler_params=pltpu.CompilerParams(
            dimension_semantics=("arbitrary",), vmem_limit_bytes=VMEM_LIMIT),
        name="hyena_conv",
    )(kt, hyena_d[:, None], v_rows)


def _layer(x, pre_g, w_in, w_short, b_short, w_f1, b_f1, w_f2, b_f2, w_f3, b_f3, w_f4,
           sin_freq, hyena_d, attn_sink, w_out, post_g):
    B, L, _ = x.shape
    nblk = L // TBLK
    o_q = 4 * D_HYENA
    o_k = o_q + D_ATTN
    o_v = o_k + KV_DIM
    o_ag = o_v + KV_DIM
    w_cat = jnp.concatenate(
        [w_in[:, :o_q], w_in[:, o_q:o_k], w_in[:, o_ag:], w_in[:, o_k:o_v], w_in[:, o_v:o_ag]],
        axis=1).astype(jnp.bfloat16)

    kt = _hyena_filter(L, w_f1, b_f1, w_f2, b_f2, w_f3, b_f3, w_f4, sin_freq)
    *v_parts, x0g, za = _inproj(x, pre_g, w_cat, w_short, b_short)
    pc = HY_PART_CH
    yh = []
    for part, v in enumerate(v_parts):
        v_rows = v.reshape(B, nblk, TBLK, pc).transpose(3, 1, 0, 2).reshape(pc, nblk * B, TBLK)
        y_rows = _hyena_conv(kt, hyena_d, v_rows, part)
        yh.append(y_rows.reshape(pc, nblk, B, TBLK).transpose(2, 1, 3, 0).reshape(B, L, pc))
    return _back(za, attn_sink, x, yh, x0g, w_out.astype(jnp.bfloat16), post_g)


def kernel(x, pre_g, w_in, w_short, b_short, w_f1, b_f1, w_f2, b_f2, w_f3, b_f3, w_f4, sin_freq, hyena_d, attn_sink, w_out, post_g):
    depth = pre_g.shape[0]
    for l in range(depth):
        x = _layer(x, pre_g[l], w_in[l], w_short[l], b_short[l], w_f1[l], b_f1[l], w_f2[l],
                   b_f2[l], w_f3[l], b_f3[l], w_f4[l], sin_freq[l], hyena_d[l], attn_sink[l],
                   w_out[l], post_g[l])
    return x
```

```python
import math

import jax
import jax.numpy as jnp
import numpy as np
from jax import lax
from jax.experimental import pallas as pl
from jax.experimental.pallas import tpu as pltpu

D_MODEL = 1024
D_HYENA = 512
D_ATTN = 512
N_HEADS = 8
HEAD_DIM = 64
N_KV_HEADS = 2
Q_PER_KV = N_HEADS // N_KV_HEADS
KV_DIM = N_KV_HEADS * HEAD_DIM
WINDOW = 128
BLOCK = 128
FILTER_ORDER = 64
N_BANDS = 16
POS_EMB_DIM = 1 + 2 * N_BANDS
POS_EMB_PAD = 40
DECAY_TARGET = 1e-2
FAST_DECAY_PCT = 0.3
SLOW_DECAY_PCT = 1.5
RMS_EPS = 1e-6
NEG_INF = -1e30
PV_HEADS = 2
LOG2E = math.log2(math.e)
Q_SCALE_LOG2 = HEAD_DIM ** -0.5 * LOG2E

N_HY = 3 * D_HYENA
O_GH = N_HY
O_ATT = O_GH + D_HYENA
N_ATT = 2 * D_ATTN + 2 * KV_DIM
D_IN = O_ATT + N_ATT

LANES = 128
HALO = 16

TBLK = 256
SHIFT_ROWS = 128
HY_UNROLL = 4
HY_PARTS = 4
HY_PART_CH = D_HYENA // HY_PARTS

VMEM_LIMIT = 56 * 1024 * 1024

_HI = lax.Precision.HIGHEST


def _filter_consts(L):
    t = np.arange(L, dtype=np.float32)
    t_norm = t / np.float32(max(L - 1, 1))
    w = np.float32(2.0 * math.pi) * t / np.float32(L)
    bands = np.linspace(1e-4, N_BANDS - 1, N_BANDS).astype(np.float32)
    ang = w[:, None] * bands[None, :]
    z = np.concatenate([t_norm[:, None], np.cos(ang), -np.sin(ang)], axis=-1)
    min_decay = math.log(DECAY_TARGET) / SLOW_DECAY_PCT
    max_decay = math.log(DECAY_TARGET) / FAST_DECAY_PCT
    deltas = np.abs(np.linspace(min_decay, max_decay, D_HYENA)).astype(np.float32)
    decay = np.exp(-t_norm[:, None] * deltas[None, :]).astype(np.float32)
    pad = np.zeros((POS_EMB_PAD - POS_EMB_DIM, L), np.float32)
    zt = np.concatenate([z.T, pad], axis=0)
    flip = np.eye(LANES, dtype=np.float32)[::-1]
    return zt, np.ascontiguousarray(decay.T), np.ascontiguousarray(flip)


def _filter_kernel(zt_ref, dec_ref, flip_ref, p_ref, w4_ref, k_ref):
    L = zt_ref.shape[1]
    f32 = jnp.float32
    w1 = p_ref[:, 0:POS_EMB_PAD]
    w2 = p_ref[:, LANES:LANES + FILTER_ORDER]
    w3 = p_ref[:, 2 * LANES:2 * LANES + FILTER_ORDER]
    vec = p_ref[:, 3 * LANES:4 * LANES]
    w4 = w4_ref[...].T
    h = jnp.dot(w1, zt_ref[...], precision=_HI, preferred_element_type=f32)
    h = jnp.sin(vec[:, 3:4] * (h + vec[:, 0:1]))
    h = jnp.dot(w2, h, precision=_HI, preferred_element_type=f32)
    h = jnp.sin(vec[:, 4:5] * (h + vec[:, 1:2]))
    h = jnp.dot(w3, h, precision=_HI, preferred_element_type=f32)
    h = jnp.sin(vec[:, 5:6] * (h + vec[:, 2:3]))
    hf = jnp.dot(w4[0:D_HYENA, :], h, precision=_HI, preferred_element_type=f32) * dec_ref[...]
    hb = jnp.dot(w4[D_HYENA:, :], h, precision=_HI, preferred_element_type=f32) * dec_ref[...]
    nt = L // LANES
    rev = jnp.concatenate(
        [jnp.dot(hb[:, (nt - 1 - j) * LANES:(nt - j) * LANES], flip_ref[...], precision=_HI,
                 preferred_element_type=f32) for j in range(nt)], axis=1)
    col = lax.broadcasted_iota(jnp.int32, rev.shape, 1)
    hb2 = jnp.where(col == 0, 0.0, pltpu.roll(rev, 1, axis=1))
    ss = jnp.sum(hf * hf, axis=1, keepdims=True) + jnp.sum(hb2 * hb2, axis=1, keepdims=True)
    inv = lax.rsqrt(ss + 1e-12)
    k_ref[:, 0:L] = hf * inv
    k_ref[:, L:2 * L] = hb2 * inv


def _hyena_filter(L, w_f1, b_f1, w_f2, b_f2, w_f3, b_f3, w_f4, sin_freq):
    zt, dec, flip = _filter_consts(L)
    def tile(a):
        return jnp.pad(a, ((0, 0), (0, LANES - a.shape[1])))

    vectors = jnp.stack([b_f1, b_f2, b_f3, sin_freq[0], sin_freq[1], sin_freq[2]], axis=1)
    params = jnp.concatenate([tile(w_f1.T), tile(w_f2.T), tile(w_f3.T), tile(vectors)], axis=1)
    args = (jnp.asarray(zt), jnp.asarray(dec), jnp.asarray(flip), params, w_f4)
    return pl.pallas_call(
        _filter_kernel,
        out_shape=jax.ShapeDtypeStruct((D_HYENA, 2 * L), jnp.float32),
        compiler_params=pltpu.CompilerParams(vmem_limit_bytes=VMEM_LIMIT),
        name="hyena_filter",
    )(*args)


def _inproj_kernel(xp_ref, x_ref, xn_ref, g_ref, w_ref, wsh_ref, bsh_ref, *out_refs):
    v_refs = out_refs[:HY_PARTS]
    x0g_ref, za_ref = out_refs[HY_PARTS:]
    t = pl.program_id(1)
    nt = pl.num_programs(1)
    tm = x_ref.shape[1]

    def norm(xv):
        ms = jnp.mean(xv * xv, axis=-1, keepdims=True)
        return (xv * lax.rsqrt(ms + RMS_EPS) * g_ref[...]).astype(jnp.bfloat16)

    h = norm(x_ref[0])
    hp = jnp.where(t > 0, norm(xp_ref[0]), 0.0).astype(jnp.bfloat16)
    hn = jnp.where(t < nt - 1, norm(xn_ref[0]), 0.0).astype(jnp.bfloat16)
    hall = jnp.concatenate([hp, h, hn], axis=0)
    zh = jnp.dot(hall, w_ref[:, 0:N_HY], preferred_element_type=jnp.float32)
    z = jnp.dot(h, w_ref[:, N_HY:], preferred_element_type=jnp.float32)

    rows = tm + 2 * HALO
    u_prev = pltpu.roll(zh, 1, axis=0)[HALO:HALO + tm]
    u = zh[HALO:HALO + tm]
    u_next = pltpu.roll(zh, rows - 1, axis=0)[HALO:HALO + tm]
    uc = (u_prev * wsh_ref[0:1, :] + u * wsh_ref[1:2, :] + u_next * wsh_ref[2:3, :]
          + bsh_ref[...])
    x0 = uc[:, 0:D_HYENA]
    x1 = uc[:, D_HYENA:2 * D_HYENA]
    vv = uc[:, 2 * D_HYENA:]
    gh = z[:, 0:D_HYENA]
    vx = (vv * x1).astype(v_refs[0].dtype)
    for p, v_ref in enumerate(v_refs):
        v_ref[0] = vx[:, p * HY_PART_CH:(p + 1) * HY_PART_CH]
    x0g_ref[0] = (x0 * (gh * jax.nn.sigmoid(gh))).astype(x0g_ref.dtype)
    o_q = O_ATT - N_HY
    za_ref[0, :, 0:D_ATTN] = (z[:, o_q:o_q + D_ATTN] * Q_SCALE_LOG2).astype(za_ref.dtype)
    ga = z[:, o_q + D_ATTN:o_q + 2 * D_ATTN]
    za_ref[0, :, D_ATTN:2 * D_ATTN] = (ga * jax.nn.sigmoid(ga)).astype(za_ref.dtype)
    za_ref[0, :, 2 * D_ATTN:] = z[:, o_q + 2 * D_ATTN:].astype(za_ref.dtype)


def _inproj(x, pre_g, w_cat, w_short, b_short, tm=1024):
    B, L, D = x.shape
    rb = tm // HALO
    nrb = L // HALO
    grid = (B, L // tm)
    return pl.pallas_call(
        _inproj_kernel,
        grid=grid,
        in_specs=[
            pl.BlockSpec((1, HALO, D), lambda b, t: (b, jnp.maximum(t * rb - 1, 0), 0)),
            pl.BlockSpec((1, tm, D), lambda b, t: (b, t, 0)),
            pl.BlockSpec((1, HALO, D), lambda b, t: (b, jnp.minimum((t + 1) * rb, nrb - 1), 0)),
            pl.BlockSpec((1, D), lambda b, t: (0, 0)),
            pl.BlockSpec((D, D_IN), lambda b, t: (0, 0)),
            pl.BlockSpec((3, N_HY), lambda b, t: (0, 0)),
            pl.BlockSpec((1, N_HY), lambda b, t: (0, 0)),
        ],
        out_specs=(
            [pl.BlockSpec((1, tm, HY_PART_CH), lambda b, t: (b, t, 0))] * HY_PARTS
            + [pl.BlockSpec((1, tm, D_HYENA), lambda b, t: (b, t, 0)),
               pl.BlockSpec((1, tm, N_ATT), lambda b, t: (b, t, 0))]),
        out_shape=(
            [jax.ShapeDtypeStruct((B, L, HY_PART_CH), jnp.bfloat16)] * HY_PARTS
            + [jax.ShapeDtypeStruct((B, L, D_HYENA), jnp.bfloat16),
               jax.ShapeDtypeStruct((B, L, N_ATT), jnp.bfloat16)]),
        compiler_params=pltpu.CompilerParams(
            dimension_semantics=("parallel", "arbitrary"), vmem_limit_bytes=VMEM_LIMIT),
        name="inproj",
    )(x, x, x, pre_g[None, :], w_cat, w_short, b_short[None, :])


def _alibi_slopes():
    return [float(v) for v in
            np.exp2(-8.0 * np.arange(1, N_HEADS + 1, dtype=np.float32) / N_HEADS).astype(np.float32)]


def _attn_bias_tables():
    c = np.arange(BLOCK)[:, None]
    r = np.arange(BLOCK)[None, :]
    dist = np.abs(c - r).astype(np.float32)
    slopes = np.asarray(_alibi_slopes(), np.float32)[:, None, None] * np.float32(LOG2E)
    return np.concatenate([-slopes * dist, -slopes * (WINDOW - dist)], axis=1).astype(np.float32)


def _back_kernel(sink_ref, bias_ref, za_ref, x_ref, *refs):
    yh_refs = refs[:HY_PARTS]
    (x0g_ref, w_ref, g_ref, o_ref,
     kpl_ref, vt_ref, s_ref, ot_ref, mask_ref, ya_ref) = refs[HY_PARTS:]
    t = pl.program_id(1)
    tm = x_ref.shape[1]
    L = za_ref.shape[1]
    nb = L // BLOCK
    nbs = tm // BLOCK
    i0 = t * nbs
    o_g = D_ATTN
    o_k = 2 * D_ATTN
    o_v = o_k + KV_DIM
    f32 = jnp.float32
    bf16 = jnp.bfloat16
    slopes = _alibi_slopes()
    group = Q_PER_KV * HEAD_DIM

    c_i = lax.broadcasted_iota(jnp.int32, (BLOCK, BLOCK), 0)
    r_i = lax.broadcasted_iota(jnp.int32, (BLOCK, BLOCK), 1)
    upper = c_i >= r_i
    diag = c_i == r_i
    ones_rows = jnp.ones((16, 3 * BLOCK), bf16)

    @pl.when(t == 0)
    def _():
        lane = lax.broadcasted_iota(jnp.int32, (BLOCK, KV_DIM), 1)
        lo = lane < HEAD_DIM
        ztile = jnp.zeros((BLOCK, KV_DIM), bf16)
        for idx in range(2 * N_KV_HEADS):
            kpl_ref[idx, 0] = ztile
            kpl_ref[idx, nb + 1] = ztile
        vt_ref[0] = ztile
        vt_ref[nb + 1] = ztile
        mask_ref[...] = jnp.where(upper, 1.0, 0.0).astype(bf16)

        def prep(j, carry):
            start = pl.multiple_of(j * BLOCK, BLOCK)
            kc = za_ref[0, pl.ds(start, BLOCK), o_k:o_k + KV_DIM].astype(f32)
            ksw = pltpu.roll(kc, HEAD_DIM, axis=1)
            kpl_ref[0, j + 1] = jnp.where(lo, kc, 0.0).astype(bf16)
            kpl_ref[1, j + 1] = jnp.where(lo, 0.0, ksw).astype(bf16)
            kpl_ref[2, j + 1] = jnp.where(lo, ksw, 0.0).astype(bf16)
            kpl_ref[3, j + 1] = jnp.where(lo, 0.0, kc).astype(bf16)
            vc = za_ref[0, pl.ds(start, BLOCK), o_v:o_v + KV_DIM].astype(f32)
            vt_ref[j + 1] = vc.T.astype(bf16)
            return carry

        lax.fori_loop(0, nb, prep, 0, unroll=4)

    def scores(i, slot):
        start = pl.multiple_of(i * BLOCK, BLOCK)
        qblk = za_ref[0, pl.ds(start, BLOCK), 0:D_ATTN]
        for kv in range(N_KV_HEADS):
            base = kv * group
            qs = jnp.concatenate([qblk[:, base:base + 2 * HEAD_DIM],
                                  qblk[:, base + 2 * HEAD_DIM:base + group]], axis=0)
            kp = jnp.concatenate([kpl_ref[2 * kv + half, i + tt]
                                  for half in range(2) for tt in range(3)], axis=0)
            s_ref[slot, kv] = lax.dot_general(kp, qs, (((1,), (1,)), ((), ())),
                                              preferred_element_type=f32)

    def softmax_pv(i, slot, edge):
        pen1 = jnp.where(i == 0, NEG_INF, 0.0).astype(f32)
        pen3 = jnp.where(i == nb - 1, NEG_INF, 0.0).astype(f32)
        for kv in range(N_KV_HEADS):
            rows = slice(kv * HEAD_DIM, (kv + 1) * HEAD_DIM)
            vt3 = vt_ref[i + 2, rows, :]
            vtw = jnp.concatenate([vt_ref[i, rows, :], vt_ref[i + 1, rows, :], vt3], axis=1)
            pts, extras, pxs = [], [], []
            for g in range(Q_PER_KV):
                half, pair = g % 2, g // 2
                h = kv * Q_PER_KV + g
                r0 = half * 3 * BLOCK
                cols = slice(pair * BLOCK, (pair + 1) * BLOCK)
                s1 = s_ref[slot, kv, r0:r0 + BLOCK, cols]
                s2 = s_ref[slot, kv, r0 + BLOCK:r0 + 2 * BLOCK, cols]
                s3 = s_ref[slot, kv, r0 + 2 * BLOCK:r0 + 3 * BLOCK, cols]
                if edge == "lo":
                    mrg = jnp.where(upper, s1 + pen1, s3)
                elif edge == "hi":
                    mrg = jnp.where(upper, s1, s3 + pen3)
                else:
                    mrg = jnp.where(upper, s1, s3)
                a_mid = s2 + bias_ref[h, 0:BLOCK, :]
                a_mrg = mrg + bias_ref[h, BLOCK:, :]
                sink2 = sink_ref[h] * LOG2E
                s_x = (jnp.sum(jnp.where(diag, s3, 0.0), axis=0, keepdims=True)
                       - slopes[h] * LOG2E * WINDOW)
                if edge == "hi":
                    s_x = s_x + pen3
                m = jnp.max(jnp.maximum(a_mid, a_mrg), axis=0, keepdims=True)
                m = jnp.maximum(jnp.maximum(m, sink2), s_x)
                p_mid = jnp.exp2(a_mid - m).astype(bf16)
                p_mrg = jnp.exp2(a_mrg - m).astype(bf16)
                p_x = jnp.exp2(s_x - m)
                pxs.append(p_x)
                extras.append(jnp.exp2(sink2 - m) + p_x)
                p1 = p_mrg * mask_ref[...]
                p3 = p_mrg - p1
                pts.append(jnp.concatenate([p1, p_mid, p3], axis=0))
                if len(pts) < PV_HEADS:
                    continue
                pt = jnp.concatenate(pts, axis=1)
                ot = jnp.dot(jnp.concatenate([vtw, ones_rows], axis=0), pt,
                             preferred_element_type=f32)
                for n in range(PV_HEADS):
                    hh = h - (PV_HEADS - 1) + n
                    cols = slice(n * BLOCK, (n + 1) * BLOCK)
                    o = ot[0:HEAD_DIM, cols] + vt3.astype(f32) * pxs[n]
                    inv = 1.0 / (ot[HEAD_DIM:HEAD_DIM + 1, cols] + extras[n])
                    ot_ref[slot, hh * HEAD_DIM:(hh + 1) * HEAD_DIM, :] = o * inv
                pts, extras, pxs = [], [], []

    def finish(i, slot):
        start = pl.multiple_of(i * BLOCK, BLOCK)
        local = pl.multiple_of((i - i0) * BLOCK, BLOCK)
        gate = za_ref[0, pl.ds(start, BLOCK), o_g:o_g + D_ATTN].astype(f32)
        ya_ref[pl.ds(local, BLOCK), :] = (ot_ref[slot].T * gate).astype(ya_ref.dtype)

    def two_blocks(j):
        i = i0 + 2 * j + 1
        scores(i + 1, 0)
        softmax_pv(i, 1, None)
        finish(i - 1, 0)
        scores(i + 2, 1)
        softmax_pv(i + 1, 0, None)
        finish(i, 1)

    def project(c):
        r = slice(2 * c * BLOCK, 2 * (c + 1) * BLOCK)
        x0g = x0g_ref[0, r, :].astype(f32)
        yh = jnp.concatenate([ref[0, r, :] for ref in yh_refs], axis=1).astype(f32)
        y = jnp.dot(jnp.concatenate([(yh * x0g).astype(bf16), ya_ref[r, :]], axis=1), w_ref[...],
                    preferred_element_type=f32)
        ms = jnp.mean(y * y, axis=-1, keepdims=True)
        o_ref[0, r, :] = x_ref[0, r, :] + y * lax.rsqrt(ms + RMS_EPS) * g_ref[...]

    scores(i0, 0)
    scores(i0 + 1, 1)
    softmax_pv(i0, 0, "lo")
    npairs = nbs // 2
    for j in range(npairs - 1):
        two_blocks(j)
        if j >= 1:
            project(j - 1)
    softmax_pv(i0 + nbs - 1, 1, "hi")
    finish(i0 + nbs - 2, 0)
    finish(i0 + nbs - 1, 1)
    for c in range(max(npairs - 2, 0), npairs):
        project(c)


def _back(za, sink, x, yh_parts, x0g, w_out, post_g, tm=1024):
    B, L, D = x.shape
    nb = L // BLOCK
    nbs = tm // BLOCK
    assert WINDOW == BLOCK and nbs >= 4 and nbs % 2 == 0 and L % tm == 0
    nkt = nb + 2
    bias = jnp.asarray(_attn_bias_tables())
    return pl.pallas_call(
        _back_kernel,
        grid=(B, L // tm),
        in_specs=[
            pl.BlockSpec(memory_space=pltpu.SMEM),
            pl.BlockSpec((N_HEADS, 2 * BLOCK, BLOCK), lambda b, t: (0, 0, 0)),
            pl.BlockSpec((1, L, N_ATT), lambda b, t: (b, 0, 0)),
            pl.BlockSpec((1, tm, D), lambda b, t: (b, t, 0)),
        ] + [pl.BlockSpec((1, tm, HY_PART_CH), lambda b, t: (b, t, 0))] * HY_PARTS + [
            pl.BlockSpec((1, tm, D_HYENA), lambda b, t: (b, t, 0)),
            pl.BlockSpec((D_HYENA + D_ATTN, D), lambda b, t: (0, 0)),
            pl.BlockSpec((1, D), lambda b, t: (0, 0)),
        ],
        out_specs=pl.BlockSpec((1, tm, D), lambda b, t: (b, t, 0)),
        out_shape=jax.ShapeDtypeStruct((B, L, D), jnp.float32),
        scratch_shapes=[
            pltpu.VMEM((2 * N_KV_HEADS, nkt, BLOCK, KV_DIM), jnp.bfloat16),
            pltpu.VMEM((nkt, KV_DIM, BLOCK), jnp.bfloat16),
            pltpu.VMEM((2, N_KV_HEADS, 6 * BLOCK, 2 * BLOCK), jnp.float32),
            pltpu.VMEM((2, D_ATTN, BLOCK), jnp.float32),
            pltpu.VMEM((BLOCK, BLOCK), jnp.bfloat16),
            pltpu.VMEM((tm, D_ATTN), jnp.bfloat16),
        ],
        compiler_params=pltpu.CompilerParams(
            dimension_semantics=("parallel", "arbitrary"), vmem_limit_bytes=VMEM_LIMIT),
        name="attn_outproj",
    )(sink, bias, za, x, *yh_parts, x0g, w_out, post_g[None, :])


def _hyena_kernel(k_ref, d_ref, v_ref, y_ref, *scratch):
    tables = scratch[:HY_UNROLL]
    accs = scratch[HY_UNROLL:]
    step = pl.program_id(0)
    cg = v_ref.shape[0]
    rows = v_ref.shape[1]
    nchan, n2 = k_ref.shape
    nblk = n2 // (2 * TBLK)
    bsz = rows // nblk
    off = TBLK * (nblk - 1) + SHIFT_ROWS

    def build_table(chan, s_ref):
        krow = k_ref[pl.ds(jnp.minimum(chan, nchan - 1), 1), :]
        kb = jnp.broadcast_to(krow, (SHIFT_ROWS, n2))
        s_ref[...] = pltpu.roll(kb, off, axis=1, stride=1, stride_axis=0).astype(s_ref.dtype)

    def convolve(ci, s_ref, acc_ref):
        acc_ref[...] = d_ref[pl.ds(ci, 1), :] * v_ref[ci].astype(jnp.float32)
        for d in range(-(nblk - 1), nblk):
            x0 = TBLK * (d + nblk - 1)
            w = jnp.concatenate(
                [s_ref[:, x0 + SHIFT_ROWS:x0 + SHIFT_ROWS + TBLK], s_ref[:, x0:x0 + TBLK]], axis=0)
            n = (nblk - abs(d)) * bsz
            src = 0 if d >= 0 else -d * bsz
            dst = d * bsz if d >= 0 else 0
            acc_ref[dst:dst + n, :] += jnp.dot(v_ref[ci, pl.ds(src, n), :], w,
                                               preferred_element_type=jnp.float32)
        y_ref[ci] = acc_ref[...].astype(y_ref.dtype)

    @pl.when(step == 0)
    def _():
        for u in range(HY_UNROLL):
            build_table(u, tables[u])

    def body(it, carry):
        for u in range(HY_UNROLL):
            convolve(it * HY_UNROLL + u, tables[u], accs[u])
        for u in range(HY_UNROLL):
            build_table(step * cg + (it + 1) * HY_UNROLL + u, tables[u])
        return carry

    lax.fori_loop(0, cg // HY_UNROLL, body, 0)


def _hyena_conv(kt, hyena_d, v_rows, part, cg=16):
    C, rows, _ = v_rows.shape
    n2 = kt.shape[1]
    steps = C // cg
    return pl.pallas_call(
        _hyena_kernel,
        grid=(steps,),
        in_specs=[
            pl.BlockSpec((C, n2), lambda c: (part, 0)),
            pl.BlockSpec((cg, 1), lambda c: (c + part * steps, 0)),
            pl.BlockSpec((cg, rows, TBLK), lambda c: (c, 0, 0)),
        ],
        out_specs=pl.BlockSpec((cg, rows, TBLK), lambda c: (c, 0, 0)),
        out_shape=jax.ShapeDtypeStruct((C, rows, TBLK), jnp.bfloat16),
        scratch_shapes=(
            [pltpu.VMEM((SHIFT_ROWS, n2), jnp.bfloat16)] * HY_UNROLL
            + [pltpu.VMEM((rows, TBLK), jnp.float32)] * HY_UNROLL),
        compiler_params=pltpu.CompilerParams(
            dimension_semantics=("arbitrary",), vmem_limit_bytes=VMEM_LIMIT),
        name="hyena_conv",
    )(kt, hyena_d[:, None], v_rows)


def _layer(x, pre_g, w_in, w_short, b_short, w_f1, b_f1, w_f2, b_f2, w_f3, b_f3, w_f4,
           sin_freq, hyena_d, attn_sink, w_out, post_g):
    B, L, _ = x.shape
    nblk = L // TBLK
    o_q = 4 * D_HYENA
    o_k = o_q + D_ATTN
    o_v = o_k + KV_DIM
    o_ag = o_v + KV_DIM
    w_cat = jnp.concatenate(
        [w_in[:, :o_q], w_in[:, o_q:o_k], w_in[:, o_ag:], w_in[:, o_k:o_v], w_in[:, o_v:o_ag]],
        axis=1).astype(jnp.bfloat16)

    kt = _hyena_filter(L, w_f1, b_f1, w_f2, b_f2, w_f3, b_f3, w_f4, sin_freq)
    *v_parts, x0g, za = _inproj(x, pre_g, w_cat, w_short, b_short)
    pc = HY_PART_CH
    yh = []
    for part, v in enumerate(v_parts):
        v_rows = v.reshape(B, nblk, TBLK, pc).transpose(3, 1, 0, 2).reshape(pc, nblk * B, TBLK)
        y_rows = _hyena_conv(kt, hyena_d, v_rows, part)
        yh.append(y_rows.reshape(pc, nblk, B, TBLK).transpose(2, 1, 3, 0).reshape(B, L, pc))
    return _back(za, attn_sink, x, yh, x0g, w_out.astype(jnp.bfloat16), post_g)


def kernel(x, pre_g, w_in, w_short, b_short, w_f1, b_f1, w_f2, b_f2, w_f3, b_f3, w_f4, sin_freq, hyena_d, attn_sink, w_out, post_g):
    depth = pre_g.shape[0]
    for l in range(depth):
        x = _layer(x, pre_g[l], w_in[l], w_short[l], b_short[l], w_f1[l], b_f1[l], w_f2[l],
                   b_f2[l], w_f3[l], b_f3[l], w_f4[l], sin_freq[l], hyena_d[l], attn_sink[l],
                   w_out[l], post_g[l])
    return x
```

```python
import math

import jax
import jax.numpy as jnp
import numpy as np
from jax import lax
from jax.experimental import pallas as pl
from jax.experimental.pallas import tpu as pltpu

D_MODEL = 1024
D_HYENA = 512
D_ATTN = 512
N_HEADS = 8
HEAD_DIM = 64
N_KV_HEADS = 2
Q_PER_KV = N_HEADS // N_KV_HEADS
KV_DIM = N_KV_HEADS * HEAD_DIM
WINDOW = 128
BLOCK = 128
FILTER_ORDER = 64
N_BANDS = 16
POS_EMB_DIM = 1 + 2 * N_BANDS
POS_EMB_PAD = 40
DECAY_TARGET = 1e-2
FAST_DECAY_PCT = 0.3
SLOW_DECAY_PCT = 1.5
RMS_EPS = 1e-6
NEG_INF = -1e30
PV_HEADS = 2
LOG2E = math.log2(math.e)
Q_SCALE_LOG2 = HEAD_DIM ** -0.5 * LOG2E

N_HY = 3 * D_HYENA
O_GH = N_HY
O_ATT = O_GH + D_HYENA
N_ATT = 2 * D_ATTN + 2 * KV_DIM
D_IN = O_ATT + N_ATT
O_KPL = 2 * D_ATTN
ZA_W = O_KPL + 4 * KV_DIM

LANES = 128
HALO = 16

TBLK = 256
SHIFT_ROWS = 128
HY_UNROLL = 4
HY_PARTS = 4
HY_PART_CH = D_HYENA // HY_PARTS

VMEM_LIMIT = 56 * 1024 * 1024

_HI = lax.Precision.HIGHEST


def _filter_consts(L):
    t = np.arange(L, dtype=np.float32)
    t_norm = t / np.float32(max(L - 1, 1))
    w = np.float32(2.0 * math.pi) * t / np.float32(L)
    bands = np.linspace(1e-4, N_BANDS - 1, N_BANDS).astype(np.float32)
    ang = w[:, None] * bands[None, :]
    z = np.concatenate([t_norm[:, None], np.cos(ang), -np.sin(ang)], axis=-1)
    min_decay = math.log(DECAY_TARGET) / SLOW_DECAY_PCT
    max_decay = math.log(DECAY_TARGET) / FAST_DECAY_PCT
    deltas = np.abs(np.linspace(min_decay, max_decay, D_HYENA)).astype(np.float32)
    decay = np.exp(-t_norm[:, None] * deltas[None, :]).astype(np.float32)
    pad = np.zeros((POS_EMB_PAD - POS_EMB_DIM, L), np.float32)
    zt = np.concatenate([z.T, pad], axis=0)
    flip = np.eye(LANES, dtype=np.float32)[::-1]
    return zt, np.ascontiguousarray(decay.T), np.ascontiguousarray(flip)


def _filter_kernel(zt_ref, dec_ref, flip_ref, p_ref, w4_ref, k_ref):
    L = zt_ref.shape[1]
    f32 = jnp.float32
    w1 = p_ref[:, 0:POS_EMB_PAD]
    w2 = p_ref[:, LANES:LANES + FILTER_ORDER]
    w3 = p_ref[:, 2 * LANES:2 * LANES + FILTER_ORDER]
    vec = p_ref[:, 3 * LANES:4 * LANES]
    w4 = w4_ref[...].T
    h = jnp.dot(w1, zt_ref[...], precision=_HI, preferred_element_type=f32)
    h = jnp.sin(vec[:, 3:4] * (h + vec[:, 0:1]))
    h = jnp.dot(w2, h, precision=_HI, preferred_element_type=f32)
    h = jnp.sin(vec[:, 4:5] * (h + vec[:, 1:2]))
    h = jnp.dot(w3, h, precision=_HI, preferred_element_type=f32)
    h = jnp.sin(vec[:, 5:6] * (h + vec[:, 2:3]))
    hf = jnp.dot(w4[0:D_HYENA, :], h, precision=_HI, preferred_element_type=f32) * dec_ref[...]
    hb = jnp.dot(w4[D_HYENA:, :], h, precision=_HI, preferred_element_type=f32) * dec_ref[...]
    nt = L // LANES
    rev = jnp.concatenate(
        [jnp.dot(hb[:, (nt - 1 - j) * LANES:(nt - j) * LANES], flip_ref[...], precision=_HI,
                 preferred_element_type=f32) for j in range(nt)], axis=1)
    col = lax.broadcasted_iota(jnp.int32, rev.shape, 1)
    hb2 = jnp.where(col == 0, 0.0, pltpu.roll(rev, 1, axis=1))
    ss = jnp.sum(hf * hf, axis=1, keepdims=True) + jnp.sum(hb2 * hb2, axis=1, keepdims=True)
    inv = lax.rsqrt(ss + 1e-12)
    k_ref[:, 0:L] = hf * inv
    k_ref[:, L:2 * L] = hb2 * inv


def _hyena_filter(L, w_f1, b_f1, w_f2, b_f2, w_f3, b_f3, w_f4, sin_freq):
    zt, dec, flip = _filter_consts(L)
    def tile(a):
        return jnp.pad(a, ((0, 0), (0, LANES - a.shape[1])))

    vectors = jnp.stack([b_f1, b_f2, b_f3, sin_freq[0], sin_freq[1], sin_freq[2]], axis=1)
    params = jnp.concatenate([tile(w_f1.T), tile(w_f2.T), tile(w_f3.T), tile(vectors)], axis=1)
    args = (jnp.asarray(zt), jnp.asarray(dec), jnp.asarray(flip), params, w_f4)
    return pl.pallas_call(
        _filter_kernel,
        out_shape=jax.ShapeDtypeStruct((D_HYENA, 2 * L), jnp.float32),
        compiler_params=pltpu.CompilerParams(vmem_limit_bytes=VMEM_LIMIT),
        name="hyena_filter",
    )(*args)


def _inproj_kernel(xp_ref, x_ref, xn_ref, g_ref, w_ref, wsh_ref, bsh_ref, *out_refs):
    v_refs = out_refs[:HY_PARTS]
    x0g_ref, za_ref, vt_ref = out_refs[HY_PARTS:]
    t = pl.program_id(1)
    nt = pl.num_programs(1)
    tm = x_ref.shape[1]

    def norm(xv):
        ms = jnp.mean(xv * xv, axis=-1, keepdims=True)
        return (xv * lax.rsqrt(ms + RMS_EPS) * g_ref[...]).astype(jnp.bfloat16)

    h = norm(x_ref[0])
    hp = jnp.where(t > 0, norm(xp_ref[0]), 0.0).astype(jnp.bfloat16)
    hn = jnp.where(t < nt - 1, norm(xn_ref[0]), 0.0).astype(jnp.bfloat16)
    hall = jnp.concatenate([hp, h, hn], axis=0)
    zh = jnp.dot(hall, w_ref[:, 0:N_HY], preferred_element_type=jnp.float32)
    z = jnp.dot(h, w_ref[:, N_HY:], preferred_element_type=jnp.float32)

    rows = tm + 2 * HALO
    u_prev = pltpu.roll(zh, 1, axis=0)[HALO:HALO + tm]
    u = zh[HALO:HALO + tm]
    u_next = pltpu.roll(zh, rows - 1, axis=0)[HALO:HALO + tm]
    uc = (u_prev * wsh_ref[0:1, :] + u * wsh_ref[1:2, :] + u_next * wsh_ref[2:3, :]
          + bsh_ref[...])
    x0 = uc[:, 0:D_HYENA]
    x1 = uc[:, D_HYENA:2 * D_HYENA]
    vv = uc[:, 2 * D_HYENA:]
    gh = z[:, 0:D_HYENA]
    vx = (vv * x1).astype(v_refs[0].dtype)
    for p, v_ref in enumerate(v_refs):
        v_ref[0] = vx[:, p * HY_PART_CH:(p + 1) * HY_PART_CH]
    x0g_ref[0] = (x0 * (gh * jax.nn.sigmoid(gh))).astype(x0g_ref.dtype)
    o_q = O_ATT - N_HY
    za_ref[0, :, 0:D_ATTN] = (z[:, o_q:o_q + D_ATTN] * Q_SCALE_LOG2).astype(za_ref.dtype)
    ga = z[:, o_q + D_ATTN:o_q + 2 * D_ATTN]
    za_ref[0, :, D_ATTN:2 * D_ATTN] = (ga * jax.nn.sigmoid(ga)).astype(za_ref.dtype)
    kc = z[:, o_q + 2 * D_ATTN:o_q + 2 * D_ATTN + KV_DIM]
    ksw = pltpu.roll(kc, HEAD_DIM, axis=1)
    lo = lax.broadcasted_iota(jnp.int32, kc.shape, 1) < HEAD_DIM
    placed = (jnp.where(lo, kc, 0.0), jnp.where(lo, 0.0, ksw),
              jnp.where(lo, ksw, 0.0), jnp.where(lo, 0.0, kc))
    for n, kp in enumerate(placed):
        za_ref[0, :, O_KPL + n * KV_DIM:O_KPL + (n + 1) * KV_DIM] = kp.astype(za_ref.dtype)
    vc = z[:, o_q + 2 * D_ATTN + KV_DIM:]
    for j in range(tm // BLOCK):
        vt_ref[0, j] = vc[j * BLOCK:(j + 1) * BLOCK, :].T.astype(vt_ref.dtype)


def _inproj(x, pre_g, w_cat, w_short, b_short, tm=1024):
    B, L, D = x.shape
    rb = tm // HALO
    nrb = L // HALO
    grid = (B, L // tm)
    return pl.pallas_call(
        _inproj_kernel,
        grid=grid,
        in_specs=[
            pl.BlockSpec((1, HALO, D), lambda b, t: (b, jnp.maximum(t * rb - 1, 0), 0)),
            pl.BlockSpec((1, tm, D), lambda b, t: (b, t, 0)),
            pl.BlockSpec((1, HALO, D), lambda b, t: (b, jnp.minimum((t + 1) * rb, nrb - 1), 0)),
            pl.BlockSpec((1, D), lambda b, t: (0, 0)),
            pl.BlockSpec((D, D_IN), lambda b, t: (0, 0)),
            pl.BlockSpec((3, N_HY), lambda b, t: (0, 0)),
            pl.BlockSpec((1, N_HY), lambda b, t: (0, 0)),
        ],
        out_specs=(
            [pl.BlockSpec((1, tm, HY_PART_CH), lambda b, t: (b, t, 0))] * HY_PARTS
            + [pl.BlockSpec((1, tm, D_HYENA), lambda b, t: (b, t, 0)),
               pl.BlockSpec((1, tm, ZA_W), lambda b, t: (b, t, 0)),
               pl.BlockSpec((1, tm // BLOCK, KV_DIM, BLOCK), lambda b, t: (b, t, 0, 0))]),
        out_shape=(
            [jax.ShapeDtypeStruct((B, L, HY_PART_CH), jnp.bfloat16)] * HY_PARTS
            + [jax.ShapeDtypeStruct((B, L, D_HYENA), jnp.bfloat16),
               jax.ShapeDtypeStruct((B, L, ZA_W), jnp.bfloat16),
               jax.ShapeDtypeStruct((B, L // BLOCK, KV_DIM, BLOCK), jnp.bfloat16)]),
        compiler_params=pltpu.CompilerParams(
            dimension_semantics=("parallel", "arbitrary"), vmem_limit_bytes=VMEM_LIMIT),
        name="inproj",
    )(x, x, x, pre_g[None, :], w_cat, w_short, b_short[None, :])


def _alibi_slopes():
    return [float(v) for v in
            np.exp2(-8.0 * np.arange(1, N_HEADS + 1, dtype=np.float32) / N_HEADS).astype(np.float32)]


def _attn_bias_tables():
    c = np.arange(BLOCK)[:, None]
    r = np.arange(BLOCK)[None, :]
    dist = np.abs(c - r).astype(np.float32)
    slopes = np.asarray(_alibi_slopes(), np.float32)[:, None, None] * np.float32(LOG2E)
    return np.concatenate([-slopes * dist, -slopes * (WINDOW - dist)], axis=1).astype(np.float32)


def _back_kernel(sink_ref, bias_ref, mask_ref, za_ref, vt_ref, x_ref, *refs):
    yh_refs = refs[:HY_PARTS]
    x0g_ref, w_ref, g_ref, o_ref, s_ref, ot_ref, ya_ref = refs[HY_PARTS:]
    t = pl.program_id(1)
    tm = x_ref.shape[1]
    L = za_ref.shape[1]
    nb = L // BLOCK
    nbs = tm // BLOCK
    i0 = t * nbs
    o_g = D_ATTN
    f32 = jnp.float32
    bf16 = jnp.bfloat16
    slopes = _alibi_slopes()
    group = Q_PER_KV * HEAD_DIM

    c_i = lax.broadcasted_iota(jnp.int32, (BLOCK, BLOCK), 0)
    r_i = lax.broadcasted_iota(jnp.int32, (BLOCK, BLOCK), 1)
    upper = c_i >= r_i
    diag = c_i == r_i
    ones_rows = jnp.ones((16, 3 * BLOCK), bf16)

    def key_tile(j):
        return jnp.clip(j, 0, nb - 1)

    def scores(i, slot):
        start = pl.multiple_of(i * BLOCK, BLOCK)
        qblk = za_ref[0, pl.ds(start, BLOCK), 0:D_ATTN]
        krows = [pl.ds(pl.multiple_of(key_tile(i + tt - 1) * BLOCK, BLOCK), BLOCK)
                 for tt in range(3)]
        for kv in range(N_KV_HEADS):
            base = kv * group
            qs = jnp.concatenate([qblk[:, base:base + 2 * HEAD_DIM],
                                  qblk[:, base + 2 * HEAD_DIM:base + group]], axis=0)
            kp = jnp.concatenate(
                [za_ref[0, krows[tt], O_KPL + (2 * kv + half) * KV_DIM:
                        O_KPL + (2 * kv + half + 1) * KV_DIM]
                 for half in range(2) for tt in range(3)], axis=0)
            s_ref[slot, kv] = lax.dot_general(kp, qs, (((1,), (1,)), ((), ())),
                                              preferred_element_type=f32)

    def softmax_pv(i, slot, edge):
        pen1 = jnp.where(i == 0, NEG_INF, 0.0).astype(f32)
        pen3 = jnp.where(i == nb - 1, NEG_INF, 0.0).astype(f32)
        for kv in range(N_KV_HEADS):
            rows = slice(kv * HEAD_DIM, (kv + 1) * HEAD_DIM)
            vt3 = vt_ref[0, key_tile(i + 1), rows, :]
            vtw = jnp.concatenate([vt_ref[0, key_tile(i - 1), rows, :], vt_ref[0, i, rows, :], vt3],
                                  axis=1)
            pts, extras, pxs = [], [], []
            for g in range(Q_PER_KV):
                half, pair = g % 2, g // 2
                h = kv * Q_PER_KV + g
                r0 = half * 3 * BLOCK
                cols = slice(pair * BLOCK, (pair + 1) * BLOCK)
                s1 = s_ref[slot, kv, r0:r0 + BLOCK, cols]
                s2 = s_ref[slot, kv, r0 + BLOCK:r0 + 2 * BLOCK, cols]
                s3 = s_ref[slot, kv, r0 + 2 * BLOCK:r0 + 3 * BLOCK, cols]
                if edge == "lo":
                    mrg = jnp.where(upper, s1 + pen1, s3)
                elif edge == "hi":
                    mrg = jnp.where(upper, s1, s3 + pen3)
                else:
                    mrg = jnp.where(upper, s1, s3)
                a_mid = s2 + bias_ref[h, 0:BLOCK, :]
                a_mrg = mrg + bias_ref[h, BLOCK:, :]
                sink2 = sink_ref[h] * LOG2E
                s_x = (jnp.sum(jnp.where(diag, s3, 0.0), axis=0, keepdims=True)
                       - slopes[h] * LOG2E * WINDOW)
                if edge == "hi":
                    s_x = s_x + pen3
                m = jnp.max(jnp.maximum(a_mid, a_mrg), axis=0, keepdims=True)
                m = jnp.maximum(jnp.maximum(m, sink2), s_x)
                p_mid = jnp.exp2(a_mid - m).astype(bf16)
                p_mrg = jnp.exp2(a_mrg - m).astype(bf16)
                p_x = jnp.exp2(s_x - m)
                pxs.append(p_x)
                extras.append(jnp.exp2(sink2 - m) + p_x)
                p1 = p_mrg * mask_ref[...]
                p3 = p_mrg - p1
                pts.append(jnp.concatenate([p1, p_mid, p3], axis=0))
                if len(pts) < PV_HEADS:
                    continue
                pt = jnp.concatenate(pts, axis=1)
                ot = jnp.dot(jnp.concatenate([vtw, ones_rows], axis=0), pt,
                             preferred_element_type=f32)
                for n in range(PV_HEADS):
                    hh = h - (PV_HEADS - 1) + n
                    cols = slice(n * BLOCK, (n + 1) * BLOCK)
                    o = ot[0:HEAD_DIM, cols] + vt3.astype(f32) * pxs[n]
                    inv = 1.0 / (ot[HEAD_DIM:HEAD_DIM + 1, cols] + extras[n])
                    ot_ref[slot, hh * HEAD_DIM:(hh + 1) * HEAD_DIM, :] = o * inv
                pts, extras, pxs = [], [], []

    def finish(i, slot):
        start = pl.multiple_of(i * BLOCK, BLOCK)
        local = pl.multiple_of((i - i0) * BLOCK, BLOCK)
        gate = za_ref[0, pl.ds(start, BLOCK), o_g:o_g + D_ATTN].astype(f32)
        ya_ref[pl.ds(local, BLOCK), :] = (ot_ref[slot].T * gate).astype(ya_ref.dtype)

    def two_blocks(j):
        i = i0 + 2 * j + 1
        scores(i + 1, 0)
        softmax_pv(i, 1, None)
        finish(i - 1, 0)
        scores(i + 2, 1)
        softmax_pv(i + 1, 0, None)
        finish(i, 1)

    def project(c):
        r = slice(2 * c * BLOCK, 2 * (c + 1) * BLOCK)
        x0g = x0g_ref[0, r, :].astype(f32)
        yh = jnp.concatenate([ref[0, r, :] for ref in yh_refs], axis=1).astype(f32)
        y = jnp.dot(jnp.concatenate([(yh * x0g).astype(bf16), ya_ref[r, :]], axis=1), w_ref[...],
                    preferred_element_type=f32)
        ms = jnp.mean(y * y, axis=-1, keepdims=True)
        o_ref[0, r, :] = x_ref[0, r, :] + y * lax.rsqrt(ms + RMS_EPS) * g_ref[...]

    scores(i0, 0)
    scores(i0 + 1, 1)
    softmax_pv(i0, 0, "lo")
    npairs = nbs // 2
    for j in range(npairs - 1):
        two_blocks(j)
        if j >= 1:
            project(j - 1)
    softmax_pv(i0 + nbs - 1, 1, "hi")
    finish(i0 + nbs - 2, 0)
    finish(i0 + nbs - 1, 1)
    for c in range(max(npairs - 2, 0), npairs):
        project(c)


def _back(za, vt, sink, x, yh_parts, x0g, w_out, post_g, tm=1024):
    B, L, D = x.shape
    nb = L // BLOCK
    nbs = tm // BLOCK
    assert WINDOW == BLOCK and nbs >= 4 and nbs % 2 == 0 and L % tm == 0
    bias = jnp.asarray(_attn_bias_tables())
    upper = jnp.asarray(np.triu(np.ones((BLOCK, BLOCK), np.float32)).T, jnp.bfloat16)
    return pl.pallas_call(
        _back_kernel,
        grid=(B, L // tm),
        in_specs=[
            pl.BlockSpec(memory_space=pltpu.SMEM),
            pl.BlockSpec((N_HEADS, 2 * BLOCK, BLOCK), lambda b, t: (0, 0, 0)),
            pl.BlockSpec((BLOCK, BLOCK), lambda b, t: (0, 0)),
            pl.BlockSpec((1, L, ZA_W), lambda b, t: (b, 0, 0)),
            pl.BlockSpec((1, nb, KV_DIM, BLOCK), lambda b, t: (b, 0, 0, 0)),
            pl.BlockSpec((1, tm, D), lambda b, t: (b, t, 0)),
        ] + [pl.BlockSpec((1, tm, HY_PART_CH), lambda b, t: (b, t, 0))] * HY_PARTS + [
            pl.BlockSpec((1, tm, D_HYENA), lambda b, t: (b, t, 0)),
            pl.BlockSpec((D_HYENA + D_ATTN, D), lambda b, t: (0, 0)),
            pl.BlockSpec((1, D), lambda b, t: (0, 0)),
        ],
        out_specs=pl.BlockSpec((1, tm, D), lambda b, t: (b, t, 0)),
        out_shape=jax.ShapeDtypeStruct((B, L, D), jnp.float32),
        scratch_shapes=[
            pltpu.VMEM((2, N_KV_HEADS, 6 * BLOCK, 2 * BLOCK), jnp.float32),
            pltpu.VMEM((2, D_ATTN, BLOCK), jnp.float32),
            pltpu.VMEM((tm, D_ATTN), jnp.bfloat16),
        ],
        compiler_params=pltpu.CompilerParams(
            dimension_semantics=("parallel", "parallel"), vmem_limit_bytes=VMEM_LIMIT),
        name="attn_outproj",
    )(sink, bias, upper, za, vt, x, *yh_parts, x0g, w_out, post_g[None, :])


def _hyena_kernel(k_ref, d_ref, v_ref, y_ref, *scratch):
    tables = scratch[:HY_UNROLL]
    accs = scratch[HY_UNROLL:]
    step = pl.program_id(0)
    cg = v_ref.shape[0]
    rows = v_ref.shape[1]
    nchan, n2 = k_ref.shape
    nblk = n2 // (2 * TBLK)
    bsz = rows // nblk
    off = TBLK * (nblk - 1) + SHIFT_ROWS

    def build_table(chan, s_ref):
        krow = k_ref[pl.ds(jnp.minimum(chan, nchan - 1), 1), :]
        kb = jnp.broadcast_to(krow, (SHIFT_ROWS, n2))
        s_ref[...] = pltpu.roll(kb, off, axis=1, stride=1, stride_axis=0).astype(s_ref.dtype)

    def convolve(ci, s_ref, acc_ref):
        acc_ref[...] = d_ref[pl.ds(ci, 1), :] * v_ref[ci].astype(jnp.float32)
        for d in range(-(nblk - 1), nblk):
            x0 = TBLK * (d + nblk - 1)
            w = jnp.concatenate(
                [s_ref[:, x0 + SHIFT_ROWS:x0 + SHIFT_ROWS + TBLK], s_ref[:, x0:x0 + TBLK]], axis=0)
            n = (nblk - abs(d)) * bsz
            src = 0 if d >= 0 else -d * bsz
            dst = d * bsz if d >= 0 else 0
            acc_ref[dst:dst + n, :] += jnp.dot(v_ref[ci, pl.ds(src, n), :], w,
                                               preferred_element_type=jnp.float32)
        y_ref[ci] = acc_ref[...].astype(y_ref.dtype)

    @pl.when(step == 0)
    def _():
        for u in range(HY_UNROLL):
            build_table(u, tables[u])

    def body(it, carry):
        for u in range(HY_UNROLL):
            convolve(it * HY_UNROLL + u, tables[u], accs[u])
        for u in range(HY_UNROLL):
            build_table(step * cg + (it + 1) * HY_UNROLL + u, tables[u])
        return carry

    lax.fori_loop(0, cg // HY_UNROLL, body, 0)


def _hyena_conv(kt, hyena_d, v_rows, part, cg=16):
    C, rows, _ = v_rows.shape
    n2 = kt.shape[1]
    steps = C // cg
    return pl.pallas_call(
        _hyena_kernel,
        grid=(steps,),
        in_specs=[
            pl.BlockSpec((C, n2), lambda c: (part, 0)),
            pl.BlockSpec((cg, 1), lambda c: (c + part * steps, 0)),
            pl.BlockSpec((cg, rows, TBLK), lambda c: (c, 0, 0)),
        ],
        out_specs=pl.BlockSpec((cg, rows, TBLK), lambda c: (c, 0, 0)),
        out_shape=jax.ShapeDtypeStruct((C, rows, TBLK), jnp.bfloat16),
        scratch_shapes=(
            [pltpu.VMEM((SHIFT_ROWS, n2), jnp.bfloat16)] * HY_UNROLL
            + [pltpu.VMEM((rows, TBLK), jnp.float32)] * HY_UNROLL),
        compiler_params=pltpu.CompilerParams(
            dimension_semantics=("arbitrary",), vmem_limit_bytes=VMEM_LIMIT),
        name="hyena_conv",
    )(kt, hyena_d[:, None], v_rows)


def _layer(x, pre_g, w_in, w_short, b_short, w_f1, b_f1, w_f2, b_f2, w_f3, b_f3, w_f4,
           sin_freq, hyena_d, attn_sink, w_out, post_g):
    B, L, _ = x.shape
    nblk = L // TBLK
    o_q = 4 * D_HYENA
    o_k = o_q + D_ATTN
    o_v = o_k + KV_DIM
    o_ag = o_v + KV_DIM
    w_cat = jnp.concatenate(
        [w_in[:, :o_q], w_in[:, o_q:o_k], w_in[:, o_ag:], w_in[:, o_k:o_v], w_in[:, o_v:o_ag]],
        axis=1).astype(jnp.bfloat16)

    kt = _hyena_filter(L, w_f1, b_f1, w_f2, b_f2, w_f3, b_f3, w_f4, sin_freq)
    *v_parts, x0g, za, vt = _inproj(x, pre_g, w_cat, w_short, b_short)
    pc = HY_PART_CH
    yh = []
    for part, v in enumerate(v_parts):
        v_rows = v.reshape(B, nblk, TBLK, pc).transpose(3, 1, 0, 2).reshape(pc, nblk * B, TBLK)
        y_rows = _hyena_conv(kt, hyena_d, v_rows, part)
        yh.append(y_rows.reshape(pc, nblk, B, TBLK).transpose(2, 1, 3, 0).reshape(B, L, pc))
    return _back(za, vt, attn_sink, x, yh, x0g, w_out.astype(jnp.bfloat16), post_g)


def kernel(x, pre_g, w_in, w_short, b_short, w_f1, b_f1, w_f2, b_f2, w_f3, b_f3, w_f4, sin_freq, hyena_d, attn_sink, w_out, post_g):
    depth = pre_g.shape[0]
    for l in range(depth):
        x = _layer(x, pre_g[l], w_in[l], w_short[l], b_short[l], w_f1[l], b_f1[l], w_f2[l],
                   b_f2[l], w_f3[l], b_f3[l], w_f4[l], sin_freq[l], hyena_d[l], attn_sink[l],
                   w_out[l], post_g[l])
    return x
```

```python
import math

import jax
import jax.numpy as jnp
import numpy as np
from jax import lax
from jax.experimental import pallas as pl
from jax.experimental.pallas import tpu as pltpu

D_MODEL = 1024
D_HYENA = 512
D_ATTN = 512
N_HEADS = 8
HEAD_DIM = 64
N_KV_HEADS = 2
Q_PER_KV = N_HEADS // N_KV_HEADS
KV_DIM = N_KV_HEADS * HEAD_DIM
WINDOW = 128
BLOCK = 128
FILTER_ORDER = 64
N_BANDS = 16
POS_EMB_DIM = 1 + 2 * N_BANDS
POS_EMB_PAD = 40
DECAY_TARGET = 1e-2
FAST_DECAY_PCT = 0.3
SLOW_DECAY_PCT = 1.5
RMS_EPS = 1e-6
NEG_INF = -1e30
PV_HEADS = 2
LOG2E = math.log2(math.e)
Q_SCALE_LOG2 = HEAD_DIM ** -0.5 * LOG2E

N_HY = 3 * D_HYENA
O_GH = N_HY
O_ATT = O_GH + D_HYENA
N_ATT = 2 * D_ATTN + 2 * KV_DIM
D_IN = O_ATT + N_ATT
O_KPL = 2 * D_ATTN
ZA_W = O_KPL + 4 * KV_DIM

LANES = 128
HALO = 16

TBLK = 256
SHIFT_ROWS = 128
HY_UNROLL = 4
HY_PARTS = 4
HY_PART_CH = D_HYENA // HY_PARTS

VMEM_LIMIT = 56 * 1024 * 1024

_HI = lax.Precision.HIGHEST


def _filter_consts(L):
    t = np.arange(L, dtype=np.float32)
    t_norm = t / np.float32(max(L - 1, 1))
    w = np.float32(2.0 * math.pi) * t / np.float32(L)
    bands = np.linspace(1e-4, N_BANDS - 1, N_BANDS).astype(np.float32)
    ang = w[:, None] * bands[None, :]
    z = np.concatenate([t_norm[:, None], np.cos(ang), -np.sin(ang)], axis=-1)
    min_decay = math.log(DECAY_TARGET) / SLOW_DECAY_PCT
    max_decay = math.log(DECAY_TARGET) / FAST_DECAY_PCT
    deltas = np.abs(np.linspace(min_decay, max_decay, D_HYENA)).astype(np.float32)
    decay = np.exp(-t_norm[:, None] * deltas[None, :]).astype(np.float32)
    pad = np.zeros((POS_EMB_PAD - POS_EMB_DIM, L), np.float32)
    zt = np.concatenate([z.T, pad], axis=0)
    flip = np.eye(LANES, dtype=np.float32)[::-1]
    return zt, np.ascontiguousarray(decay.T), np.ascontiguousarray(flip)


def _filter_kernel(zt_ref, dec_ref, flip_ref, p_ref, w4_ref, k_ref):
    L = zt_ref.shape[1]
    f32 = jnp.float32
    w1 = p_ref[:, 0:POS_EMB_PAD]
    w2 = p_ref[:, LANES:LANES + FILTER_ORDER]
    w3 = p_ref[:, 2 * LANES:2 * LANES + FILTER_ORDER]
    vec = p_ref[:, 3 * LANES:4 * LANES]
    w4 = w4_ref[...].T
    h = jnp.dot(w1, zt_ref[...], precision=_HI, preferred_element_type=f32)
    h = jnp.sin(vec[:, 3:4] * (h + vec[:, 0:1]))
    h = jnp.dot(w2, h, precision=_HI, preferred_element_type=f32)
    h = jnp.sin(vec[:, 4:5] * (h + vec[:, 1:2]))
    h = jnp.dot(w3, h, precision=_HI, preferred_element_type=f32)
    h = jnp.sin(vec[:, 5:6] * (h + vec[:, 2:3]))
    hf = jnp.dot(w4[0:D_HYENA, :], h, precision=_HI, preferred_element_type=f32) * dec_ref[...]
    hb = jnp.dot(w4[D_HYENA:, :], h, precision=_HI, preferred_element_type=f32) * dec_ref[...]
    nt = L // LANES
    rev = jnp.concatenate(
        [jnp.dot(hb[:, (nt - 1 - j) * LANES:(nt - j) * LANES], flip_ref[...], precision=_HI,
                 preferred_element_type=f32) for j in range(nt)], axis=1)
    col = lax.broadcasted_iota(jnp.int32, rev.shape, 1)
    hb2 = jnp.where(col == 0, 0.0, pltpu.roll(rev, 1, axis=1))
    ss = jnp.sum(hf * hf, axis=1, keepdims=True) + jnp.sum(hb2 * hb2, axis=1, keepdims=True)
    inv = lax.rsqrt(ss + 1e-12)
    k_ref[:, 0:L] = hf * inv
    k_ref[:, L:2 * L] = hb2 * inv


def _hyena_filter(L, w_f1, b_f1, w_f2, b_f2, w_f3, b_f3, w_f4, sin_freq):
    zt, dec, flip = _filter_consts(L)
    def tile(a):
        return jnp.pad(a, ((0, 0), (0, LANES - a.shape[1])))

    vectors = jnp.stack([b_f1, b_f2, b_f3, sin_freq[0], sin_freq[1], sin_freq[2]], axis=1)
    params = jnp.concatenate([tile(w_f1.T), tile(w_f2.T), tile(w_f3.T), tile(vectors)], axis=1)
    args = (jnp.asarray(zt), jnp.asarray(dec), jnp.asarray(flip), params, w_f4)
    return pl.pallas_call(
        _filter_kernel,
        out_shape=jax.ShapeDtypeStruct((D_HYENA, 2 * L), jnp.float32),
        compiler_params=pltpu.CompilerParams(vmem_limit_bytes=VMEM_LIMIT),
        name="hyena_filter",
    )(*args)


def _inproj_kernel(xp_ref, x_ref, xn_ref, g_ref, w_ref, wsh_ref, bsh_ref, *out_refs):
    v_refs = out_refs[:HY_PARTS]
    x0g_ref, za_ref, vt_ref = out_refs[HY_PARTS:]
    t = pl.program_id(1)
    nt = pl.num_programs(1)
    tm = x_ref.shape[1]

    def norm(xv):
        ms = jnp.mean(xv * xv, axis=-1, keepdims=True)
        return (xv * lax.rsqrt(ms + RMS_EPS) * g_ref[...]).astype(jnp.bfloat16)

    h = norm(x_ref[0])
    hp = jnp.where(t > 0, norm(xp_ref[0]), 0.0).astype(jnp.bfloat16)
    hn = jnp.where(t < nt - 1, norm(xn_ref[0]), 0.0).astype(jnp.bfloat16)
    hall = jnp.concatenate([hp, h, hn], axis=0)
    zh = jnp.dot(hall, w_ref[:, 0:N_HY], preferred_element_type=jnp.float32)
    z = jnp.dot(h, w_ref[:, N_HY:], preferred_element_type=jnp.float32)

    rows = tm + 2 * HALO
    u_prev = pltpu.roll(zh, 1, axis=0)[HALO:HALO + tm]
    u = zh[HALO:HALO + tm]
    u_next = pltpu.roll(zh, rows - 1, axis=0)[HALO:HALO + tm]
    uc = (u_prev * wsh_ref[0:1, :] + u * wsh_ref[1:2, :] + u_next * wsh_ref[2:3, :]
          + bsh_ref[...])
    x0 = uc[:, 0:D_HYENA]
    x1 = uc[:, D_HYENA:2 * D_HYENA]
    vv = uc[:, 2 * D_HYENA:]
    gh = z[:, 0:D_HYENA]
    vx = (vv * x1).astype(v_refs[0].dtype)
    for p, v_ref in enumerate(v_refs):
        v_ref[0] = vx[:, p * HY_PART_CH:(p + 1) * HY_PART_CH]
    x0g_ref[0] = (x0 * (gh * jax.nn.sigmoid(gh))).astype(x0g_ref.dtype)
    o_q = O_ATT - N_HY
    za_ref[0, :, 0:D_ATTN] = (z[:, o_q:o_q + D_ATTN] * Q_SCALE_LOG2).astype(za_ref.dtype)
    o_k = o_q + D_ATTN
    o_v = o_k + KV_DIM
    o_ga = o_v + KV_DIM
    ga = z[:, o_ga:o_ga + D_ATTN]
    za_ref[0, :, D_ATTN:2 * D_ATTN] = (ga * jax.nn.sigmoid(ga)).astype(za_ref.dtype)
    kc = z[:, o_k:o_k + KV_DIM]
    ksw = pltpu.roll(kc, HEAD_DIM, axis=1)
    lo = lax.broadcasted_iota(jnp.int32, kc.shape, 1) < HEAD_DIM
    placed = (jnp.where(lo, kc, 0.0), jnp.where(lo, 0.0, ksw),
              jnp.where(lo, ksw, 0.0), jnp.where(lo, 0.0, kc))
    for n, kp in enumerate(placed):
        za_ref[0, :, O_KPL + n * KV_DIM:O_KPL + (n + 1) * KV_DIM] = kp.astype(za_ref.dtype)
    vc = z[:, o_v:o_v + KV_DIM]
    for j in range(tm // BLOCK):
        vt_ref[0, j] = vc[j * BLOCK:(j + 1) * BLOCK, :].T.astype(vt_ref.dtype)


def _inproj(x, pre_g, w_cat, w_short, b_short, tm=1024):
    B, L, D = x.shape
    rb = tm // HALO
    nrb = L // HALO
    grid = (B, L // tm)
    return pl.pallas_call(
        _inproj_kernel,
        grid=grid,
        in_specs=[
            pl.BlockSpec((1, HALO, D), lambda b, t: (b, jnp.maximum(t * rb - 1, 0), 0)),
            pl.BlockSpec((1, tm, D), lambda b, t: (b, t, 0)),
            pl.BlockSpec((1, HALO, D), lambda b, t: (b, jnp.minimum((t + 1) * rb, nrb - 1), 0)),
            pl.BlockSpec((1, D), lambda b, t: (0, 0)),
            pl.BlockSpec((D, D_IN), lambda b, t: (0, 0)),
            pl.BlockSpec((3, N_HY), lambda b, t: (0, 0)),
            pl.BlockSpec((1, N_HY), lambda b, t: (0, 0)),
        ],
        out_specs=(
            [pl.BlockSpec((1, tm, HY_PART_CH), lambda b, t: (b, t, 0))] * HY_PARTS
            + [pl.BlockSpec((1, tm, D_HYENA), lambda b, t: (b, t, 0)),
               pl.BlockSpec((1, tm, ZA_W), lambda b, t: (b, t, 0)),
               pl.BlockSpec((1, tm // BLOCK, KV_DIM, BLOCK), lambda b, t: (b, t, 0, 0))]),
        out_shape=(
            [jax.ShapeDtypeStruct((B, L, HY_PART_CH), jnp.bfloat16)] * HY_PARTS
            + [jax.ShapeDtypeStruct((B, L, D_HYENA), jnp.bfloat16),
               jax.ShapeDtypeStruct((B, L, ZA_W), jnp.bfloat16),
               jax.ShapeDtypeStruct((B, L // BLOCK, KV_DIM, BLOCK), jnp.bfloat16)]),
        compiler_params=pltpu.CompilerParams(
            dimension_semantics=("parallel", "arbitrary"), vmem_limit_bytes=VMEM_LIMIT),
        name="inproj",
    )(x, x, x, pre_g[None, :], w_cat, w_short, b_short[None, :])


def _alibi_slopes():
    return [float(v) for v in
            np.exp2(-8.0 * np.arange(1, N_HEADS + 1, dtype=np.float32) / N_HEADS).astype(np.float32)]


def _attn_bias_tables():
    c = np.arange(BLOCK)[:, None]
    r = np.arange(BLOCK)[None, :]
    dist = np.abs(c - r).astype(np.float32)
    slopes = np.asarray(_alibi_slopes(), np.float32)[:, None, None] * np.float32(LOG2E)
    return np.concatenate([-slopes * dist, -slopes * (WINDOW - dist)], axis=1).astype(np.float32)


def _back_kernel(sink_ref, bias_ref, mask_ref, za_ref, vt_ref, x_ref, *refs):
    yh_refs = refs[:HY_PARTS]
    x0g_ref, w_ref, g_ref, o_ref, s_ref, ot_ref, ya_ref = refs[HY_PARTS:]
    t = pl.program_id(1)
    tm = x_ref.shape[1]
    L = za_ref.shape[1]
    nb = L // BLOCK
    nbs = tm // BLOCK
    i0 = t * nbs
    o_g = D_ATTN
    f32 = jnp.float32
    bf16 = jnp.bfloat16
    slopes = _alibi_slopes()
    group = Q_PER_KV * HEAD_DIM

    c_i = lax.broadcasted_iota(jnp.int32, (BLOCK, BLOCK), 0)
    r_i = lax.broadcasted_iota(jnp.int32, (BLOCK, BLOCK), 1)
    upper = c_i >= r_i
    diag = c_i == r_i
    ones_rows = jnp.ones((16, 3 * BLOCK), bf16)

    def key_tile(j):
        return jnp.clip(j, 0, nb - 1)

    def scores(i, slot):
        start = pl.multiple_of(i * BLOCK, BLOCK)
        qblk = za_ref[0, pl.ds(start, BLOCK), 0:D_ATTN]
        krows = [pl.ds(pl.multiple_of(key_tile(i + tt - 1) * BLOCK, BLOCK), BLOCK)
                 for tt in range(3)]
        for kv in range(N_KV_HEADS):
            base = kv * group
            qs = jnp.concatenate([qblk[:, base:base + 2 * HEAD_DIM],
                                  qblk[:, base + 2 * HEAD_DIM:base + group]], axis=0)
            kp = jnp.concatenate(
                [za_ref[0, krows[tt], O_KPL + (2 * kv + half) * KV_DIM:
                        O_KPL + (2 * kv + half + 1) * KV_DIM]
                 for half in range(2) for tt in range(3)], axis=0)
            s_ref[slot, kv] = lax.dot_general(kp, qs, (((1,), (1,)), ((), ())),
                                              preferred_element_type=f32)

    def softmax_pv(i, slot, edge):
        pen1 = jnp.where(i == 0, NEG_INF, 0.0).astype(f32)
        pen3 = jnp.where(i == nb - 1, NEG_INF, 0.0).astype(f32)
        for kv in range(N_KV_HEADS):
            rows = slice(kv * HEAD_DIM, (kv + 1) * HEAD_DIM)
            vt3 = vt_ref[0, key_tile(i + 1), rows, :]
            vtw = jnp.concatenate([vt_ref[0, key_tile(i - 1), rows, :], vt_ref[0, i, rows, :], vt3],
                                  axis=1)
            pts, extras, pxs = [], [], []
            for g in range(Q_PER_KV):
                half, pair = g % 2, g // 2
                h = kv * Q_PER_KV + g
                r0 = half * 3 * BLOCK
                cols = slice(pair * BLOCK, (pair + 1) * BLOCK)
                s1 = s_ref[slot, kv, r0:r0 + BLOCK, cols]
                s2 = s_ref[slot, kv, r0 + BLOCK:r0 + 2 * BLOCK, cols]
                s3 = s_ref[slot, kv, r0 + 2 * BLOCK:r0 + 3 * BLOCK, cols]
                if edge == "lo":
                    mrg = jnp.where(upper, s1 + pen1, s3)
                elif edge == "hi":
                    mrg = jnp.where(upper, s1, s3 + pen3)
                else:
                    mrg = jnp.where(upper, s1, s3)
                a_mid = s2 + bias_ref[h, 0:BLOCK, :]
                a_mrg = mrg + bias_ref[h, BLOCK:, :]
                sink2 = sink_ref[h] * LOG2E
                s_x = (jnp.sum(jnp.where(diag, s3, 0.0), axis=0, keepdims=True)
                       - slopes[h] * LOG2E * WINDOW)
                if edge == "hi":
                    s_x = s_x + pen3
                m = jnp.max(jnp.maximum(a_mid, a_mrg), axis=0, keepdims=True)
                m = jnp.maximum(jnp.maximum(m, sink2), s_x)
                p_mid = jnp.exp2(a_mid - m).astype(bf16)
                p_mrg = jnp.exp2(a_mrg - m).astype(bf16)
                p_x = jnp.exp2(s_x - m)
                pxs.append(p_x)
                extras.append(jnp.exp2(sink2 - m) + p_x)
                p1 = p_mrg * mask_ref[...]
                p3 = p_mrg - p1
                pts.append(jnp.concatenate([p1, p_mid, p3], axis=0))
                if len(pts) < PV_HEADS:
                    continue
                pt = jnp.concatenate(pts, axis=1)
                ot = jnp.dot(jnp.concatenate([vtw, ones_rows], axis=0), pt,
                             preferred_element_type=f32)
                for n in range(PV_HEADS):
                    hh = h - (PV_HEADS - 1) + n
                    cols = slice(n * BLOCK, (n + 1) * BLOCK)
                    o = ot[0:HEAD_DIM, cols] + vt3.astype(f32) * pxs[n]
                    inv = 1.0 / (ot[HEAD_DIM:HEAD_DIM + 1, cols] + extras[n])
                    ot_ref[slot, hh * HEAD_DIM:(hh + 1) * HEAD_DIM, :] = o * inv
                pts, extras, pxs = [], [], []

    def finish(i, slot):
        start = pl.multiple_of(i * BLOCK, BLOCK)
        local = pl.multiple_of((i - i0) * BLOCK, BLOCK)
        gate = za_ref[0, pl.ds(start, BLOCK), o_g:o_g + D_ATTN].astype(f32)
        ya_ref[pl.ds(local, BLOCK), :] = (ot_ref[slot].T * gate).astype(ya_ref.dtype)

    def two_blocks(j):
        i = i0 + 2 * j + 1
        scores(i + 1, 0)
        softmax_pv(i, 1, None)
        finish(i - 1, 0)
        scores(i + 2, 1)
        softmax_pv(i + 1, 0, None)
        finish(i, 1)

    def project(c):
        r = slice(2 * c * BLOCK, 2 * (c + 1) * BLOCK)
        x0g = x0g_ref[0, r, :].astype(f32)
        yh = jnp.concatenate([ref[0, r, :] for ref in yh_refs], axis=1).astype(f32)
        y = jnp.dot(jnp.concatenate([(yh * x0g).astype(bf16), ya_ref[r, :]], axis=1), w_ref[...],
                    preferred_element_type=f32)
        ms = jnp.mean(y * y, axis=-1, keepdims=True)
        o_ref[0, r, :] = x_ref[0, r, :] + y * lax.rsqrt(ms + RMS_EPS) * g_ref[...]

    scores(i0, 0)
    scores(i0 + 1, 1)
    softmax_pv(i0, 0, "lo")
    npairs = nbs // 2
    for j in range(npairs - 1):
        two_blocks(j)
        if j >= 1:
            project(j - 1)
    softmax_pv(i0 + nbs - 1, 1, "hi")
    finish(i0 + nbs - 2, 0)
    finish(i0 + nbs - 1, 1)
    for c in range(max(npairs - 2, 0), npairs):
        project(c)


def _back(za, vt, sink, x, yh_parts, x0g, w_out, post_g, tm=1024):
    B, L, D = x.shape
    nb = L // BLOCK
    nbs = tm // BLOCK
    assert WINDOW == BLOCK and nbs >= 4 and nbs % 2 == 0 and L % tm == 0
    bias = jnp.asarray(_attn_bias_tables())
    upper = jnp.asarray(np.triu(np.ones((BLOCK, BLOCK), np.float32)).T, jnp.bfloat16)
    return pl.pallas_call(
        _back_kernel,
        grid=(B, L // tm),
        in_specs=[
            pl.BlockSpec(memory_space=pltpu.SMEM),
            pl.BlockSpec((N_HEADS, 2 * BLOCK, BLOCK), lambda b, t: (0, 0, 0)),
            pl.BlockSpec((BLOCK, BLOCK), lambda b, t: (0, 0)),
            pl.BlockSpec((1, L, ZA_W), lambda b, t: (b, 0, 0)),
            pl.BlockSpec((1, nb, KV_DIM, BLOCK), lambda b, t: (b, 0, 0, 0)),
            pl.BlockSpec((1, tm, D), lambda b, t: (b, t, 0)),
        ] + [pl.BlockSpec((1, tm, HY_PART_CH), lambda b, t: (b, t, 0))] * HY_PARTS + [
            pl.BlockSpec((1, tm, D_HYENA), lambda b, t: (b, t, 0)),
            pl.BlockSpec((D_HYENA + D_ATTN, D), lambda b, t: (0, 0)),
            pl.BlockSpec((1, D), lambda b, t: (0, 0)),
        ],
        out_specs=pl.BlockSpec((1, tm, D), lambda b, t: (b, t, 0)),
        out_shape=jax.ShapeDtypeStruct((B, L, D), jnp.float32),
        scratch_shapes=[
            pltpu.VMEM((2, N_KV_HEADS, 6 * BLOCK, 2 * BLOCK), jnp.float32),
            pltpu.VMEM((2, D_ATTN, BLOCK), jnp.float32),
            pltpu.VMEM((tm, D_ATTN), jnp.bfloat16),
        ],
        compiler_params=pltpu.CompilerParams(
            dimension_semantics=("parallel", "parallel"), vmem_limit_bytes=VMEM_LIMIT),
        name="attn_outproj",
    )(sink, bias, upper, za, vt, x, *yh_parts, x0g, w_out, post_g[None, :])


def _hyena_kernel(k_ref, d_ref, v_ref, y_ref, *scratch):
    tables = scratch[:HY_UNROLL]
    accs = scratch[HY_UNROLL:]
    step = pl.program_id(0)
    cg = v_ref.shape[0]
    rows = v_ref.shape[1]
    nchan, n2 = k_ref.shape
    nblk = n2 // (2 * TBLK)
    bsz = rows // nblk
    off = TBLK * (nblk - 1) + SHIFT_ROWS

    def build_table(chan, s_ref):
        krow = k_ref[pl.ds(jnp.minimum(chan, nchan - 1), 1), :]
        kb = jnp.broadcast_to(krow, (SHIFT_ROWS, n2))
        s_ref[...] = pltpu.roll(kb, off, axis=1, stride=1, stride_axis=0).astype(s_ref.dtype)

    def convolve(ci, s_ref, acc_ref):
        acc_ref[...] = d_ref[pl.ds(ci, 1), :] * v_ref[ci].astype(jnp.float32)
        for d in range(-(nblk - 1), nblk):
            x0 = TBLK * (d + nblk - 1)
            w = jnp.concatenate(
                [s_ref[:, x0 + SHIFT_ROWS:x0 + SHIFT_ROWS + TBLK], s_ref[:, x0:x0 + TBLK]], axis=0)
            n = (nblk - abs(d)) * bsz
            src = 0 if d >= 0 else -d * bsz
            dst = d * bsz if d >= 0 else 0
            acc_ref[dst:dst + n, :] += jnp.dot(v_ref[ci, pl.ds(src, n), :], w,
                                               preferred_element_type=jnp.float32)
        y_ref[ci] = acc_ref[...].astype(y_ref.dtype)

    @pl.when(step == 0)
    def _():
        for u in range(HY_UNROLL):
            build_table(u, tables[u])

    def body(it, carry):
        for u in range(HY_UNROLL):
            convolve(it * HY_UNROLL + u, tables[u], accs[u])
        for u in range(HY_UNROLL):
            build_table(step * cg + (it + 1) * HY_UNROLL + u, tables[u])
        return carry

    lax.fori_loop(0, cg // HY_UNROLL, body, 0)


def _hyena_conv(kt, hyena_d, v_rows, part, cg=32):
    C, rows, _ = v_rows.shape
    n2 = kt.shape[1]
    steps = C // cg
    return pl.pallas_call(
        _hyena_kernel,
        grid=(steps,),
        in_specs=[
            pl.BlockSpec((C, n2), lambda c: (part, 0)),
            pl.BlockSpec((cg, 1), lambda c: (c + part * steps, 0)),
            pl.BlockSpec((cg, rows, TBLK), lambda c: (c, 0, 0)),
        ],
        out_specs=pl.BlockSpec((cg, rows, TBLK), lambda c: (c, 0, 0)),
        out_shape=jax.ShapeDtypeStruct((C, rows, TBLK), jnp.bfloat16),
        scratch_shapes=(
            [pltpu.VMEM((SHIFT_ROWS, n2), jnp.bfloat16)] * HY_UNROLL
            + [pltpu.VMEM((rows, TBLK), jnp.float32)] * HY_UNROLL),
        compiler_params=pltpu.CompilerParams(
            dimension_semantics=("arbitrary",), vmem_limit_bytes=VMEM_LIMIT),
        name="hyena_conv",
    )(kt, hyena_d[:, None], v_rows)


def _layer(x, pre_g, w_in, w_short, b_short, w_f1, b_f1, w_f2, b_f2, w_f3, b_f3, w_f4,
           sin_freq, hyena_d, attn_sink, w_out, post_g):
    B, L, _ = x.shape
    nblk = L // TBLK
    kt = _hyena_filter(L, w_f1, b_f1, w_f2, b_f2, w_f3, b_f3, w_f4, sin_freq)
    *v_parts, x0g, za, vt = _inproj(x, pre_g, w_in.astype(jnp.bfloat16), w_short, b_short)
    pc = HY_PART_CH
    yh = []
    for part, v in enumerate(v_parts):
        v_rows = v.reshape(B, nblk, TBLK, pc).transpose(3, 1, 0, 2).reshape(pc, nblk * B, TBLK)
        y_rows = _hyena_conv(kt, hyena_d, v_rows, part)
        yh.append(y_rows.reshape(pc, nblk, B, TBLK).transpose(2, 1, 3, 0).reshape(B, L, pc))
    return _back(za, vt, attn_sink, x, yh, x0g, w_out.astype(jnp.bfloat16), post_g)


def kernel(x, pre_g, w_in, w_short, b_short, w_f1, b_f1, w_f2, b_f2, w_f3, b_f3, w_f4, sin_freq, hyena_d, attn_sink, w_out, post_g):
    depth = pre_g.shape[0]
    for l in range(depth):
        x = _layer(x, pre_g[l], w_in[l], w_short[l], b_short[l], w_f1[l], b_f1[l], w_f2[l],
                   b_f2[l], w_f3[l], b_f3[l], w_f4[l], sin_freq[l], hyena_d[l], attn_sink[l],
                   w_out[l], post_g[l])
    return x
```

```python
import math

import jax
import jax.numpy as jnp
import numpy as np
from jax import lax
from jax.experimental import pallas as pl
from jax.experimental.pallas import tpu as pltpu

D_MODEL = 1024
D_HYENA = 512
D_ATTN = 512
N_HEADS = 8
HEAD_DIM = 64
N_KV_HEADS = 2
Q_PER_KV = N_HEADS // N_KV_HEADS
KV_DIM = N_KV_HEADS * HEAD_DIM
WINDOW = 128
BLOCK = 128
FILTER_ORDER = 64
N_BANDS = 16
POS_EMB_DIM = 1 + 2 * N_BANDS
POS_EMB_PAD = 40
DECAY_TARGET = 1e-2
FAST_DECAY_PCT = 0.3
SLOW_DECAY_PCT = 1.5
RMS_EPS = 1e-6
NEG_INF = -1e30
PV_HEADS = 2
LOG2E = math.log2(math.e)
Q_SCALE_LOG2 = HEAD_DIM ** -0.5 * LOG2E

N_HY = 3 * D_HYENA
O_GH = N_HY
O_ATT = O_GH + D_HYENA
N_ATT = 2 * D_ATTN + 2 * KV_DIM
D_IN = O_ATT + N_ATT
O_KPL = 2 * D_ATTN
ZA_W = O_KPL + 4 * KV_DIM

LANES = 128
HALO = 16

TBLK = 256
SHIFT_ROWS = 128
HY_UNROLL = 4
HY_PARTS = 4
HY_PART_CH = D_HYENA // HY_PARTS

VMEM_LIMIT = 56 * 1024 * 1024

_HI = lax.Precision.HIGHEST


def _dot_bf16x3(a, b):
    f32, bf16 = jnp.float32, jnp.bfloat16
    a_hi = a.astype(bf16)
    a_lo = (a - a_hi.astype(f32)).astype(bf16)
    b_hi = b.astype(bf16)
    b_lo = (b - b_hi.astype(f32)).astype(bf16)

    def dot(x, y):
        return jnp.dot(x, y, preferred_element_type=f32)

    return dot(a_hi, b_hi) + (dot(a_hi, b_lo) + dot(a_lo, b_hi))


def _filter_consts(L):
    t = np.arange(L, dtype=np.float32)
    t_norm = t / np.float32(max(L - 1, 1))
    w = np.float32(2.0 * math.pi) * t / np.float32(L)
    bands = np.linspace(1e-4, N_BANDS - 1, N_BANDS).astype(np.float32)
    ang = w[:, None] * bands[None, :]
    z = np.concatenate([t_norm[:, None], np.cos(ang), -np.sin(ang)], axis=-1)
    min_decay = math.log(DECAY_TARGET) / SLOW_DECAY_PCT
    max_decay = math.log(DECAY_TARGET) / FAST_DECAY_PCT
    deltas = np.abs(np.linspace(min_decay, max_decay, D_HYENA)).astype(np.float32)
    decay = np.exp(-t_norm[:, None] * deltas[None, :]).astype(np.float32)
    pad = np.zeros((POS_EMB_PAD - POS_EMB_DIM, L), np.float32)
    zt = np.concatenate([z.T, pad], axis=0)
    flip = np.eye(LANES, dtype=np.float32)[::-1]
    return zt, np.ascontiguousarray(decay.T), np.ascontiguousarray(flip)


def _filter_kernel(zt_ref, dec_ref, flip_ref, p_ref, w4_ref, k_ref):
    L = zt_ref.shape[1]
    f32 = jnp.float32
    w1 = p_ref[:, 0:POS_EMB_PAD]
    w2 = p_ref[:, LANES:LANES + FILTER_ORDER]
    w3 = p_ref[:, 2 * LANES:2 * LANES + FILTER_ORDER]
    vec = p_ref[:, 3 * LANES:4 * LANES]
    w4 = w4_ref[...].T
    h = jnp.dot(w1, zt_ref[...], precision=_HI, preferred_element_type=f32)
    h = jnp.sin(vec[:, 3:4] * (h + vec[:, 0:1]))
    h = jnp.dot(w2, h, precision=_HI, preferred_element_type=f32)
    h = jnp.sin(vec[:, 4:5] * (h + vec[:, 1:2]))
    h = jnp.dot(w3, h, precision=_HI, preferred_element_type=f32)
    h = jnp.sin(vec[:, 5:6] * (h + vec[:, 2:3]))
    hf = _dot_bf16x3(w4[0:D_HYENA, :], h) * dec_ref[...]
    hb = _dot_bf16x3(w4[D_HYENA:, :], h) * dec_ref[...]
    nt = L // LANES
    rev = jnp.concatenate(
        [jnp.dot(hb[:, (nt - 1 - j) * LANES:(nt - j) * LANES], flip_ref[...], precision=_HI,
                 preferred_element_type=f32) for j in range(nt)], axis=1)
    col = lax.broadcasted_iota(jnp.int32, rev.shape, 1)
    hb2 = jnp.where(col == 0, 0.0, pltpu.roll(rev, 1, axis=1))
    ss = jnp.sum(hf * hf, axis=1, keepdims=True) + jnp.sum(hb2 * hb2, axis=1, keepdims=True)
    inv = lax.rsqrt(ss + 1e-12)
    k_ref[:, 0:L] = hf * inv
    k_ref[:, L:2 * L] = hb2 * inv


def _hyena_filter(L, w_f1, b_f1, w_f2, b_f2, w_f3, b_f3, w_f4, sin_freq):
    zt, dec, flip = _filter_consts(L)
    def tile(a):
        return jnp.pad(a, ((0, 0), (0, LANES - a.shape[1])))

    vectors = jnp.stack([b_f1, b_f2, b_f3, sin_freq[0], sin_freq[1], sin_freq[2]], axis=1)
    params = jnp.concatenate([tile(w_f1.T), tile(w_f2.T), tile(w_f3.T), tile(vectors)], axis=1)
    args = (jnp.asarray(zt), jnp.asarray(dec), jnp.asarray(flip), params, w_f4)
    return pl.pallas_call(
        _filter_kernel,
        out_shape=jax.ShapeDtypeStruct((D_HYENA, 2 * L), jnp.float32),
        compiler_params=pltpu.CompilerParams(vmem_limit_bytes=VMEM_LIMIT),
        name="hyena_filter",
    )(*args)


def _inproj_kernel(xp_ref, x_ref, xn_ref, g_ref, w_ref, wsh_ref, bsh_ref, *out_refs):
    v_refs = out_refs[:HY_PARTS]
    x0g_ref, za_ref, vt_ref = out_refs[HY_PARTS:]
    t = pl.program_id(1)
    nt = pl.num_programs(1)
    tm = x_ref.shape[1]

    def norm(xv):
        ms = jnp.mean(xv * xv, axis=-1, keepdims=True)
        return (xv * lax.rsqrt(ms + RMS_EPS) * g_ref[...]).astype(jnp.bfloat16)

    h = norm(x_ref[0])
    hp = jnp.where(t > 0, norm(xp_ref[0]), 0.0).astype(jnp.bfloat16)
    hn = jnp.where(t < nt - 1, norm(xn_ref[0]), 0.0).astype(jnp.bfloat16)
    hall = jnp.concatenate([hp, h, hn], axis=0)
    zh = jnp.dot(hall, w_ref[:, 0:N_HY], preferred_element_type=jnp.float32)
    z = jnp.dot(h, w_ref[:, N_HY:], preferred_element_type=jnp.float32)

    rows = tm + 2 * HALO
    u_prev = pltpu.roll(zh, 1, axis=0)[HALO:HALO + tm]
    u = zh[HALO:HALO + tm]
    u_next = pltpu.roll(zh, rows - 1, axis=0)[HALO:HALO + tm]
    uc = (u_prev * wsh_ref[0:1, :] + u * wsh_ref[1:2, :] + u_next * wsh_ref[2:3, :]
          + bsh_ref[...])
    x0 = uc[:, 0:D_HYENA]
    x1 = uc[:, D_HYENA:2 * D_HYENA]
    vv = uc[:, 2 * D_HYENA:]
    gh = z[:, 0:D_HYENA]
    vx = (vv * x1).astype(v_refs[0].dtype)
    for p, v_ref in enumerate(v_refs):
        v_ref[0] = vx[:, p * HY_PART_CH:(p + 1) * HY_PART_CH]
    x0g_ref[0] = (x0 * (gh * jax.nn.sigmoid(gh))).astype(x0g_ref.dtype)
    o_q = O_ATT - N_HY
    za_ref[0, :, 0:D_ATTN] = (z[:, o_q:o_q + D_ATTN] * Q_SCALE_LOG2).astype(za_ref.dtype)
    o_k = o_q + D_ATTN
    o_v = o_k + KV_DIM
    o_ga = o_v + KV_DIM
    ga = z[:, o_ga:o_ga + D_ATTN]
    za_ref[0, :, D_ATTN:2 * D_ATTN] = (ga * jax.nn.sigmoid(ga)).astype(za_ref.dtype)
    kc = z[:, o_k:o_k + KV_DIM]
    ksw = pltpu.roll(kc, HEAD_DIM, axis=1)
    lo = lax.broadcasted_iota(jnp.int32, kc.shape, 1) < HEAD_DIM
    placed = (jnp.where(lo, kc, 0.0), jnp.where(lo, 0.0, ksw),
              jnp.where(lo, ksw, 0.0), jnp.where(lo, 0.0, kc))
    for n, kp in enumerate(placed):
        za_ref[0, :, O_KPL + n * KV_DIM:O_KPL + (n + 1) * KV_DIM] = kp.astype(za_ref.dtype)
    vc = z[:, o_v:o_v + KV_DIM]
    for j in range(tm // BLOCK):
        vt_ref[0, j] = vc[j * BLOCK:(j + 1) * BLOCK, :].T.astype(vt_ref.dtype)


def _inproj(x, pre_g, w_cat, w_short, b_short, tm=1024):
    B, L, D = x.shape
    rb = tm // HALO
    nrb = L // HALO
    grid = (B, L // tm)
    return pl.pallas_call(
        _inproj_kernel,
        grid=grid,
        in_specs=[
            pl.BlockSpec((1, HALO, D), lambda b, t: (b, jnp.maximum(t * rb - 1, 0), 0)),
            pl.BlockSpec((1, tm, D), lambda b, t: (b, t, 0)),
            pl.BlockSpec((1, HALO, D), lambda b, t: (b, jnp.minimum((t + 1) * rb, nrb - 1), 0)),
            pl.BlockSpec((1, D), lambda b, t: (0, 0)),
            pl.BlockSpec((D, D_IN), lambda b, t: (0, 0)),
            pl.BlockSpec((3, N_HY), lambda b, t: (0, 0)),
            pl.BlockSpec((1, N_HY), lambda b, t: (0, 0)),
        ],
        out_specs=(
            [pl.BlockSpec((1, tm, HY_PART_CH), lambda b, t: (b, t, 0))] * HY_PARTS
            + [pl.BlockSpec((1, tm, D_HYENA), lambda b, t: (b, t, 0)),
               pl.BlockSpec((1, tm, ZA_W), lambda b, t: (b, t, 0)),
               pl.BlockSpec((1, tm // BLOCK, KV_DIM, BLOCK), lambda b, t: (b, t, 0, 0))]),
        out_shape=(
            [jax.ShapeDtypeStruct((B, L, HY_PART_CH), jnp.bfloat16)] * HY_PARTS
            + [jax.ShapeDtypeStruct((B, L, D_HYENA), jnp.bfloat16),
               jax.ShapeDtypeStruct((B, L, ZA_W), jnp.bfloat16),
               jax.ShapeDtypeStruct((B, L // BLOCK, KV_DIM, BLOCK), jnp.bfloat16)]),
        compiler_params=pltpu.CompilerParams(
            dimension_semantics=("parallel", "arbitrary"), vmem_limit_bytes=VMEM_LIMIT),
        name="inproj",
    )(x, x, x, pre_g[None, :], w_cat, w_short, b_short[None, :])


def _alibi_slopes():
    return [float(v) for v in
            np.exp2(-8.0 * np.arange(1, N_HEADS + 1, dtype=np.float32) / N_HEADS).astype(np.float32)]


def _attn_bias_tables():
    c = np.arange(BLOCK)[:, None]
    r = np.arange(BLOCK)[None, :]
    dist = np.abs(c - r).astype(np.float32)
    slopes = np.asarray(_alibi_slopes(), np.float32)[:, None, None] * np.float32(LOG2E)
    return np.concatenate([-slopes * dist, -slopes * (WINDOW - dist)], axis=1).astype(np.float32)


def _back_kernel(sink_ref, bias_ref, mask_ref, za_ref, vt_ref, x_ref, *refs):
    yh_refs = refs[:HY_PARTS]
    x0g_ref, w_ref, g_ref, o_ref, s_ref, ot_ref, ya_ref = refs[HY_PARTS:]
    t = pl.program_id(1)
    tm = x_ref.shape[1]
    L = za_ref.shape[1]
    nb = L // BLOCK
    nbs = tm // BLOCK
    i0 = t * nbs
    o_g = D_ATTN
    f32 = jnp.float32
    bf16 = jnp.bfloat16
    slopes = _alibi_slopes()
    group = Q_PER_KV * HEAD_DIM

    c_i = lax.broadcasted_iota(jnp.int32, (BLOCK, BLOCK), 0)
    r_i = lax.broadcasted_iota(jnp.int32, (BLOCK, BLOCK), 1)
    upper = c_i >= r_i
    diag = c_i == r_i
    ones_rows = jnp.ones((16, 3 * BLOCK), bf16)

    def key_tile(j):
        return jnp.clip(j, 0, nb - 1)

    def scores(i, slot):
        start = pl.multiple_of(i * BLOCK, BLOCK)
        qblk = za_ref[0, pl.ds(start, BLOCK), 0:D_ATTN]
        krows = [pl.ds(pl.multiple_of(key_tile(i + tt - 1) * BLOCK, BLOCK), BLOCK)
                 for tt in range(3)]
        for kv in range(N_KV_HEADS):
            base = kv * group
            qs = jnp.concatenate([qblk[:, base:base + 2 * HEAD_DIM],
                                  qblk[:, base + 2 * HEAD_DIM:base + group]], axis=0)
            kp = jnp.concatenate(
                [za_ref[0, krows[tt], O_KPL + (2 * kv + half) * KV_DIM:
                        O_KPL + (2 * kv + half + 1) * KV_DIM]
                 for half in range(2) for tt in range(3)], axis=0)
            s_ref[slot, kv] = lax.dot_general(kp, qs, (((1,), (1,)), ((), ())),
                                              preferred_element_type=f32)

    def softmax_pv(i, slot, edge):
        pen1 = jnp.where(i == 0, NEG_INF, 0.0).astype(f32)
        pen3 = jnp.where(i == nb - 1, NEG_INF, 0.0).astype(f32)
        for kv in range(N_KV_HEADS):
            rows = slice(kv * HEAD_DIM, (kv + 1) * HEAD_DIM)
            vt3 = vt_ref[0, key_tile(i + 1), rows, :]
            vtw = jnp.concatenate([vt_ref[0, key_tile(i - 1), rows, :], vt_ref[0, i, rows, :], vt3],
                                  axis=1)
            pts, extras, pxs = [], [], []
            for g in range(Q_PER_KV):
                half, pair = g % 2, g // 2
                h = kv * Q_PER_KV + g
                r0 = half * 3 * BLOCK
                cols = slice(pair * BLOCK, (pair + 1) * BLOCK)
                s1 = s_ref[slot, kv, r0:r0 + BLOCK, cols]
                s2 = s_ref[slot, kv, r0 + BLOCK:r0 + 2 * BLOCK, cols]
                s3 = s_ref[slot, kv, r0 + 2 * BLOCK:r0 + 3 * BLOCK, cols]
                if edge == "lo":
                    mrg = jnp.where(upper, s1 + pen1, s3)
                elif edge == "hi":
                    mrg = jnp.where(upper, s1, s3 + pen3)
                else:
                    mrg = jnp.where(upper, s1, s3)
                a_mid = s2 + bias_ref[h, 0:BLOCK, :]
                a_mrg = mrg + bias_ref[h, BLOCK:, :]
                sink2 = sink_ref[h] * LOG2E
                s_x = (jnp.sum(jnp.where(diag, s3, 0.0), axis=0, keepdims=True)
                       - slopes[h] * LOG2E * WINDOW)
                if edge == "hi":
                    s_x = s_x + pen3
                m = jnp.max(jnp.maximum(a_mid, a_mrg), axis=0, keepdims=True)
                m = jnp.maximum(jnp.maximum(m, sink2), s_x)
                p_mid = jnp.exp2(a_mid - m).astype(bf16)
                p_mrg = jnp.exp2(a_mrg - m).astype(bf16)
                p_x = jnp.exp2(s_x - m)
                pxs.append(p_x)
                extras.append(jnp.exp2(sink2 - m) + p_x)
                p1 = p_mrg * mask_ref[...]
                p3 = p_mrg - p1
                pts.append(jnp.concatenate([p1, p_mid, p3], axis=0))
                if len(pts) < PV_HEADS:
                    continue
                pt = jnp.concatenate(pts, axis=1)
                ot = jnp.dot(jnp.concatenate([vtw, ones_rows], axis=0), pt,
                             preferred_element_type=f32)
                for n in range(PV_HEADS):
                    hh = h - (PV_HEADS - 1) + n
                    cols = slice(n * BLOCK, (n + 1) * BLOCK)
                    o = ot[0:HEAD_DIM, cols] + vt3.astype(f32) * pxs[n]
                    inv = 1.0 / (ot[HEAD_DIM:HEAD_DIM + 1, cols] + extras[n])
                    ot_ref[slot, hh * HEAD_DIM:(hh + 1) * HEAD_DIM, :] = o * inv
                pts, extras, pxs = [], [], []

    def finish(i, slot):
        start = pl.multiple_of(i * BLOCK, BLOCK)
        local = pl.multiple_of((i - i0) * BLOCK, BLOCK)
        gate = za_ref[0, pl.ds(start, BLOCK), o_g:o_g + D_ATTN].astype(f32)
        ya_ref[pl.ds(local, BLOCK), :] = (ot_ref[slot].T * gate).astype(ya_ref.dtype)

    def two_blocks(j):
        i = i0 + 2 * j + 1
        scores(i + 1, 0)
        softmax_pv(i, 1, None)
        finish(i - 1, 0)
        scores(i + 2, 1)
        softmax_pv(i + 1, 0, None)
        finish(i, 1)

    def project(c):
        r = slice(2 * c * BLOCK, 2 * (c + 1) * BLOCK)
        x0g = x0g_ref[0, r, :].astype(f32)
        yh = jnp.concatenate([ref[0, r, :] for ref in yh_refs], axis=1).astype(f32)
        y = jnp.dot(jnp.concatenate([(yh * x0g).astype(bf16), ya_ref[r, :]], axis=1), w_ref[...],
                    preferred_element_type=f32)
        ms = jnp.mean(y * y, axis=-1, keepdims=True)
        o_ref[0, r, :] = x_ref[0, r, :] + y * lax.rsqrt(ms + RMS_EPS) * g_ref[...]

    scores(i0, 0)
    scores(i0 + 1, 1)
    softmax_pv(i0, 0, "lo")
    npairs = nbs // 2
    for j in range(npairs - 1):
        two_blocks(j)
        if j >= 1:
            project(j - 1)
    softmax_pv(i0 + nbs - 1, 1, "hi")
    finish(i0 + nbs - 2, 0)
    finish(i0 + nbs - 1, 1)
    for c in range(max(npairs - 2, 0), npairs):
        project(c)


def _back(za, vt, sink, x, yh_parts, x0g, w_out, post_g, tm=1024):
    B, L, D = x.shape
    nb = L // BLOCK
    nbs = tm // BLOCK
    assert WINDOW == BLOCK and nbs >= 4 and nbs % 2 == 0 and L % tm == 0
    bias = jnp.asarray(_attn_bias_tables())
    upper = jnp.asarray(np.triu(np.ones((BLOCK, BLOCK), np.float32)).T, jnp.bfloat16)
    return pl.pallas_call(
        _back_kernel,
        grid=(B, L // tm),
        in_specs=[
            pl.BlockSpec(memory_space=pltpu.SMEM),
            pl.BlockSpec((N_HEADS, 2 * BLOCK, BLOCK), lambda b, t: (0, 0, 0)),
            pl.BlockSpec((BLOCK, BLOCK), lambda b, t: (0, 0)),
            pl.BlockSpec((1, L, ZA_W), lambda b, t: (b, 0, 0)),
            pl.BlockSpec((1, nb, KV_DIM, BLOCK), lambda b, t: (b, 0, 0, 0)),
            pl.BlockSpec((1, tm, D), lambda b, t: (b, t, 0)),
        ] + [pl.BlockSpec((1, tm, HY_PART_CH), lambda b, t: (b, t, 0))] * HY_PARTS + [
            pl.BlockSpec((1, tm, D_HYENA), lambda b, t: (b, t, 0)),
            pl.BlockSpec((D_HYENA + D_ATTN, D), lambda b, t: (0, 0)),
            pl.BlockSpec((1, D), lambda b, t: (0, 0)),
        ],
        out_specs=pl.BlockSpec((1, tm, D), lambda b, t: (b, t, 0)),
        out_shape=jax.ShapeDtypeStruct((B, L, D), jnp.float32),
        scratch_shapes=[
            pltpu.VMEM((2, N_KV_HEADS, 6 * BLOCK, 2 * BLOCK), jnp.float32),
            pltpu.VMEM((2, D_ATTN, BLOCK), jnp.float32),
            pltpu.VMEM((tm, D_ATTN), jnp.bfloat16),
        ],
        compiler_params=pltpu.CompilerParams(
            dimension_semantics=("parallel", "parallel"), vmem_limit_bytes=VMEM_LIMIT),
        name="attn_outproj",
    )(sink, bias, upper, za, vt, x, *yh_parts, x0g, w_out, post_g[None, :])


def _hyena_kernel(k_ref, d_ref, v_ref, y_ref, *scratch):
    tables = scratch[:HY_UNROLL]
    accs = scratch[HY_UNROLL:]
    step = pl.program_id(0)
    cg = v_ref.shape[0]
    rows = v_ref.shape[1]
    nchan, n2 = k_ref.shape
    nblk = n2 // (2 * TBLK)
    bsz = rows // nblk
    off = TBLK * (nblk - 1) + SHIFT_ROWS

    def build_table(chan, s_ref):
        krow = k_ref[pl.ds(jnp.minimum(chan, nchan - 1), 1), :]
        kb = jnp.broadcast_to(krow, (SHIFT_ROWS, n2))
        s_ref[...] = pltpu.roll(kb, off, axis=1, stride=1, stride_axis=0).astype(s_ref.dtype)

    def convolve(ci, s_ref, acc_ref):
        acc_ref[...] = d_ref[pl.ds(ci, 1), :] * v_ref[ci].astype(jnp.float32)
        for d in range(-(nblk - 1), nblk):
            x0 = TBLK * (d + nblk - 1)
            w = jnp.concatenate(
                [s_ref[:, x0 + SHIFT_ROWS:x0 + SHIFT_ROWS + TBLK], s_ref[:, x0:x0 + TBLK]], axis=0)
            n = (nblk - abs(d)) * bsz
            src = 0 if d >= 0 else -d * bsz
            dst = d * bsz if d >= 0 else 0
            acc_ref[dst:dst + n, :] += jnp.dot(v_ref[ci, pl.ds(src, n), :], w,
                                               preferred_element_type=jnp.float32)
        y_ref[ci] = acc_ref[...].astype(y_ref.dtype)

    @pl.when(step == 0)
    def _():
        for u in range(HY_UNROLL):
            build_table(u, tables[u])

    def body(it, carry):
        for u in range(HY_UNROLL):
            convolve(it * HY_UNROLL + u, tables[u], accs[u])
        for u in range(HY_UNROLL):
            build_table(step * cg + (it + 1) * HY_UNROLL + u, tables[u])
        return carry

    lax.fori_loop(0, cg // HY_UNROLL, body, 0)


def _hyena_conv(kt, hyena_d, v_rows, part, cg=16):
    C, rows, _ = v_rows.shape
    n2 = kt.shape[1]
    steps = C // cg
    return pl.pallas_call(
        _hyena_kernel,
        grid=(steps,),
        in_specs=[
            pl.BlockSpec((C, n2), lambda c: (part, 0)),
            pl.BlockSpec((cg, 1), lambda c: (c + part * steps, 0)),
            pl.BlockSpec((cg, rows, TBLK), lambda c: (c, 0, 0)),
        ],
        out_specs=pl.BlockSpec((cg, rows, TBLK), lambda c: (c, 0, 0)),
        out_shape=jax.ShapeDtypeStruct((C, rows, TBLK), jnp.bfloat16),
        scratch_shapes=(
            [pltpu.VMEM((SHIFT_ROWS, n2), jnp.bfloat16)] * HY_UNROLL
            + [pltpu.VMEM((rows, TBLK), jnp.float32)] * HY_UNROLL),
        compiler_params=pltpu.CompilerParams(
            dimension_semantics=("arbitrary",), vmem_limit_bytes=VMEM_LIMIT),
        name="hyena_conv",
    )(kt, hyena_d[:, None], v_rows)


def _layer(x, pre_g, w_in, w_short, b_short, w_f1, b_f1, w_f2, b_f2, w_f3, b_f3, w_f4,
           sin_freq, hyena_d, attn_sink, w_out, post_g):
    B, L, _ = x.shape
    nblk = L // TBLK
    kt = _hyena_filter(L, w_f1, b_f1, w_f2, b_f2, w_f3, b_f3, w_f4, sin_freq)
    *v_parts, x0g, za, vt = _inproj(x, pre_g, w_in.astype(jnp.bfloat16), w_short, b_short)
    pc = HY_PART_CH
    yh = []
    for part, v in enumerate(v_parts):
        v_rows = v.reshape(B, nblk, TBLK, pc).transpose(3, 1, 0, 2).reshape(pc, nblk * B, TBLK)
        y_rows = _hyena_conv(kt, hyena_d, v_rows, part)
        yh.append(y_rows.reshape(pc, nblk, B, TBLK).transpose(2, 1, 3, 0).reshape(B, L, pc))
    return _back(za, vt, attn_sink, x, yh, x0g, w_out.astype(jnp.bfloat16), post_g)


def kernel(x, pre_g, w_in, w_short, b_short, w_f1, b_f1, w_f2, b_f2, w_f3, b_f3, w_f4, sin_freq, hyena_d, attn_sink, w_out, post_g):
    depth = pre_g.shape[0]
    for l in range(depth):
        x = _layer(x, pre_g[l], w_in[l], w_short[l], b_short[l], w_f1[l], b_f1[l], w_f2[l],
                   b_f2[l], w_f3[l], b_f3[l], w_f4[l], sin_freq[l], hyena_d[l], attn_sink[l],
                   w_out[l], post_g[l])
    return x
```

```python
import math

import jax
import jax.numpy as jnp
import numpy as np
from jax import lax
from jax.experimental import pallas as pl
from jax.experimental.pallas import tpu as pltpu

D_MODEL = 1024
D_HYENA = 512
D_ATTN = 512
N_HEADS = 8
HEAD_DIM = 64
N_KV_HEADS = 2
Q_PER_KV = N_HEADS // N_KV_HEADS
KV_DIM = N_KV_HEADS * HEAD_DIM
WINDOW = 128
BLOCK = 128
FILTER_ORDER = 64
N_BANDS = 16
POS_EMB_DIM = 1 + 2 * N_BANDS
POS_EMB_PAD = 40
DECAY_TARGET = 1e-2
FAST_DECAY_PCT = 0.3
SLOW_DECAY_PCT = 1.5
RMS_EPS = 1e-6
NEG_INF = -1e30
PV_HEADS = 2
LOG2E = math.log2(math.e)
Q_SCALE_LOG2 = HEAD_DIM ** -0.5 * LOG2E

N_HY = 3 * D_HYENA
O_GH = N_HY
O_ATT = O_GH + D_HYENA
N_ATT = 2 * D_ATTN + 2 * KV_DIM
D_IN = O_ATT + N_ATT
O_KPL = 2 * D_ATTN
ZA_W = O_KPL + 4 * KV_DIM

LANES = 128
HALO = 16

TBLK = 256
SHIFT_ROWS = 128
HY_UNROLL = 4
HY_PARTS = 4
HY_PART_CH = D_HYENA // HY_PARTS

VMEM_LIMIT = 56 * 1024 * 1024

_HI = lax.Precision.HIGHEST


def _dot_bf16x3(a, b):
    f32, bf16 = jnp.float32, jnp.bfloat16
    a_hi = a.astype(bf16)
    a_lo = (a - a_hi.astype(f32)).astype(bf16)
    b_hi = b.astype(bf16)
    b_lo = (b - b_hi.astype(f32)).astype(bf16)

    def dot(x, y):
        return jnp.dot(x, y, preferred_element_type=f32)

    return dot(a_hi, b_hi) + (dot(a_hi, b_lo) + dot(a_lo, b_hi))


def _filter_consts(L):
    t = np.arange(L, dtype=np.float32)
    t_norm = t / np.float32(max(L - 1, 1))
    w = np.float32(2.0 * math.pi) * t / np.float32(L)
    bands = np.linspace(1e-4, N_BANDS - 1, N_BANDS).astype(np.float32)
    ang = w[:, None] * bands[None, :]
    z = np.concatenate([t_norm[:, None], np.cos(ang), -np.sin(ang)], axis=-1)
    min_decay = math.log(DECAY_TARGET) / SLOW_DECAY_PCT
    max_decay = math.log(DECAY_TARGET) / FAST_DECAY_PCT
    deltas = np.abs(np.linspace(min_decay, max_decay, D_HYENA)).astype(np.float32)
    decay = np.exp(-t_norm[:, None] * deltas[None, :]).astype(np.float32)
    pad = np.zeros((POS_EMB_PAD - POS_EMB_DIM, L), np.float32)
    zt = np.concatenate([z.T, pad], axis=0)
    flip = np.eye(LANES, dtype=np.float32)[::-1]
    return zt, np.ascontiguousarray(decay.T), np.ascontiguousarray(flip)


def _filter_kernel(zt_ref, dec_ref, flip_ref, p_ref, w4_ref, k_ref):
    L = zt_ref.shape[1]
    f32 = jnp.float32
    w1 = p_ref[:, 0:POS_EMB_PAD]
    w2 = p_ref[:, LANES:LANES + FILTER_ORDER]
    w3 = p_ref[:, 2 * LANES:2 * LANES + FILTER_ORDER]
    vec = p_ref[:, 3 * LANES:4 * LANES]
    w4 = w4_ref[...].T
    h = jnp.dot(w1, zt_ref[...], precision=_HI, preferred_element_type=f32)
    h = jnp.sin(vec[:, 3:4] * (h + vec[:, 0:1]))
    h = jnp.dot(w2, h, precision=_HI, preferred_element_type=f32)
    h = jnp.sin(vec[:, 4:5] * (h + vec[:, 1:2]))
    h = jnp.dot(w3, h, precision=_HI, preferred_element_type=f32)
    h = jnp.sin(vec[:, 5:6] * (h + vec[:, 2:3]))
    hf = _dot_bf16x3(w4[0:D_HYENA, :], h) * dec_ref[...]
    hb = _dot_bf16x3(w4[D_HYENA:, :], h) * dec_ref[...]
    nt = L // LANES
    rev = jnp.concatenate(
        [jnp.dot(hb[:, (nt - 1 - j) * LANES:(nt - j) * LANES], flip_ref[...], precision=_HI,
                 preferred_element_type=f32) for j in range(nt)], axis=1)
    col = lax.broadcasted_iota(jnp.int32, rev.shape, 1)
    hb2 = jnp.where(col == 0, 0.0, pltpu.roll(rev, 1, axis=1))
    ss = jnp.sum(hf * hf, axis=1, keepdims=True) + jnp.sum(hb2 * hb2, axis=1, keepdims=True)
    inv = lax.rsqrt(ss + 1e-12)
    k_ref[:, 0:L] = hf * inv
    k_ref[:, L:2 * L] = hb2 * inv


def _hyena_filter(L, w_f1, b_f1, w_f2, b_f2, w_f3, b_f3, w_f4, sin_freq):
    zt, dec, flip = _filter_consts(L)
    def tile(a):
        return jnp.pad(a, ((0, 0), (0, LANES - a.shape[1])))

    vectors = jnp.stack([b_f1, b_f2, b_f3, sin_freq[0], sin_freq[1], sin_freq[2]], axis=1)
    params = jnp.concatenate([tile(w_f1.T), tile(w_f2.T), tile(w_f3.T), tile(vectors)], axis=1)
    args = (jnp.asarray(zt), jnp.asarray(dec), jnp.asarray(flip), params, w_f4)
    return pl.pallas_call(
        _filter_kernel,
        out_shape=jax.ShapeDtypeStruct((D_HYENA, 2 * L), jnp.float32),
        compiler_params=pltpu.CompilerParams(vmem_limit_bytes=VMEM_LIMIT),
        name="hyena_filter",
    )(*args)


def _inproj_kernel(xp_ref, x_ref, xn_ref, g_ref, w_ref, wsh_ref, bsh_ref, *out_refs):
    v_refs = out_refs[:HY_PARTS]
    x0g_ref, za_ref, vt_ref = out_refs[HY_PARTS:]
    t = pl.program_id(1)
    nt = pl.num_programs(1)
    tm = x_ref.shape[1]

    def norm(xv):
        ms = jnp.mean(xv * xv, axis=-1, keepdims=True)
        return (xv * lax.rsqrt(ms + RMS_EPS) * g_ref[...]).astype(jnp.bfloat16)

    h = norm(x_ref[0])
    hp = jnp.where(t > 0, norm(xp_ref[0]), 0.0).astype(jnp.bfloat16)
    hn = jnp.where(t < nt - 1, norm(xn_ref[0]), 0.0).astype(jnp.bfloat16)
    hall = jnp.concatenate([hp, h, hn], axis=0)
    zh = jnp.dot(hall, w_ref[:, 0:N_HY], preferred_element_type=jnp.float32)
    z = jnp.dot(h, w_ref[:, N_HY:], preferred_element_type=jnp.float32)

    rows = tm + 2 * HALO
    u_prev = pltpu.roll(zh, 1, axis=0)[HALO:HALO + tm]
    u = zh[HALO:HALO + tm]
    u_next = pltpu.roll(zh, rows - 1, axis=0)[HALO:HALO + tm]
    uc = (u_prev * wsh_ref[0:1, :] + u * wsh_ref[1:2, :] + u_next * wsh_ref[2:3, :]
          + bsh_ref[...])
    x0 = uc[:, 0:D_HYENA]
    x1 = uc[:, D_HYENA:2 * D_HYENA]
    vv = uc[:, 2 * D_HYENA:]
    gh = z[:, 0:D_HYENA]
    vx = (vv * x1).astype(v_refs[0].dtype)
    for p, v_ref in enumerate(v_refs):
        v_ref[0] = vx[:, p * HY_PART_CH:(p + 1) * HY_PART_CH]
    x0g_ref[0] = (x0 * (gh * jax.nn.sigmoid(gh))).astype(x0g_ref.dtype)
    o_q = O_ATT - N_HY
    za_ref[0, :, 0:D_ATTN] = (z[:, o_q:o_q + D_ATTN] * Q_SCALE_LOG2).astype(za_ref.dtype)
    o_k = o_q + D_ATTN
    o_v = o_k + KV_DIM
    o_ga = o_v + KV_DIM
    ga = z[:, o_ga:o_ga + D_ATTN]
    za_ref[0, :, D_ATTN:2 * D_ATTN] = (ga * jax.nn.sigmoid(ga)).astype(za_ref.dtype)
    kc = z[:, o_k:o_k + KV_DIM]
    ksw = pltpu.roll(kc, HEAD_DIM, axis=1)
    lo = lax.broadcasted_iota(jnp.int32, kc.shape, 1) < HEAD_DIM
    placed = (jnp.where(lo, kc, 0.0), jnp.where(lo, 0.0, ksw),
              jnp.where(lo, ksw, 0.0), jnp.where(lo, 0.0, kc))
    for n, kp in enumerate(placed):
        za_ref[0, :, O_KPL + n * KV_DIM:O_KPL + (n + 1) * KV_DIM] = kp.astype(za_ref.dtype)
    vc = z[:, o_v:o_v + KV_DIM]
    for j in range(tm // BLOCK):
        vt_ref[0, j] = vc[j * BLOCK:(j + 1) * BLOCK, :].T.astype(vt_ref.dtype)


def _inproj(x, pre_g, w_cat, w_short, b_short, tm=1024):
    B, L, D = x.shape
    rb = tm // HALO
    nrb = L // HALO
    grid = (B, L // tm)
    return pl.pallas_call(
        _inproj_kernel,
        grid=grid,
        in_specs=[
            pl.BlockSpec((1, HALO, D), lambda b, t: (b, jnp.maximum(t * rb - 1, 0), 0)),
            pl.BlockSpec((1, tm, D), lambda b, t: (b, t, 0)),
            pl.BlockSpec((1, HALO, D), lambda b, t: (b, jnp.minimum((t + 1) * rb, nrb - 1), 0)),
            pl.BlockSpec((1, D), lambda b, t: (0, 0)),
            pl.BlockSpec((D, D_IN), lambda b, t: (0, 0)),
            pl.BlockSpec((3, N_HY), lambda b, t: (0, 0)),
            pl.BlockSpec((1, N_HY), lambda b, t: (0, 0)),
        ],
        out_specs=(
            [pl.BlockSpec((1, tm, HY_PART_CH), lambda b, t: (b, t, 0))] * HY_PARTS
            + [pl.BlockSpec((1, tm, D_HYENA), lambda b, t: (b, t, 0)),
               pl.BlockSpec((1, tm, ZA_W), lambda b, t: (b, t, 0)),
               pl.BlockSpec((1, tm // BLOCK, KV_DIM, BLOCK), lambda b, t: (b, t, 0, 0))]),
        out_shape=(
            [jax.ShapeDtypeStruct((B, L, HY_PART_CH), jnp.bfloat16)] * HY_PARTS
            + [jax.ShapeDtypeStruct((B, L, D_HYENA), jnp.bfloat16),
               jax.ShapeDtypeStruct((B, L, ZA_W), jnp.bfloat16),
               jax.ShapeDtypeStruct((B, L // BLOCK, KV_DIM, BLOCK), jnp.bfloat16)]),
        compiler_params=pltpu.CompilerParams(
            dimension_semantics=("parallel", "arbitrary"), vmem_limit_bytes=VMEM_LIMIT),
        name="inproj",
    )(x, x, x, pre_g[None, :], w_cat, w_short, b_short[None, :])


def _alibi_slopes():
    return [float(v) for v in
            np.exp2(-8.0 * np.arange(1, N_HEADS + 1, dtype=np.float32) / N_HEADS).astype(np.float32)]


def _attn_bias_tables():
    c = np.arange(BLOCK)[:, None]
    r = np.arange(BLOCK)[None, :]
    dist = np.abs(c - r).astype(np.float32)
    slopes = np.asarray(_alibi_slopes(), np.float32)[:, None, None] * np.float32(LOG2E)
    return np.concatenate([-slopes * dist, -slopes * (WINDOW - dist)], axis=1).astype(np.float32)


def _back_kernel(sink_ref, bias_ref, mask_ref, za_ref, vt_ref, x_ref, *refs):
    yh_refs = refs[:HY_PARTS]
    x0g_ref, w_ref, g_ref, o_ref, s_ref, ot_ref, ya_ref = refs[HY_PARTS:]
    t = pl.program_id(1)
    tm = x_ref.shape[1]
    L = za_ref.shape[1]
    nb = L // BLOCK
    nbs = tm // BLOCK
    i0 = t * nbs
    o_g = D_ATTN
    f32 = jnp.float32
    bf16 = jnp.bfloat16
    slopes = _alibi_slopes()
    group = Q_PER_KV * HEAD_DIM

    c_i = lax.broadcasted_iota(jnp.int32, (BLOCK, BLOCK), 0)
    r_i = lax.broadcasted_iota(jnp.int32, (BLOCK, BLOCK), 1)
    upper = c_i >= r_i
    diag = c_i == r_i
    ones_rows = jnp.ones((16, 3 * BLOCK), bf16)

    def key_tile(j):
        return jnp.clip(j, 0, nb - 1)

    def scores(i, slot):
        start = pl.multiple_of(i * BLOCK, BLOCK)
        qblk = za_ref[0, pl.ds(start, BLOCK), 0:D_ATTN]
        krows = [pl.ds(pl.multiple_of(key_tile(i + tt - 1) * BLOCK, BLOCK), BLOCK)
                 for tt in range(3)]
        for kv in range(N_KV_HEADS):
            base = kv * group
            qs = jnp.concatenate([qblk[:, base:base + 2 * HEAD_DIM],
                                  qblk[:, base + 2 * HEAD_DIM:base + group]], axis=0)
            kp = jnp.concatenate(
                [za_ref[0, krows[tt], O_KPL + (2 * kv + half) * KV_DIM:
                        O_KPL + (2 * kv + half + 1) * KV_DIM]
                 for half in range(2) for tt in range(3)], axis=0)
            s_ref[slot, kv] = lax.dot_general(kp, qs, (((1,), (1,)), ((), ())),
                                              preferred_element_type=f32)

    def softmax_pv(i, slot, edge):
        pen1 = jnp.where(i == 0, NEG_INF, 0.0).astype(f32)
        pen3 = jnp.where(i == nb - 1, NEG_INF, 0.0).astype(f32)
        for kv in range(N_KV_HEADS):
            rows = slice(kv * HEAD_DIM, (kv + 1) * HEAD_DIM)
            vt3 = vt_ref[0, key_tile(i + 1), rows, :]
            vtw = jnp.concatenate([vt_ref[0, key_tile(i - 1), rows, :], vt_ref[0, i, rows, :], vt3],
                                  axis=1)
            pts, extras, pxs = [], [], []
            for g in range(Q_PER_KV):
                half, pair = g % 2, g // 2
                h = kv * Q_PER_KV + g
                r0 = half * 3 * BLOCK
                cols = slice(pair * BLOCK, (pair + 1) * BLOCK)
                s1 = s_ref[slot, kv, r0:r0 + BLOCK, cols]
                s2 = s_ref[slot, kv, r0 + BLOCK:r0 + 2 * BLOCK, cols]
                s3 = s_ref[slot, kv, r0 + 2 * BLOCK:r0 + 3 * BLOCK, cols]
                if edge == "lo":
                    mrg = jnp.where(upper, s1 + pen1, s3)
                elif edge == "hi":
                    mrg = jnp.where(upper, s1, s3 + pen3)
                else:
                    mrg = jnp.where(upper, s1, s3)
                a_mid = s2 + bias_ref[h, 0:BLOCK, :]
                a_mrg = mrg + bias_ref[h, BLOCK:, :]
                sink2 = sink_ref[h] * LOG2E
                s_x = (jnp.sum(jnp.where(diag, s3, 0.0), axis=0, keepdims=True)
                       - slopes[h] * LOG2E * WINDOW)
                if edge == "hi":
                    s_x = s_x + pen3
                m = jnp.max(jnp.maximum(a_mid, a_mrg), axis=0, keepdims=True)
                m = jnp.maximum(jnp.maximum(m, sink2), s_x)
                p_mid = jnp.exp2(a_mid - m).astype(bf16)
                p_mrg = jnp.exp2(a_mrg - m).astype(bf16)
                p_x = jnp.exp2(s_x - m)
                pxs.append(p_x)
                extras.append(jnp.exp2(sink2 - m) + p_x)
                p1 = p_mrg * mask_ref[...]
                p3 = p_mrg - p1
                pts.append(jnp.concatenate([p1, p_mid, p3], axis=0))
                if len(pts) < PV_HEADS:
                    continue
                pt = jnp.concatenate(pts, axis=1)
                ot = jnp.dot(jnp.concatenate([vtw, ones_rows], axis=0), pt,
                             preferred_element_type=f32)
                for n in range(PV_HEADS):
                    hh = h - (PV_HEADS - 1) + n
                    cols = slice(n * BLOCK, (n + 1) * BLOCK)
                    o = ot[0:HEAD_DIM, cols] + vt3.astype(f32) * pxs[n]
                    inv = 1.0 / (ot[HEAD_DIM:HEAD_DIM + 1, cols] + extras[n])
                    ot_ref[slot, hh * HEAD_DIM:(hh + 1) * HEAD_DIM, :] = o * inv
                pts, extras, pxs = [], [], []

    def finish(i, slot):
        start = pl.multiple_of(i * BLOCK, BLOCK)
        local = pl.multiple_of((i - i0) * BLOCK, BLOCK)
        gate = za_ref[0, pl.ds(start, BLOCK), o_g:o_g + D_ATTN].astype(f32)
        ya_ref[pl.ds(local, BLOCK), :] = (ot_ref[slot].T * gate).astype(ya_ref.dtype)

    def two_blocks(j):
        i = i0 + 2 * j + 1
        scores(i + 1, 0)
        softmax_pv(i, 1, None)
        finish(i - 1, 0)
        scores(i + 2, 1)
        softmax_pv(i + 1, 0, None)
        finish(i, 1)

    def project(c0, c1):
        r = slice(2 * c0 * BLOCK, 2 * c1 * BLOCK)
        x0g = x0g_ref[0, r, :].astype(f32)
        yh = jnp.concatenate([ref[0, r, :] for ref in yh_refs], axis=1).astype(f32)
        y = jnp.dot(jnp.concatenate([(yh * x0g).astype(bf16), ya_ref[r, :]], axis=1), w_ref[...],
                    preferred_element_type=f32)
        ms = jnp.mean(y * y, axis=-1, keepdims=True)
        o_ref[0, r, :] = x_ref[0, r, :] + y * lax.rsqrt(ms + RMS_EPS) * g_ref[...]

    scores(i0, 0)
    scores(i0 + 1, 1)
    softmax_pv(i0, 0, "lo")
    npairs = nbs // 2
    half = npairs // 2
    for j in range(npairs - 1):
        two_blocks(j)
        if j == half:
            project(0, half)
    softmax_pv(i0 + nbs - 1, 1, "hi")
    finish(i0 + nbs - 2, 0)
    finish(i0 + nbs - 1, 1)
    project(half, npairs)


def _back(za, vt, sink, x, yh_parts, x0g, w_out, post_g, tm=1024):
    B, L, D = x.shape
    nb = L // BLOCK
    nbs = tm // BLOCK
    assert WINDOW == BLOCK and nbs >= 4 and nbs % 2 == 0 and L % tm == 0
    bias = jnp.asarray(_attn_bias_tables())
    upper = jnp.asarray(np.triu(np.ones((BLOCK, BLOCK), np.float32)).T, jnp.bfloat16)
    return pl.pallas_call(
        _back_kernel,
        grid=(B, L // tm),
        in_specs=[
            pl.BlockSpec(memory_space=pltpu.SMEM),
            pl.BlockSpec((N_HEADS, 2 * BLOCK, BLOCK), lambda b, t: (0, 0, 0)),
            pl.BlockSpec((BLOCK, BLOCK), lambda b, t: (0, 0)),
            pl.BlockSpec((1, L, ZA_W), lambda b, t: (b, 0, 0)),
            pl.BlockSpec((1, nb, KV_DIM, BLOCK), lambda b, t: (b, 0, 0, 0)),
            pl.BlockSpec((1, tm, D), lambda b, t: (b, t, 0)),
        ] + [pl.BlockSpec((1, tm, HY_PART_CH), lambda b, t: (b, t, 0))] * HY_PARTS + [
            pl.BlockSpec((1, tm, D_HYENA), lambda b, t: (b, t, 0)),
            pl.BlockSpec((D_HYENA + D_ATTN, D), lambda b, t: (0, 0)),
            pl.BlockSpec((1, D), lambda b, t: (0, 0)),
        ],
        out_specs=pl.BlockSpec((1, tm, D), lambda b, t: (b, t, 0)),
        out_shape=jax.ShapeDtypeStruct((B, L, D), jnp.float32),
        scratch_shapes=[
            pltpu.VMEM((2, N_KV_HEADS, 6 * BLOCK, 2 * BLOCK), jnp.float32),
            pltpu.VMEM((2, D_ATTN, BLOCK), jnp.float32),
            pltpu.VMEM((tm, D_ATTN), jnp.bfloat16),
        ],
        compiler_params=pltpu.CompilerParams(
            dimension_semantics=("parallel", "parallel"), vmem_limit_bytes=VMEM_LIMIT),
        name="attn_outproj",
    )(sink, bias, upper, za, vt, x, *yh_parts, x0g, w_out, post_g[None, :])


def _hyena_kernel(k_ref, d_ref, v_ref, y_ref, *scratch):
    tables = scratch[:HY_UNROLL]
    accs = scratch[HY_UNROLL:]
    step = pl.program_id(0)
    cg = v_ref.shape[0]
    rows = v_ref.shape[1]
    nchan, n2 = k_ref.shape
    nblk = n2 // (2 * TBLK)
    bsz = rows // nblk
    off = TBLK * (nblk - 1) + SHIFT_ROWS

    def build_table(chan, s_ref):
        krow = k_ref[pl.ds(jnp.minimum(chan, nchan - 1), 1), :]
        kb = jnp.broadcast_to(krow, (SHIFT_ROWS, n2))
        s_ref[...] = pltpu.roll(kb, off, axis=1, stride=1, stride_axis=0).astype(s_ref.dtype)

    def convolve(ci, s_ref, acc_ref):
        acc_ref[...] = d_ref[pl.ds(ci, 1), :] * v_ref[ci].astype(jnp.float32)
        for d in range(-(nblk - 1), nblk):
            x0 = TBLK * (d + nblk - 1)
            w = jnp.concatenate(
                [s_ref[:, x0 + SHIFT_ROWS:x0 + SHIFT_ROWS + TBLK], s_ref[:, x0:x0 + TBLK]], axis=0)
            n = (nblk - abs(d)) * bsz
            src = 0 if d >= 0 else -d * bsz
            dst = d * bsz if d >= 0 else 0
            acc_ref[dst:dst + n, :] += jnp.dot(v_ref[ci, pl.ds(src, n), :], w,
                                               preferred_element_type=jnp.float32)
        y_ref[ci] = acc_ref[...].astype(y_ref.dtype)

    @pl.when(step == 0)
    def _():
        for u in range(HY_UNROLL):
            build_table(u, tables[u])

    def body(it, carry):
        for u in range(HY_UNROLL):
            convolve(it * HY_UNROLL + u, tables[u], accs[u])
        for u in range(HY_UNROLL):
            build_table(step * cg + (it + 1) * HY_UNROLL + u, tables[u])
        return carry

    lax.fori_loop(0, cg // HY_UNROLL, body, 0)


def _hyena_conv(kt, hyena_d, v_rows, part, cg=16):
    C, rows, _ = v_rows.shape
    n2 = kt.shape[1]
    steps = C // cg
    return pl.pallas_call(
        _hyena_kernel,
        grid=(steps,),
        in_specs=[
            pl.BlockSpec((C, n2), lambda c: (part, 0)),
            pl.BlockSpec((cg, 1), lambda c: (c + part * steps, 0)),
            pl.BlockSpec((cg, rows, TBLK), lambda c: (c, 0, 0)),
        ],
        out_specs=pl.BlockSpec((cg, rows, TBLK), lambda c: (c, 0, 0)),
        out_shape=jax.ShapeDtypeStruct((C, rows, TBLK), jnp.bfloat16),
        scratch_shapes=(
            [pltpu.VMEM((SHIFT_ROWS, n2), jnp.bfloat16)] * HY_UNROLL
            + [pltpu.VMEM((rows, TBLK), jnp.float32)] * HY_UNROLL),
        compiler_params=pltpu.CompilerParams(
            dimension_semantics=("arbitrary",), vmem_limit_bytes=VMEM_LIMIT),
        name="hyena_conv",
    )(kt, hyena_d[:, None], v_rows)


def _layer(x, pre_g, w_in, w_short, b_short, w_f1, b_f1, w_f2, b_f2, w_f3, b_f3, w_f4,
           sin_freq, hyena_d, attn_sink, w_out, post_g):
    B, L, _ = x.shape
    nblk = L // TBLK
    kt = _hyena_filter(L, w_f1, b_f1, w_f2, b_f2, w_f3, b_f3, w_f4, sin_freq)
    *v_parts, x0g, za, vt = _inproj(x, pre_g, w_in.astype(jnp.bfloat16), w_short, b_short)
    pc = HY_PART_CH
    yh = []
    for part, v in enumerate(v_parts):
        v_rows = v.reshape(B, nblk, TBLK, pc).transpose(3, 1, 0, 2).reshape(pc, nblk * B, TBLK)
        y_rows = _hyena_conv(kt, hyena_d, v_rows, part)
        yh.append(y_rows.reshape(pc, nblk, B, TBLK).transpose(2, 1, 3, 0).reshape(B, L, pc))
    return _back(za, vt, attn_sink, x, yh, x0g, w_out.astype(jnp.bfloat16), post_g)


def kernel(x, pre_g, w_in, w_short, b_short, w_f1, b_f1, w_f2, b_f2, w_f3, b_f3, w_f4, sin_freq, hyena_d, attn_sink, w_out, post_g):
    depth = pre_g.shape[0]
    for l in range(depth):
        x = _layer(x, pre_g[l], w_in[l], w_short[l], b_short[l], w_f1[l], b_f1[l], w_f2[l],
                   b_f2[l], w_f3[l], b_f3[l], w_f4[l], sin_freq[l], hyena_d[l], attn_sink[l],
                   w_out[l], post_g[l])
    return x
```

```python
import math

import jax
import jax.numpy as jnp
import numpy as np
from jax import lax
from jax.experimental import pallas as pl
from jax.experimental.pallas import tpu as pltpu

D_MODEL = 1024
D_HYENA = 512
D_ATTN = 512
N_HEADS = 8
HEAD_DIM = 64
N_KV_HEADS = 2
Q_PER_KV = N_HEADS // N_KV_HEADS
KV_DIM = N_KV_HEADS * HEAD_DIM
WINDOW = 128
BLOCK = 128
FILTER_ORDER = 64
N_BANDS = 16
POS_EMB_DIM = 1 + 2 * N_BANDS
POS_EMB_PAD = 40
DECAY_TARGET = 1e-2
FAST_DECAY_PCT = 0.3
SLOW_DECAY_PCT = 1.5
RMS_EPS = 1e-6
NEG_INF = -1e30
PV_HEADS = 2
LOG2E = math.log2(math.e)
Q_SCALE_LOG2 = HEAD_DIM ** -0.5 * LOG2E

N_HY = 3 * D_HYENA
O_GH = N_HY
O_ATT = O_GH + D_HYENA
N_ATT = 2 * D_ATTN + 2 * KV_DIM
D_IN = O_ATT + N_ATT
O_KPL = 2 * D_ATTN
ZA_W = O_KPL + 4 * KV_DIM

LANES = 128
HALO = 16

TBLK = 256
SHIFT_ROWS = 128
HY_UNROLL = 4
HY_PART_CH = (256, 128, 128)
HY_PART_OFF = (0, 256, 384)
HY_PARTS = len(HY_PART_CH)

VMEM_LIMIT = 56 * 1024 * 1024

_HI = lax.Precision.HIGHEST


def _dot_bf16x3(a, b):
    f32, bf16 = jnp.float32, jnp.bfloat16
    a_hi = a.astype(bf16)
    a_lo = (a - a_hi.astype(f32)).astype(bf16)
    b_hi = b.astype(bf16)
    b_lo = (b - b_hi.astype(f32)).astype(bf16)

    def dot(x, y):
        return jnp.dot(x, y, preferred_element_type=f32)

    return dot(a_hi, b_hi) + (dot(a_hi, b_lo) + dot(a_lo, b_hi))


def _filter_consts(L):
    t = np.arange(L, dtype=np.float32)
    t_norm = t / np.float32(max(L - 1, 1))
    w = np.float32(2.0 * math.pi) * t / np.float32(L)
    bands = np.linspace(1e-4, N_BANDS - 1, N_BANDS).astype(np.float32)
    ang = w[:, None] * bands[None, :]
    z = np.concatenate([t_norm[:, None], np.cos(ang), -np.sin(ang)], axis=-1)
    min_decay = math.log(DECAY_TARGET) / SLOW_DECAY_PCT
    max_decay = math.log(DECAY_TARGET) / FAST_DECAY_PCT
    deltas = np.abs(np.linspace(min_decay, max_decay, D_HYENA)).astype(np.float32)
    decay = np.exp(-t_norm[:, None] * deltas[None, :]).astype(np.float32)
    pad = np.zeros((POS_EMB_PAD - POS_EMB_DIM, L), np.float32)
    zt = np.concatenate([z.T, pad], axis=0)
    flip = np.eye(LANES, dtype=np.float32)[::-1]
    return zt, np.ascontiguousarray(decay.T), np.ascontiguousarray(flip)


def _filter_kernel(zt_ref, dec_ref, flip_ref, p_ref, w4_ref, k_ref):
    L = zt_ref.shape[1]
    f32 = jnp.float32
    w1 = p_ref[:, 0:POS_EMB_PAD]
    w2 = p_ref[:, LANES:LANES + FILTER_ORDER]
    w3 = p_ref[:, 2 * LANES:2 * LANES + FILTER_ORDER]
    vec = p_ref[:, 3 * LANES:4 * LANES]
    w4 = w4_ref[...].T
    h = jnp.dot(w1, zt_ref[...], precision=_HI, preferred_element_type=f32)
    h = jnp.sin(vec[:, 3:4] * (h + vec[:, 0:1]))
    h = jnp.dot(w2, h, precision=_HI, preferred_element_type=f32)
    h = jnp.sin(vec[:, 4:5] * (h + vec[:, 1:2]))
    h = jnp.dot(w3, h, precision=_HI, preferred_element_type=f32)
    h = jnp.sin(vec[:, 5:6] * (h + vec[:, 2:3]))
    hf = _dot_bf16x3(w4[0:D_HYENA, :], h) * dec_ref[...]
    hb = _dot_bf16x3(w4[D_HYENA:, :], h) * dec_ref[...]
    nt = L // LANES
    rev = jnp.concatenate(
        [jnp.dot(hb[:, (nt - 1 - j) * LANES:(nt - j) * LANES], flip_ref[...], precision=_HI,
                 preferred_element_type=f32) for j in range(nt)], axis=1)
    col = lax.broadcasted_iota(jnp.int32, rev.shape, 1)
    hb2 = jnp.where(col == 0, 0.0, pltpu.roll(rev, 1, axis=1))
    ss = jnp.sum(hf * hf, axis=1, keepdims=True) + jnp.sum(hb2 * hb2, axis=1, keepdims=True)
    inv = lax.rsqrt(ss + 1e-12)
    k_ref[:, 0:L] = hf * inv
    k_ref[:, L:2 * L] = hb2 * inv


def _hyena_filter(L, w_f1, b_f1, w_f2, b_f2, w_f3, b_f3, w_f4, sin_freq):
    zt, dec, flip = _filter_consts(L)
    def tile(a):
        return jnp.pad(a, ((0, 0), (0, LANES - a.shape[1])))

    vectors = jnp.stack([b_f1, b_f2, b_f3, sin_freq[0], sin_freq[1], sin_freq[2]], axis=1)
    params = jnp.concatenate([tile(w_f1.T), tile(w_f2.T), tile(w_f3.T), tile(vectors)], axis=1)
    args = (jnp.asarray(zt), jnp.asarray(dec), jnp.asarray(flip), params, w_f4)
    return pl.pallas_call(
        _filter_kernel,
        out_shape=jax.ShapeDtypeStruct((D_HYENA, 2 * L), jnp.float32),
        compiler_params=pltpu.CompilerParams(vmem_limit_bytes=VMEM_LIMIT),
        name="hyena_filter",
    )(*args)


def _inproj_kernel(xp_ref, x_ref, xn_ref, g_ref, w_ref, wsh_ref, bsh_ref, *out_refs):
    v_refs = out_refs[:HY_PARTS]
    x0g_ref, za_ref, vt_ref = out_refs[HY_PARTS:]
    t = pl.program_id(1)
    nt = pl.num_programs(1)
    tm = x_ref.shape[1]

    def norm(xv):
        ms = jnp.mean(xv * xv, axis=-1, keepdims=True)
        return (xv * lax.rsqrt(ms + RMS_EPS) * g_ref[...]).astype(jnp.bfloat16)

    h = norm(x_ref[0])
    hp = jnp.where(t > 0, norm(xp_ref[0]), 0.0).astype(jnp.bfloat16)
    hn = jnp.where(t < nt - 1, norm(xn_ref[0]), 0.0).astype(jnp.bfloat16)
    hall = jnp.concatenate([hp, h, hn], axis=0)
    zh = jnp.dot(hall, w_ref[:, 0:N_HY], preferred_element_type=jnp.float32)
    z = jnp.dot(h, w_ref[:, N_HY:], preferred_element_type=jnp.float32)

    rows = tm + 2 * HALO
    u_prev = pltpu.roll(zh, 1, axis=0)[HALO:HALO + tm]
    u = zh[HALO:HALO + tm]
    u_next = pltpu.roll(zh, rows - 1, axis=0)[HALO:HALO + tm]
    uc = (u_prev * wsh_ref[0:1, :] + u * wsh_ref[1:2, :] + u_next * wsh_ref[2:3, :]
          + bsh_ref[...])
    x0 = uc[:, 0:D_HYENA]
    x1 = uc[:, D_HYENA:2 * D_HYENA]
    vv = uc[:, 2 * D_HYENA:]
    gh = z[:, 0:D_HYENA]
    vx = (vv * x1).astype(v_refs[0].dtype)
    for p, v_ref in enumerate(v_refs):
        v_ref[0] = vx[:, HY_PART_OFF[p]:HY_PART_OFF[p] + HY_PART_CH[p]]
    x0g_ref[0] = (x0 * (gh * jax.nn.sigmoid(gh))).astype(x0g_ref.dtype)
    o_q = O_ATT - N_HY
    za_ref[0, :, 0:D_ATTN] = (z[:, o_q:o_q + D_ATTN] * Q_SCALE_LOG2).astype(za_ref.dtype)
    o_k = o_q + D_ATTN
    o_v = o_k + KV_DIM
    o_ga = o_v + KV_DIM
    ga = z[:, o_ga:o_ga + D_ATTN]
    za_ref[0, :, D_ATTN:2 * D_ATTN] = (ga * jax.nn.sigmoid(ga)).astype(za_ref.dtype)
    kc = z[:, o_k:o_k + KV_DIM]
    ksw = pltpu.roll(kc, HEAD_DIM, axis=1)
    lo = lax.broadcasted_iota(jnp.int32, kc.shape, 1) < HEAD_DIM
    placed = (jnp.where(lo, kc, 0.0), jnp.where(lo, 0.0, ksw),
              jnp.where(lo, ksw, 0.0), jnp.where(lo, 0.0, kc))
    for n, kp in enumerate(placed):
        za_ref[0, :, O_KPL + n * KV_DIM:O_KPL + (n + 1) * KV_DIM] = kp.astype(za_ref.dtype)
    vc = z[:, o_v:o_v + KV_DIM]
    for j in range(tm // BLOCK):
        vt_ref[0, j] = vc[j * BLOCK:(j + 1) * BLOCK, :].T.astype(vt_ref.dtype)


def _inproj(x, pre_g, w_cat, w_short, b_short, tm=1024):
    B, L, D = x.shape
    rb = tm // HALO
    nrb = L // HALO
    grid = (B, L // tm)
    return pl.pallas_call(
        _inproj_kernel,
        grid=grid,
        in_specs=[
            pl.BlockSpec((1, HALO, D), lambda b, t: (b, jnp.maximum(t * rb - 1, 0), 0)),
            pl.BlockSpec((1, tm, D), lambda b, t: (b, t, 0)),
            pl.BlockSpec((1, HALO, D), lambda b, t: (b, jnp.minimum((t + 1) * rb, nrb - 1), 0)),
            pl.BlockSpec((1, D), lambda b, t: (0, 0)),
            pl.BlockSpec((D, D_IN), lambda b, t: (0, 0)),
            pl.BlockSpec((3, N_HY), lambda b, t: (0, 0)),
            pl.BlockSpec((1, N_HY), lambda b, t: (0, 0)),
        ],
        out_specs=(
            [pl.BlockSpec((1, tm, ch), lambda b, t: (b, t, 0)) for ch in HY_PART_CH]
            + [pl.BlockSpec((1, tm, D_HYENA), lambda b, t: (b, t, 0)),
               pl.BlockSpec((1, tm, ZA_W), lambda b, t: (b, t, 0)),
               pl.BlockSpec((1, tm // BLOCK, KV_DIM, BLOCK), lambda b, t: (b, t, 0, 0))]),
        out_shape=(
            [jax.ShapeDtypeStruct((B, L, ch), jnp.bfloat16) for ch in HY_PART_CH]
            + [jax.ShapeDtypeStruct((B, L, D_HYENA), jnp.bfloat16),
               jax.ShapeDtypeStruct((B, L, ZA_W), jnp.bfloat16),
               jax.ShapeDtypeStruct((B, L // BLOCK, KV_DIM, BLOCK), jnp.bfloat16)]),
        compiler_params=pltpu.CompilerParams(
            dimension_semantics=("parallel", "arbitrary"), vmem_limit_bytes=VMEM_LIMIT),
        name="inproj",
    )(x, x, x, pre_g[None, :], w_cat, w_short, b_short[None, :])


def _alibi_slopes():
    return [float(v) for v in
            np.exp2(-8.0 * np.arange(1, N_HEADS + 1, dtype=np.float32) / N_HEADS).astype(np.float32)]


def _attn_bias_tables():
    c = np.arange(BLOCK)[:, None]
    r = np.arange(BLOCK)[None, :]
    dist = np.abs(c - r).astype(np.float32)
    slopes = np.asarray(_alibi_slopes(), np.float32)[:, None, None] * np.float32(LOG2E)
    return np.concatenate([-slopes * dist, -slopes * (WINDOW - dist)], axis=1).astype(np.float32)


def _back_kernel(sink_ref, bias_ref, mask_ref, za_ref, vt_ref, x_ref, *refs):
    yh_refs = refs[:HY_PARTS]
    x0g_ref, w_ref, g_ref, o_ref, s_ref, ot_ref, ya_ref = refs[HY_PARTS:]
    t = pl.program_id(1)
    tm = x_ref.shape[1]
    L = za_ref.shape[1]
    nb = L // BLOCK
    nbs = tm // BLOCK
    i0 = t * nbs
    o_g = D_ATTN
    f32 = jnp.float32
    bf16 = jnp.bfloat16
    slopes = _alibi_slopes()
    group = Q_PER_KV * HEAD_DIM

    c_i = lax.broadcasted_iota(jnp.int32, (BLOCK, BLOCK), 0)
    r_i = lax.broadcasted_iota(jnp.int32, (BLOCK, BLOCK), 1)
    upper = c_i >= r_i
    diag = c_i == r_i
    ones_rows = jnp.ones((16, 3 * BLOCK), bf16)

    def key_tile(j):
        return jnp.clip(j, 0, nb - 1)

    def scores(i, slot):
        start = pl.multiple_of(i * BLOCK, BLOCK)
        qblk = za_ref[0, pl.ds(start, BLOCK), 0:D_ATTN]
        krows = [pl.ds(pl.multiple_of(key_tile(i + tt - 1) * BLOCK, BLOCK), BLOCK)
                 for tt in range(3)]
        for kv in range(N_KV_HEADS):
            base = kv * group
            qs = jnp.concatenate([qblk[:, base:base + 2 * HEAD_DIM],
                                  qblk[:, base + 2 * HEAD_DIM:base + group]], axis=0)
            kp = jnp.concatenate(
                [za_ref[0, krows[tt], O_KPL + (2 * kv + half) * KV_DIM:
                        O_KPL + (2 * kv + half + 1) * KV_DIM]
                 for half in range(2) for tt in range(3)], axis=0)
            s_ref[slot, kv] = lax.dot_general(kp, qs, (((1,), (1,)), ((), ())),
                                              preferred_element_type=f32)

    def softmax_pv(i, slot, edge):
        pen1 = jnp.where(i == 0, NEG_INF, 0.0).astype(f32)
        pen3 = jnp.where(i == nb - 1, NEG_INF, 0.0).astype(f32)
        for kv in range(N_KV_HEADS):
            rows = slice(kv * HEAD_DIM, (kv + 1) * HEAD_DIM)
            vt3 = vt_ref[0, key_tile(i + 1), rows, :]
            vtw = jnp.concatenate([vt_ref[0, key_tile(i - 1), rows, :], vt_ref[0, i, rows, :], vt3],
                                  axis=1)
            pts, extras, pxs = [], [], []
            for g in range(Q_PER_KV):
                half, pair = g % 2, g // 2
                h = kv * Q_PER_KV + g
                r0 = half * 3 * BLOCK
                cols = slice(pair * BLOCK, (pair + 1) * BLOCK)
                s1 = s_ref[slot, kv, r0:r0 + BLOCK, cols]
                s2 = s_ref[slot, kv, r0 + BLOCK:r0 + 2 * BLOCK, cols]
                s3 = s_ref[slot, kv, r0 + 2 * BLOCK:r0 + 3 * BLOCK, cols]
                if edge == "lo":
                    mrg = jnp.where(upper, s1 + pen1, s3)
                elif edge == "hi":
                    mrg = jnp.where(upper, s1, s3 + pen3)
                else:
                    mrg = jnp.where(upper, s1, s3)
                a_mid = s2 + bias_ref[h, 0:BLOCK, :]
                a_mrg = mrg + bias_ref[h, BLOCK:, :]
                sink2 = sink_ref[h] * LOG2E
                s_x = (jnp.sum(jnp.where(diag, s3, 0.0), axis=0, keepdims=True)
                       - slopes[h] * LOG2E * WINDOW)
                if edge == "hi":
                    s_x = s_x + pen3
                m = jnp.max(jnp.maximum(a_mid, a_mrg), axis=0, keepdims=True)
                m = jnp.maximum(jnp.maximum(m, sink2), s_x)
                p_mid = jnp.exp2(a_mid - m).astype(bf16)
                p_mrg = jnp.exp2(a_mrg - m).astype(bf16)
                p_x = jnp.exp2(s_x - m)
                pxs.append(p_x)
                extras.append(jnp.exp2(sink2 - m) + p_x)
                p1 = p_mrg * mask_ref[...]
                p3 = p_mrg - p1
                pts.append(jnp.concatenate([p1, p_mid, p3], axis=0))
                if len(pts) < PV_HEADS:
                    continue
                pt = jnp.concatenate(pts, axis=1)
                ot = jnp.dot(jnp.concatenate([vtw, ones_rows], axis=0), pt,
                             preferred_element_type=f32)
                for n in range(PV_HEADS):
                    hh = h - (PV_HEADS - 1) + n
                    cols = slice(n * BLOCK, (n + 1) * BLOCK)
                    o = ot[0:HEAD_DIM, cols] + vt3.astype(f32) * pxs[n]
                    inv = 1.0 / (ot[HEAD_DIM:HEAD_DIM + 1, cols] + extras[n])
                    ot_ref[slot, hh * HEAD_DIM:(hh + 1) * HEAD_DIM, :] = o * inv
                pts, extras, pxs = [], [], []

    def finish(i, slot):
        start = pl.multiple_of(i * BLOCK, BLOCK)
        local = pl.multiple_of((i - i0) * BLOCK, BLOCK)
        gate = za_ref[0, pl.ds(start, BLOCK), o_g:o_g + D_ATTN].astype(f32)
        ya_ref[pl.ds(local, BLOCK), :] = (ot_ref[slot].T * gate).astype(ya_ref.dtype)

    def two_blocks(j):
        i = i0 + 2 * j + 1
        scores(i + 1, 0)
        softmax_pv(i, 1, None)
        finish(i - 1, 0)
        scores(i + 2, 1)
        softmax_pv(i + 1, 0, None)
        finish(i, 1)

    def project(c0, c1):
        r = slice(2 * c0 * BLOCK, 2 * c1 * BLOCK)
        x0g = x0g_ref[0, r, :].astype(f32)
        yh = jnp.concatenate([ref[0, r, :] for ref in yh_refs], axis=1).astype(f32)
        y = jnp.dot(jnp.concatenate([(yh * x0g).astype(bf16), ya_ref[r, :]], axis=1), w_ref[...],
                    preferred_element_type=f32)
        ms = jnp.mean(y * y, axis=-1, keepdims=True)
        o_ref[0, r, :] = x_ref[0, r, :] + y * lax.rsqrt(ms + RMS_EPS) * g_ref[...]

    scores(i0, 0)
    scores(i0 + 1, 1)
    softmax_pv(i0, 0, "lo")
    npairs = nbs // 2
    half = npairs // 2
    for j in range(npairs - 1):
        two_blocks(j)
        if j == half:
            project(0, half)
    softmax_pv(i0 + nbs - 1, 1, "hi")
    finish(i0 + nbs - 2, 0)
    finish(i0 + nbs - 1, 1)
    project(half, npairs)


def _back(za, vt, sink, x, yh_parts, x0g, w_out, post_g, tm=1024):
    B, L, D = x.shape
    nb = L // BLOCK
    nbs = tm // BLOCK
    assert WINDOW == BLOCK and nbs >= 4 and nbs % 2 == 0 and L % tm == 0
    bias = jnp.asarray(_attn_bias_tables())
    upper = jnp.asarray(np.triu(np.ones((BLOCK, BLOCK), np.float32)).T, jnp.bfloat16)
    return pl.pallas_call(
        _back_kernel,
        grid=(B, L // tm),
        in_specs=[
            pl.BlockSpec(memory_space=pltpu.SMEM),
            pl.BlockSpec((N_HEADS, 2 * BLOCK, BLOCK), lambda b, t: (0, 0, 0)),
            pl.BlockSpec((BLOCK, BLOCK), lambda b, t: (0, 0)),
            pl.BlockSpec((1, L, ZA_W), lambda b, t: (b, 0, 0)),
            pl.BlockSpec((1, nb, KV_DIM, BLOCK), lambda b, t: (b, 0, 0, 0)),
            pl.BlockSpec((1, tm, D), lambda b, t: (b, t, 0)),
        ] + [pl.BlockSpec((1, tm, ch), lambda b, t: (b, t, 0)) for ch in HY_PART_CH] + [
            pl.BlockSpec((1, tm, D_HYENA), lambda b, t: (b, t, 0)),
            pl.BlockSpec((D_HYENA + D_ATTN, D), lambda b, t: (0, 0)),
            pl.BlockSpec((1, D), lambda b, t: (0, 0)),
        ],
        out_specs=pl.BlockSpec((1, tm, D), lambda b, t: (b, t, 0)),
        out_shape=jax.ShapeDtypeStruct((B, L, D), jnp.float32),
        scratch_shapes=[
            pltpu.VMEM((2, N_KV_HEADS, 6 * BLOCK, 2 * BLOCK), jnp.float32),
            pltpu.VMEM((2, D_ATTN, BLOCK), jnp.float32),
            pltpu.VMEM((tm, D_ATTN), jnp.bfloat16),
        ],
        compiler_params=pltpu.CompilerParams(
            dimension_semantics=("parallel", "parallel"), vmem_limit_bytes=VMEM_LIMIT),
        name="attn_outproj",
    )(sink, bias, upper, za, vt, x, *yh_parts, x0g, w_out, post_g[None, :])


def _hyena_kernel(k_ref, d_ref, v_ref, y_ref, *scratch):
    tables = scratch[:HY_UNROLL]
    accs = scratch[HY_UNROLL:]
    step = pl.program_id(0)
    cg = v_ref.shape[0]
    rows = v_ref.shape[1]
    nchan, n2 = k_ref.shape
    nblk = n2 // (2 * TBLK)
    bsz = rows // nblk
    off = TBLK * (nblk - 1) + SHIFT_ROWS

    def build_table(chan, s_ref):
        krow = k_ref[pl.ds(jnp.minimum(chan, nchan - 1), 1), :]
        kb = jnp.broadcast_to(krow, (SHIFT_ROWS, n2))
        s_ref[...] = pltpu.roll(kb, off, axis=1, stride=1, stride_axis=0).astype(s_ref.dtype)

    def convolve(ci, s_ref, acc_ref):
        acc_ref[...] = d_ref[pl.ds(ci, 1), :] * v_ref[ci].astype(jnp.float32)
        for d in range(-(nblk - 1), nblk):
            x0 = TBLK * (d + nblk - 1)
            w = jnp.concatenate(
                [s_ref[:, x0 + SHIFT_ROWS:x0 + SHIFT_ROWS + TBLK], s_ref[:, x0:x0 + TBLK]], axis=0)
            n = (nblk - abs(d)) * bsz
            src = 0 if d >= 0 else -d * bsz
            dst = d * bsz if d >= 0 else 0
            acc_ref[dst:dst + n, :] += jnp.dot(v_ref[ci, pl.ds(src, n), :], w,
                                               preferred_element_type=jnp.float32)
        y_ref[ci] = acc_ref[...].astype(y_ref.dtype)

    @pl.when(step == 0)
    def _():
        for u in range(HY_UNROLL):
            build_table(u, tables[u])

    def body(it, carry):
        for u in range(HY_UNROLL):
            convolve(it * HY_UNROLL + u, tables[u], accs[u])
        for u in range(HY_UNROLL):
            build_table(step * cg + (it + 1) * HY_UNROLL + u, tables[u])
        return carry

    lax.fori_loop(0, cg // HY_UNROLL, body, 0)


def _hyena_conv(kt, hyena_d, v_rows, off, cg=16):
    C, rows, _ = v_rows.shape
    n2 = kt.shape[1]
    assert off % C == 0 and C % cg == 0
    return pl.pallas_call(
        _hyena_kernel,
        grid=(C // cg,),
        in_specs=[
            pl.BlockSpec((C, n2), lambda c: (off // C, 0)),
            pl.BlockSpec((cg, 1), lambda c: (c + off // cg, 0)),
            pl.BlockSpec((cg, rows, TBLK), lambda c: (c, 0, 0)),
        ],
        out_specs=pl.BlockSpec((cg, rows, TBLK), lambda c: (c, 0, 0)),
        out_shape=jax.ShapeDtypeStruct((C, rows, TBLK), jnp.bfloat16),
        scratch_shapes=(
            [pltpu.VMEM((SHIFT_ROWS, n2), jnp.bfloat16)] * HY_UNROLL
            + [pltpu.VMEM((rows, TBLK), jnp.float32)] * HY_UNROLL),
        compiler_params=pltpu.CompilerParams(
            dimension_semantics=("arbitrary",), vmem_limit_bytes=VMEM_LIMIT),
        name="hyena_conv",
    )(kt, hyena_d[:, None], v_rows)


def _layer(x, pre_g, w_in, w_short, b_short, w_f1, b_f1, w_f2, b_f2, w_f3, b_f3, w_f4,
           sin_freq, hyena_d, attn_sink, w_out, post_g):
    B, L, _ = x.shape
    nblk = L // TBLK
    kt = _hyena_filter(L, w_f1, b_f1, w_f2, b_f2, w_f3, b_f3, w_f4, sin_freq)
    *v_parts, x0g, za, vt = _inproj(x, pre_g, w_in.astype(jnp.bfloat16), w_short, b_short)
    yh = []
    for v, pc, off in zip(v_parts, HY_PART_CH, HY_PART_OFF):
        v_rows = v.reshape(B, nblk, TBLK, pc).transpose(3, 1, 0, 2).reshape(pc, nblk * B, TBLK)
        y_rows = _hyena_conv(kt, hyena_d, v_rows, off)
        yh.append(y_rows.reshape(pc, nblk, B, TBLK).transpose(2, 1, 3, 0).reshape(B, L, pc))
    return _back(za, vt, attn_sink, x, yh, x0g, w_out.astype(jnp.bfloat16), post_g)


def kernel(x, pre_g, w_in, w_short, b_short, w_f1, b_f1, w_f2, b_f2, w_f3, b_f3, w_f4, sin_freq, hyena_d, attn_sink, w_out, post_g):
    depth = pre_g.shape[0]
    for l in range(depth):
        x = _layer(x, pre_g[l], w_in[l], w_short[l], b_short[l], w_f1[l], b_f1[l], w_f2[l],
                   b_f2[l], w_f3[l], b_f3[l], w_f4[l], sin_freq[l], hyena_d[l], attn_sink[l],
                   w_out[l], post_g[l])
    return x
```

```python
import math

import jax
import jax.numpy as jnp
import numpy as np
from jax import lax
from jax.experimental import pallas as pl
from jax.experimental.pallas import tpu as pltpu

D_MODEL = 1024
D_HYENA = 512
D_ATTN = 512
N_HEADS = 8
HEAD_DIM = 64
N_KV_HEADS = 2
Q_PER_KV = N_HEADS // N_KV_HEADS
KV_DIM = N_KV_HEADS * HEAD_DIM
WINDOW = 128
BLOCK = 128
FILTER_ORDER = 64
N_BANDS = 16
POS_EMB_DIM = 1 + 2 * N_BANDS
POS_EMB_PAD = 40
DECAY_TARGET = 1e-2
FAST_DECAY_PCT = 0.3
SLOW_DECAY_PCT = 1.5
RMS_EPS = 1e-6
NEG_INF = -1e30
PV_HEADS = 2
LOG2E = math.log2(math.e)
Q_SCALE_LOG2 = HEAD_DIM ** -0.5 * LOG2E

N_HY = 3 * D_HYENA
O_GH = N_HY
O_ATT = O_GH + D_HYENA
N_ATT = 2 * D_ATTN + 2 * KV_DIM
D_IN = O_ATT + N_ATT
O_KPL = 2 * D_ATTN
ZA_W = O_KPL + 4 * KV_DIM

LANES = 128
HALO = 16

TBLK = 256
SHIFT_ROWS = 128
HY_UNROLL = 2
HY_PARTS = 4
HY_PART_CH = D_HYENA // HY_PARTS

VMEM_LIMIT = 56 * 1024 * 1024

_HI = lax.Precision.HIGHEST


def _dot_bf16x3(a, b):
    f32, bf16 = jnp.float32, jnp.bfloat16
    a_hi = a.astype(bf16)
    a_lo = (a - a_hi.astype(f32)).astype(bf16)
    b_hi = b.astype(bf16)
    b_lo = (b - b_hi.astype(f32)).astype(bf16)

    def dot(x, y):
        return jnp.dot(x, y, preferred_element_type=f32)

    return dot(a_hi, b_hi) + (dot(a_hi, b_lo) + dot(a_lo, b_hi))


def _filter_consts(L):
    t = np.arange(L, dtype=np.float32)
    t_norm = t / np.float32(max(L - 1, 1))
    w = np.float32(2.0 * math.pi) * t / np.float32(L)
    bands = np.linspace(1e-4, N_BANDS - 1, N_BANDS).astype(np.float32)
    ang = w[:, None] * bands[None, :]
    z = np.concatenate([t_norm[:, None], np.cos(ang), -np.sin(ang)], axis=-1)
    min_decay = math.log(DECAY_TARGET) / SLOW_DECAY_PCT
    max_decay = math.log(DECAY_TARGET) / FAST_DECAY_PCT
    deltas = np.abs(np.linspace(min_decay, max_decay, D_HYENA)).astype(np.float32)
    decay = np.exp(-t_norm[:, None] * deltas[None, :]).astype(np.float32)
    pad = np.zeros((POS_EMB_PAD - POS_EMB_DIM, L), np.float32)
    zt = np.concatenate([z.T, pad], axis=0)
    flip = np.eye(LANES, dtype=np.float32)[::-1]
    return zt, np.ascontiguousarray(decay.T), np.ascontiguousarray(flip)


def _filter_kernel(zt_ref, dec_ref, flip_ref, p_ref, w4_ref, k_ref):
    L = zt_ref.shape[1]
    f32 = jnp.float32
    w1 = p_ref[:, 0:POS_EMB_PAD]
    w2 = p_ref[:, LANES:LANES + FILTER_ORDER]
    w3 = p_ref[:, 2 * LANES:2 * LANES + FILTER_ORDER]
    vec = p_ref[:, 3 * LANES:4 * LANES]
    w4 = w4_ref[...].T
    h = jnp.dot(w1, zt_ref[...], precision=_HI, preferred_element_type=f32)
    h = jnp.sin(vec[:, 3:4] * (h + vec[:, 0:1]))
    h = jnp.dot(w2, h, precision=_HI, preferred_element_type=f32)
    h = jnp.sin(vec[:, 4:5] * (h + vec[:, 1:2]))
    h = jnp.dot(w3, h, precision=_HI, preferred_element_type=f32)
    h = jnp.sin(vec[:, 5:6] * (h + vec[:, 2:3]))
    hf = _dot_bf16x3(w4[0:D_HYENA, :], h) * dec_ref[...]
    hb = _dot_bf16x3(w4[D_HYENA:, :], h) * dec_ref[...]
    nt = L // LANES
    rev = jnp.concatenate(
        [jnp.dot(hb[:, (nt - 1 - j) * LANES:(nt - j) * LANES], flip_ref[...], precision=_HI,
                 preferred_element_type=f32) for j in range(nt)], axis=1)
    col = lax.broadcasted_iota(jnp.int32, rev.shape, 1)
    hb2 = jnp.where(col == 0, 0.0, pltpu.roll(rev, 1, axis=1))
    ss = jnp.sum(hf * hf, axis=1, keepdims=True) + jnp.sum(hb2 * hb2, axis=1, keepdims=True)
    inv = lax.rsqrt(ss + 1e-12)
    k_ref[:, 0:L] = hf * inv
    k_ref[:, L:2 * L] = hb2 * inv


def _hyena_filter(L, w_f1, b_f1, w_f2, b_f2, w_f3, b_f3, w_f4, sin_freq):
    zt, dec, flip = _filter_consts(L)
    def tile(a):
        return jnp.pad(a, ((0, 0), (0, LANES - a.shape[1])))

    vectors = jnp.stack([b_f1, b_f2, b_f3, sin_freq[0], sin_freq[1], sin_freq[2]], axis=1)
    params = jnp.concatenate([tile(w_f1.T), tile(w_f2.T), tile(w_f3.T), tile(vectors)], axis=1)
    args = (jnp.asarray(zt), jnp.asarray(dec), jnp.asarray(flip), params, w_f4)
    return pl.pallas_call(
        _filter_kernel,
        out_shape=jax.ShapeDtypeStruct((D_HYENA, 2 * L), jnp.float32),
        compiler_params=pltpu.CompilerParams(vmem_limit_bytes=VMEM_LIMIT),
        name="hyena_filter",
    )(*args)


def _inproj_kernel(xp_ref, x_ref, xn_ref, g_ref, w_ref, wsh_ref, bsh_ref, *out_refs):
    v_refs = out_refs[:HY_PARTS]
    x0g_ref, za_ref, vt_ref = out_refs[HY_PARTS:]
    t = pl.program_id(1)
    nt = pl.num_programs(1)
    tm = x_ref.shape[1]

    def norm(xv):
        ms = jnp.mean(xv * xv, axis=-1, keepdims=True)
        return (xv * lax.rsqrt(ms + RMS_EPS) * g_ref[...]).astype(jnp.bfloat16)

    h = norm(x_ref[0])
    hp = jnp.where(t > 0, norm(xp_ref[0]), 0.0).astype(jnp.bfloat16)
    hn = jnp.where(t < nt - 1, norm(xn_ref[0]), 0.0).astype(jnp.bfloat16)
    hall = jnp.concatenate([hp, h, hn], axis=0)
    zh = jnp.dot(hall, w_ref[:, 0:N_HY], preferred_element_type=jnp.float32)
    z = jnp.dot(h, w_ref[:, N_HY:], preferred_element_type=jnp.float32)

    rows = tm + 2 * HALO
    u_prev = pltpu.roll(zh, 1, axis=0)[HALO:HALO + tm]
    u = zh[HALO:HALO + tm]
    u_next = pltpu.roll(zh, rows - 1, axis=0)[HALO:HALO + tm]
    uc = (u_prev * wsh_ref[0:1, :] + u * wsh_ref[1:2, :] + u_next * wsh_ref[2:3, :]
          + bsh_ref[...])
    x0 = uc[:, 0:D_HYENA]
    x1 = uc[:, D_HYENA:2 * D_HYENA]
    vv = uc[:, 2 * D_HYENA:]
    gh = z[:, 0:D_HYENA]
    vx = (vv * x1).astype(v_refs[0].dtype)
    for p, v_ref in enumerate(v_refs):
        v_ref[0] = vx[:, p * HY_PART_CH:(p + 1) * HY_PART_CH]
    x0g_ref[0] = (x0 * (gh * jax.nn.sigmoid(gh))).astype(x0g_ref.dtype)
    o_q = O_ATT - N_HY
    za_ref[0, :, 0:D_ATTN] = (z[:, o_q:o_q + D_ATTN] * Q_SCALE_LOG2).astype(za_ref.dtype)
    o_k = o_q + D_ATTN
    o_v = o_k + KV_DIM
    o_ga = o_v + KV_DIM
    ga = z[:, o_ga:o_ga + D_ATTN]
    za_ref[0, :, D_ATTN:2 * D_ATTN] = (ga * jax.nn.sigmoid(ga)).astype(za_ref.dtype)
    kc = z[:, o_k:o_k + KV_DIM]
    ksw = pltpu.roll(kc, HEAD_DIM, axis=1)
    lo = lax.broadcasted_iota(jnp.int32, kc.shape, 1) < HEAD_DIM
    placed = (jnp.where(lo, kc, 0.0), jnp.where(lo, 0.0, ksw),
              jnp.where(lo, ksw, 0.0), jnp.where(lo, 0.0, kc))
    for n, kp in enumerate(placed):
        za_ref[0, :, O_KPL + n * KV_DIM:O_KPL + (n + 1) * KV_DIM] = kp.astype(za_ref.dtype)
    vc = z[:, o_v:o_v + KV_DIM]
    for j in range(tm // BLOCK):
        vt_ref[0, j] = vc[j * BLOCK:(j + 1) * BLOCK, :].T.astype(vt_ref.dtype)


def _inproj(x, pre_g, w_cat, w_short, b_short, tm=1024):
    B, L, D = x.shape
    rb = tm // HALO
    nrb = L // HALO
    grid = (B, L // tm)
    return pl.pallas_call(
        _inproj_kernel,
        grid=grid,
        in_specs=[
            pl.BlockSpec((1, HALO, D), lambda b, t: (b, jnp.maximum(t * rb - 1, 0), 0)),
            pl.BlockSpec((1, tm, D), lambda b, t: (b, t, 0)),
            pl.BlockSpec((1, HALO, D), lambda b, t: (b, jnp.minimum((t + 1) * rb, nrb - 1), 0)),
            pl.BlockSpec((1, D), lambda b, t: (0, 0)),
            pl.BlockSpec((D, D_IN), lambda b, t: (0, 0)),
            pl.BlockSpec((3, N_HY), lambda b, t: (0, 0)),
            pl.BlockSpec((1, N_HY), lambda b, t: (0, 0)),
        ],
        out_specs=(
            [pl.BlockSpec((1, tm, HY_PART_CH), lambda b, t: (b, t, 0))] * HY_PARTS
            + [pl.BlockSpec((1, tm, D_HYENA), lambda b, t: (b, t, 0)),
               pl.BlockSpec((1, tm, ZA_W), lambda b, t: (b, t, 0)),
               pl.BlockSpec((1, tm // BLOCK, KV_DIM, BLOCK), lambda b, t: (b, t, 0, 0))]),
        out_shape=(
            [jax.ShapeDtypeStruct((B, L, HY_PART_CH), jnp.bfloat16)] * HY_PARTS
            + [jax.ShapeDtypeStruct((B, L, D_HYENA), jnp.bfloat16),
               jax.ShapeDtypeStruct((B, L, ZA_W), jnp.bfloat16),
               jax.ShapeDtypeStruct((B, L // BLOCK, KV_DIM, BLOCK), jnp.bfloat16)]),
        compiler_params=pltpu.CompilerParams(
            dimension_semantics=("parallel", "arbitrary"), vmem_limit_bytes=VMEM_LIMIT),
        name="inproj",
    )(x, x, x, pre_g[None, :], w_cat, w_short, b_short[None, :])


def _alibi_slopes():
    return [float(v) for v in
            np.exp2(-8.0 * np.arange(1, N_HEADS + 1, dtype=np.float32) / N_HEADS).astype(np.float32)]


def _attn_bias_tables():
    c = np.arange(BLOCK)[:, None]
    r = np.arange(BLOCK)[None, :]
    dist = np.abs(c - r).astype(np.float32)
    slopes = np.asarray(_alibi_slopes(), np.float32)[:, None, None] * np.float32(LOG2E)
    return np.concatenate([-slopes * dist, -slopes * (WINDOW - dist)], axis=1).astype(np.float32)


def _back_kernel(sink_ref, bias_ref, mask_ref, za_ref, vt_ref, x_ref, *refs):
    yh_refs = refs[:HY_PARTS]
    x0g_ref, w_ref, g_ref, o_ref, s_ref, ot_ref, ya_ref = refs[HY_PARTS:]
    t = pl.program_id(1)
    tm = x_ref.shape[1]
    L = za_ref.shape[1]
    nb = L // BLOCK
    nbs = tm // BLOCK
    i0 = t * nbs
    o_g = D_ATTN
    f32 = jnp.float32
    bf16 = jnp.bfloat16
    slopes = _alibi_slopes()
    group = Q_PER_KV * HEAD_DIM

    c_i = lax.broadcasted_iota(jnp.int32, (BLOCK, BLOCK), 0)
    r_i = lax.broadcasted_iota(jnp.int32, (BLOCK, BLOCK), 1)
    upper = c_i >= r_i
    diag = c_i == r_i
    ones_rows = jnp.ones((16, 3 * BLOCK), bf16)

    def key_tile(j):
        return jnp.clip(j, 0, nb - 1)

    def scores(i, slot):
        start = pl.multiple_of(i * BLOCK, BLOCK)
        qblk = za_ref[0, pl.ds(start, BLOCK), 0:D_ATTN]
        krows = [pl.ds(pl.multiple_of(key_tile(i + tt - 1) * BLOCK, BLOCK), BLOCK)
                 for tt in range(3)]
        for kv in range(N_KV_HEADS):
            base = kv * group
            qs = jnp.concatenate([qblk[:, base:base + 2 * HEAD_DIM],
                                  qblk[:, base + 2 * HEAD_DIM:base + group]], axis=0)
            kp = jnp.concatenate(
                [za_ref[0, krows[tt], O_KPL + (2 * kv + half) * KV_DIM:
                        O_KPL + (2 * kv + half + 1) * KV_DIM]
                 for half in range(2) for tt in range(3)], axis=0)
            s_ref[slot, kv] = lax.dot_general(kp, qs, (((1,), (1,)), ((), ())),
                                              preferred_element_type=f32)

    def softmax_pv(i, slot, edge):
        pen1 = jnp.where(i == 0, NEG_INF, 0.0).astype(f32)
        pen3 = jnp.where(i == nb - 1, NEG_INF, 0.0).astype(f32)
        for kv in range(N_KV_HEADS):
            rows = slice(kv * HEAD_DIM, (kv + 1) * HEAD_DIM)
            vt3 = vt_ref[0, key_tile(i + 1), rows, :]
            vtw = jnp.concatenate([vt_ref[0, key_tile(i - 1), rows, :], vt_ref[0, i, rows, :], vt3],
                                  axis=1)
            pts, extras, pxs = [], [], []
            for g in range(Q_PER_KV):
                half, pair = g % 2, g // 2
                h = kv * Q_PER_KV + g
                r0 = half * 3 * BLOCK
                cols = slice(pair * BLOCK, (pair + 1) * BLOCK)
                s1 = s_ref[slot, kv, r0:r0 + BLOCK, cols]
                s2 = s_ref[slot, kv, r0 + BLOCK:r0 + 2 * BLOCK, cols]
                s3 = s_ref[slot, kv, r0 + 2 * BLOCK:r0 + 3 * BLOCK, cols]
                if edge == "lo":
                    mrg = jnp.where(upper, s1 + pen1, s3)
                elif edge == "hi":
                    mrg = jnp.where(upper, s1, s3 + pen3)
                else:
                    mrg = jnp.where(upper, s1, s3)
                a_mid = s2 + bias_ref[h, 0:BLOCK, :]
                a_mrg = mrg + bias_ref[h, BLOCK:, :]
                sink2 = sink_ref[h] * LOG2E
                s_x = (jnp.sum(jnp.where(diag, s3, 0.0), axis=0, keepdims=True)
                       - slopes[h] * LOG2E * WINDOW)
                if edge == "hi":
                    s_x = s_x + pen3
                m = jnp.max(jnp.maximum(a_mid, a_mrg), axis=0, keepdims=True)
                m = jnp.maximum(jnp.maximum(m, sink2), s_x)
                p_mid = jnp.exp2(a_mid - m).astype(bf16)
                p_mrg = jnp.exp2(a_mrg - m).astype(bf16)
                p_x = jnp.exp2(s_x - m)
                pxs.append(p_x)
                extras.append(jnp.exp2(sink2 - m) + p_x)
                p1 = p_mrg * mask_ref[...]
                p3 = p_mrg - p1
                pts.append(jnp.concatenate([p1, p_mid, p3], axis=0))
                if len(pts) < PV_HEADS:
                    continue
                pt = jnp.concatenate(pts, axis=1)
                ot = jnp.dot(jnp.concatenate([vtw, ones_rows], axis=0), pt,
                             preferred_element_type=f32)
                for n in range(PV_HEADS):
                    hh = h - (PV_HEADS - 1) + n
                    cols = slice(n * BLOCK, (n + 1) * BLOCK)
                    o = ot[0:HEAD_DIM, cols] + vt3.astype(f32) * pxs[n]
                    inv = 1.0 / (ot[HEAD_DIM:HEAD_DIM + 1, cols] + extras[n])
                    ot_ref[slot, hh * HEAD_DIM:(hh + 1) * HEAD_DIM, :] = o * inv
                pts, extras, pxs = [], [], []

    def finish(i, slot):
        start = pl.multiple_of(i * BLOCK, BLOCK)
        local = pl.multiple_of((i - i0) * BLOCK, BLOCK)
        gate = za_ref[0, pl.ds(start, BLOCK), o_g:o_g + D_ATTN].astype(f32)
        ya_ref[pl.ds(local, BLOCK), :] = (ot_ref[slot].T * gate).astype(ya_ref.dtype)

    def two_blocks(j):
        i = i0 + 2 * j + 1
        scores(i + 1, 0)
        softmax_pv(i, 1, None)
        finish(i - 1, 0)
        scores(i + 2, 1)
        softmax_pv(i + 1, 0, None)
        finish(i, 1)

    def project(c0, c1):
        r = slice(2 * c0 * BLOCK, 2 * c1 * BLOCK)
        x0g = x0g_ref[0, r, :].astype(f32)
        yh = jnp.concatenate([ref[0, r, :] for ref in yh_refs], axis=1).astype(f32)
        y = jnp.dot(jnp.concatenate([(yh * x0g).astype(bf16), ya_ref[r, :]], axis=1), w_ref[...],
                    preferred_element_type=f32)
        ms = jnp.mean(y * y, axis=-1, keepdims=True)
        o_ref[0, r, :] = x_ref[0, r, :] + y * lax.rsqrt(ms + RMS_EPS) * g_ref[...]

    scores(i0, 0)
    scores(i0 + 1, 1)
    softmax_pv(i0, 0, "lo")
    npairs = nbs // 2
    half = npairs // 2
    for j in range(npairs - 1):
        two_blocks(j)
        if j == half:
            project(0, half)
    softmax_pv(i0 + nbs - 1, 1, "hi")
    finish(i0 + nbs - 2, 0)
    finish(i0 + nbs - 1, 1)
    project(half, npairs)


def _back(za, vt, sink, x, yh_parts, x0g, w_out, post_g, tm=1024):
    B, L, D = x.shape
    nb = L // BLOCK
    nbs = tm // BLOCK
    assert WINDOW == BLOCK and nbs >= 4 and nbs % 2 == 0 and L % tm == 0
    bias = jnp.asarray(_attn_bias_tables())
    upper = jnp.asarray(np.triu(np.ones((BLOCK, BLOCK), np.float32)).T, jnp.bfloat16)
    return pl.pallas_call(
        _back_kernel,
        grid=(B, L // tm),
        in_specs=[
            pl.BlockSpec(memory_space=pltpu.SMEM),
            pl.BlockSpec((N_HEADS, 2 * BLOCK, BLOCK), lambda b, t: (0, 0, 0)),
            pl.BlockSpec((BLOCK, BLOCK), lambda b, t: (0, 0)),
            pl.BlockSpec((1, L, ZA_W), lambda b, t: (b, 0, 0)),
            pl.BlockSpec((1, nb, KV_DIM, BLOCK), lambda b, t: (b, 0, 0, 0)),
            pl.BlockSpec((1, tm, D), lambda b, t: (b, t, 0)),
        ] + [pl.BlockSpec((1, tm, HY_PART_CH), lambda b, t: (b, t, 0))] * HY_PARTS + [
            pl.BlockSpec((1, tm, D_HYENA), lambda b, t: (b, t, 0)),
            pl.BlockSpec((D_HYENA + D_ATTN, D), lambda b, t: (0, 0)),
            pl.BlockSpec((1, D), lambda b, t: (0, 0)),
        ],
        out_specs=pl.BlockSpec((1, tm, D), lambda b, t: (b, t, 0)),
        out_shape=jax.ShapeDtypeStruct((B, L, D), jnp.float32),
        scratch_shapes=[
            pltpu.VMEM((2, N_KV_HEADS, 6 * BLOCK, 2 * BLOCK), jnp.float32),
            pltpu.VMEM((2, D_ATTN, BLOCK), jnp.float32),
            pltpu.VMEM((tm, D_ATTN), jnp.bfloat16),
        ],
        compiler_params=pltpu.CompilerParams(
            dimension_semantics=("parallel", "parallel"), vmem_limit_bytes=VMEM_LIMIT),
        name="attn_outproj",
    )(sink, bias, upper, za, vt, x, *yh_parts, x0g, w_out, post_g[None, :])


def _hyena_kernel(k_ref, d_ref, v_ref, y_ref, *scratch):
    tables = scratch[:HY_UNROLL]
    accs = scratch[HY_UNROLL:]
    step = pl.program_id(0)
    cg = v_ref.shape[0]
    rows = v_ref.shape[1]
    nchan, n2 = k_ref.shape
    nblk = n2 // (2 * TBLK)
    bsz = rows // nblk
    off = TBLK * (nblk - 1) + SHIFT_ROWS

    def build_table(chan, s_ref):
        krow = k_ref[pl.ds(jnp.minimum(chan, nchan - 1), 1), :]
        kb = jnp.broadcast_to(krow, (SHIFT_ROWS, n2))
        s_ref[...] = pltpu.roll(kb, off, axis=1, stride=1, stride_axis=0).astype(s_ref.dtype)

    def convolve(ci, s_ref, acc_ref):
        acc_ref[...] = d_ref[pl.ds(ci, 1), :] * v_ref[ci].astype(jnp.float32)
        for d in range(-(nblk - 1), nblk):
            x0 = TBLK * (d + nblk - 1)
            w = jnp.concatenate(
                [s_ref[:, x0 + SHIFT_ROWS:x0 + SHIFT_ROWS + TBLK], s_ref[:, x0:x0 + TBLK]], axis=0)
            n = (nblk - abs(d)) * bsz
            src = 0 if d >= 0 else -d * bsz
            dst = d * bsz if d >= 0 else 0
            acc_ref[dst:dst + n, :] += jnp.dot(v_ref[ci, pl.ds(src, n), :], w,
                                               preferred_element_type=jnp.float32)
        y_ref[ci] = acc_ref[...].astype(y_ref.dtype)

    @pl.when(step == 0)
    def _():
        for u in range(HY_UNROLL):
            build_table(u, tables[u])

    def body(it, carry):
        for u in range(HY_UNROLL):
            convolve(it * HY_UNROLL + u, tables[u], accs[u])
        for u in range(HY_UNROLL):
            build_table(step * cg + (it + 1) * HY_UNROLL + u, tables[u])
        return carry

    lax.fori_loop(0, cg // HY_UNROLL, body, 0)


def _hyena_conv(kt, hyena_d, v_rows, part, cg=16):
    C, rows, _ = v_rows.shape
    n2 = kt.shape[1]
    steps = C // cg
    return pl.pallas_call(
        _hyena_kernel,
        grid=(steps,),
        in_specs=[
            pl.BlockSpec((C, n2), lambda c: (part, 0)),
            pl.BlockSpec((cg, 1), lambda c: (c + part * steps, 0)),
            pl.BlockSpec((cg, rows, TBLK), lambda c: (c, 0, 0)),
        ],
        out_specs=pl.BlockSpec((cg, rows, TBLK), lambda c: (c, 0, 0)),
        out_shape=jax.ShapeDtypeStruct((C, rows, TBLK), jnp.bfloat16),
        scratch_shapes=(
            [pltpu.VMEM((SHIFT_ROWS, n2), jnp.bfloat16)] * HY_UNROLL
            + [pltpu.VMEM((rows, TBLK), jnp.float32)] * HY_UNROLL),
        compiler_params=pltpu.CompilerParams(
            dimension_semantics=("arbitrary",), vmem_limit_bytes=VMEM_LIMIT),
        name="hyena_conv",
    )(kt, hyena_d[:, None], v_rows)


def _layer(x, pre_g, w_in, w_short, b_short, w_f1, b_f1, w_f2, b_f2, w_f3, b_f3, w_f4,
           sin_freq, hyena_d, attn_sink, w_out, post_g):
    B, L, _ = x.shape
    nblk = L // TBLK
    kt = _hyena_filter(L, w_f1, b_f1, w_f2, b_f2, w_f3, b_f3, w_f4, sin_freq)
    *v_parts, x0g, za, vt = _inproj(x, pre_g, w_in.astype(jnp.bfloat16), w_short, b_short)
    pc = HY_PART_CH
    yh = []
    for part, v in enumerate(v_parts):
        v_rows = v.reshape(B, nblk, TBLK, pc).transpose(3, 1, 0, 2).reshape(pc, nblk * B, TBLK)
        y_rows = _hyena_conv(kt, hyena_d, v_rows, part)
        yh.append(y_rows.reshape(pc, nblk, B, TBLK).transpose(2, 1, 3, 0).reshape(B, L, pc))
    return _back(za, vt, attn_sink, x, yh, x0g, w_out.astype(jnp.bfloat16), post_g)


def kernel(x, pre_g, w_in, w_short, b_short, w_f1, b_f1, w_f2, b_f2, w_f3, b_f3, w_f4, sin_freq, hyena_d, attn_sink, w_out, post_g):
    depth = pre_g.shape[0]
    for l in range(depth):
        x = _layer(x, pre_g[l], w_in[l], w_short[l], b_short[l], w_f1[l], b_f1[l], w_f2[l],
                   b_f2[l], w_f3[l], b_f3[l], w_f4[l], sin_freq[l], hyena_d[l], attn_sink[l],
                   w_out[l], post_g[l])
    return x
```
